```python
import math
import jax
import jax.numpy as jnp
from jax import lax
import numpy as np

D_MODEL = 2048
BATCH = 8
SEQ = 2048
DEPTH = 2

MIX_W = 1024
N_BRANCH = 3
HG_HEADS = 8
HG_DK = 128
HG_DV = 128
HG_CHUNK = 16
F_FLOOR = 1e-30
HY_W = 1024
HY_SHORT = 3
HY_EMB = 33
HY_BANDS = (HY_EMB - 1) // 2
HY_ORDER = 64
HY_INNER = 2
HY_FAST_DECAY = 0.3
HY_SLOW_DECAY = 1.5
HY_TARGET = 1e-2
HY_FILTER_SCALE = 0.05
ATT_HEADS = 16
ATT_KV_HEADS = 2
ATT_DH = 64
WINDOW = 128
ATT_BLOCK = 128
REL_BUCKETS = 32
REL_MAX_DIST = 128
MASK_VALUE = -1e30
N_EXPERTS = 32
TOP_K = 4
D_FF = 2048
SWIGLU_ALPHA = 1.702
SWIGLU_LIMIT = 7.0
MOE_BLOCK = 128
LN_EPS = 1e-5
RMS_EPS = 1e-6
DEEPNORM_ALPHA = (2 * DEPTH) ** 0.25
DEEPNORM_BETA = (8 * DEPTH) ** -0.25

HG_KW = HG_HEADS * HG_DK
HG_VW = HG_HEADS * HG_DV
ATT_QW = ATT_HEADS * ATT_DH
ATT_KVW = ATT_KV_HEADS * ATT_DH
IN_SIZES = (HG_KW, HG_KW, HG_KW, HG_VW, HG_VW, 3 * HY_W, ATT_QW, ATT_KVW, ATT_KVW, N_BRANCH * D_MODEL)
IN_COLS = sum(IN_SIZES)
IN_SPLITS = tuple(sum(IN_SIZES[:i + 1]) for i in range(len(IN_SIZES) - 1))

kernel_name = 'hybrid_hgrn2_hyena_swa_moe_encoder'


def layer_norm(x, g, b):
    xf = x.astype(jnp.float32)
    mu = jnp.mean(xf, axis=-1, keepdims=True)
    var = jnp.mean(jnp.square(xf - mu), axis=-1, keepdims=True)
    return ((xf - mu) * lax.rsqrt(var + LN_EPS) * g + b).astype(x.dtype)


def hgrn2_lower_bound(lb_table, layer):
    p = jax.nn.softmax(lb_table.astype(jnp.float32), axis=0)
    return jnp.cumsum(p, axis=0)[layer] - p[0]


def hgrn2_forget(z, lb):
    z = z.astype(jnp.float32)
    f = lb + (1.0 - lb) * jax.nn.sigmoid(z)
    return jnp.log(jnp.maximum(f, F_FLOOR)), (1.0 - lb) * jax.nn.sigmoid(-z)


def gla_chunked(q, k, v, log_f):
    B, L, H, DK = q.shape
    DV = v.shape[-1]
    N = L // HG_CHUNK
    q, k, v, log_f = (a.reshape(B, N, HG_CHUNK, H, a.shape[-1]) for a in (q, k, v, log_f))
    cum = jnp.cumsum(log_f, axis=2)
    last = cum[:, :, -1]
    past = jnp.tril(jnp.ones((HG_CHUNK, HG_CHUNK), bool))[None, None, :, :, None, None]
    dlog = cum[:, :, :, None] - cum[:, :, None, :]
    decay = jnp.where(past, jnp.exp(jnp.minimum(dlog, 0.0)), 0.0)
    scores = jnp.einsum('bntshk,bnshk->bnhts', q[:, :, :, None] * decay, k)
    o_intra = jnp.einsum('bnhts,bnshv->bnthv', scores, v)
    chunk_kv = jnp.einsum('bnshk,bnshv->bnhkv', k * jnp.exp(last[:, :, None] - cum), v)

    def carry_state(S, inp):
        dec, kv = inp
        return dec[..., None] * S + kv, S

    _, S_prev = lax.scan(carry_state, jnp.zeros((B, H, DK, DV), q.dtype),
                         (jnp.moveaxis(jnp.exp(last), 1, 0), jnp.moveaxis(chunk_kv, 1, 0)))
    S_prev = jnp.moveaxis(S_prev, 0, 1)
    o_inter = jnp.einsum('bnthk,bnhkv->bnthv', q * jnp.exp(cum), S_prev)
    return (o_intra + o_inter).reshape(B, L, H, DV)


def hgrn2_mixer(q, f_fwd, f_bwd, i, og, lb, norm_g):
    B, L, _ = q.shape
    heads = lambda a, d: a.reshape(B, L, HG_HEADS, d).astype(jnp.float32)
    qh = jax.nn.silu(heads(q, HG_DK)) * HG_DK ** -0.5
    vh = heads(i, HG_DV)
    lb_f, lb_b = jnp.split(lb, 2)
    logf_f, kf = hgrn2_forget(heads(f_fwd, HG_DK), lb_f.reshape(HG_HEADS, HG_DK))
    logf_b, kb = hgrn2_forget(heads(f_bwd, HG_DK), lb_b.reshape(HG_HEADS, HG_DK))
    flip = lambda a: jnp.flip(a, axis=1)
    o_f = gla_chunked(qh, kf, vh, logf_f)
    o_b = flip(gla_chunked(flip(qh), flip(kb), flip(vh), flip(logf_b)))
    o = o_f + o_b
    o = o * lax.rsqrt(jnp.mean(jnp.square(o), axis=-1, keepdims=True) + RMS_EPS) \
        * norm_g.astype(jnp.float32).reshape(HG_HEADS, HG_DV)
    o = o.reshape(B, L, HG_VW) * jax.nn.silu(og.astype(jnp.float32))
    return o.astype(q.dtype)


def short_conv(x, w, b):
    C = x.shape[-1]
    pad = HY_SHORT // 2
    y = lax.conv_general_dilated(x, w[:, None, :].astype(x.dtype), window_strides=(1,),
                                 padding=[(pad, pad)], dimension_numbers=('NWC', 'WIO', 'NWC'),
                                 feature_group_count=C)
    return y + b


def hyena_filters(L, w1, b1, w2, b2, freq, w3):
    t = jnp.linspace(0.0, 1.0, L, dtype=jnp.float32)[:, None]
    w = 2.0 * math.pi * jnp.arange(L, dtype=jnp.float32)[:, None] / L
    f = jnp.linspace(1e-4, HY_BANDS - 1, HY_BANDS, dtype=jnp.float32)[None]
    z = jnp.concatenate([t, jnp.cos(f * w), -jnp.sin(f * w)], axis=-1)
    h = jnp.sin(freq * (z @ w1 + b1))
    for j in range(HY_INNER):
        h = jnp.sin(freq * (h @ w2[j] + b2[j]))
    h = (h @ w3).astype(jnp.float32)
    max_decay = math.log(HY_TARGET) / HY_FAST_DECAY
    min_decay = math.log(HY_TARGET) / HY_SLOW_DECAY
    deltas = jnp.linspace(min_decay, max_decay, HY_W, dtype=jnp.float32)
    window = jnp.exp(-t * jnp.abs(deltas))
    h = h * jnp.tile(window, (1, 2))
    return h[:, :HY_W], h[:, HY_W:]


def bidirectional_long_conv(u, h_fwd, h_bwd):
    B, L, C = u.shape
    kern = jnp.concatenate([h_fwd, jnp.zeros((1, C), jnp.float32), h_bwd[:0:-1]], axis=0)
    U = jnp.fft.rfft(u, n=2 * L, axis=1)
    K = jnp.fft.rfft(kern, n=2 * L, axis=0)
    return jnp.fft.irfft(U * K[None], n=2 * L, axis=1)[:, :L]


def hyena_mixer(proj, conv_w, conv_b, w1, b1, w2, b2, freq, w3, skip):
    L = proj.shape[1]
    x0, x1, v = jnp.split(short_conv(proj, conv_w, conv_b), 3, axis=-1)
    h_f, h_b = hyena_filters(L, w1, b1, w2, b2, freq, w3)
    u = (x1 * v).astype(jnp.float32)
    y = bidirectional_long_conv(u, h_f, h_b) + u * skip.astype(jnp.float32)
    return (x0.astype(jnp.float32) * y).astype(proj.dtype)


def t5_relative_bucket(rel):
    half = REL_BUCKETS // 2
    max_exact = half // 2
    bucket = (rel > 0).astype(jnp.int32) * half
    n = jnp.abs(rel)
    n_safe = jnp.maximum(n, 1).astype(jnp.float32)
    large = max_exact + (jnp.log(n_safe / max_exact) / math.log(REL_MAX_DIST / max_exact)
                         * (half - max_exact)).astype(jnp.int32)
    large = jnp.clip(large, 0, half - 1)
    return bucket + jnp.where(n < max_exact, n, large)


def window_attention(q, k, v, sink, rel_bias):
    B, L, _ = q.shape
    G = ATT_HEADS // ATT_KV_HEADS
    W = ATT_BLOCK
    nb = L // W
    qb = q.reshape(B, nb, W, ATT_KV_HEADS, G, ATT_DH)

    def band(a):
        a = jnp.pad(a.reshape(B, L, ATT_KV_HEADS, ATT_DH), ((0, 0), (W, W), (0, 0), (0, 0)))
        a = a.reshape(B, nb + 2, W, ATT_KV_HEADS, ATT_DH)
        return jnp.concatenate([a[:, :-2], a[:, 1:-1], a[:, 2:]], axis=2)

    kb, vb = band(k), band(v)
    s = jnp.einsum('bnqhgd,bnshd->bnhgqs', qb, kb).astype(jnp.float32) * ATT_DH ** -0.5
    kofs = jnp.arange(3 * W, dtype=jnp.int32)[None, :] - W
    rel = kofs - jnp.arange(W, dtype=jnp.int32)[:, None]
    bias = jnp.transpose(rel_bias[t5_relative_bucket(rel)], (2, 0, 1))
    bias = bias.reshape(ATT_KV_HEADS, G, W, 3 * W).astype(jnp.float32)
    kpos = jnp.arange(nb, dtype=jnp.int32)[:, None] * W + kofs
    valid = (jnp.abs(rel) <= WINDOW)[None] & ((kpos >= 0) & (kpos < L))[:, None, :]
    s = jnp.where(valid[None, :, None, None], s + bias, MASK_VALUE)
    sk = sink.astype(jnp.float32).reshape(ATT_KV_HEADS, G)[:, :, None]
    m = jnp.maximum(jnp.max(s, axis=-1), sk)
    p = jnp.exp(s - m[..., None])
    denom = jnp.sum(p, axis=-1) + jnp.exp(sk - m)
    o = jnp.einsum('bnhgqs,bnshd->bnqhgd', p, vb.astype(jnp.float32))
    o = o / jnp.moveaxis(denom, -1, 2)[..., None]
    return o.reshape(B, L, ATT_QW).astype(q.dtype)


def mixer_sublayer(h, w_in, lb, hg_norm_g, conv_w, conv_b, fw1, fb1, fw2, fb2, ffreq, fw3, skip,
                   sink, rel_bias, w_branch, w_out):
    B, L, D = h.shape
    proj = h @ w_in
    hq, hff, hfb, hi, hog, hy, aq, ak, av, gates = jnp.split(proj, IN_SPLITS, axis=-1)
    o_hg = hgrn2_mixer(hq, hff, hfb, hi, hog, lb, hg_norm_g)
    o_hy = hyena_mixer(hy, conv_w, conv_b, fw1, fb1, fw2, fb2, ffreq, fw3, skip)
    o_at = window_attention(aq, ak, av, sink, rel_bias)
    branches = jnp.stack([o_hg, o_hy, o_at], axis=2)
    merged = jnp.einsum('blnc,ncd->blnd', branches, w_branch)
    g = jax.nn.sigmoid(gates.reshape(B, L, N_BRANCH, D).astype(jnp.float32))
    y = jnp.sum(g * merged.astype(jnp.float32), axis=2).astype(h.dtype)
    return y @ w_out


def routed_moe(h, layer, router_w, router_b, w_gate_up, b_gate_up, w_down, b_down):
    B, L, D = h.shape
    T = B * L
    TK = T * TOP_K
    xf = h.reshape(T, D)
    logits = (xf @ router_w[layer] + router_b[layer]).astype(jnp.float32)
    top_val, top_idx = lax.top_k(logits, TOP_K)
    gate = jax.nn.softmax(top_val, axis=-1)
    expert = top_idx.reshape(TK).astype(jnp.int32)
    token = jnp.arange(TK, dtype=jnp.int32) // TOP_K
    weight = gate.reshape(TK)
    order = jnp.argsort(expert)
    s_expert, s_token, s_weight = expert[order], token[order], weight[order]
    counts = jnp.zeros((N_EXPERTS,), jnp.int32).at[expert].add(1)
    padded = (counts + MOE_BLOCK - 1) // MOE_BLOCK * MOE_BLOCK
    p_end = jnp.cumsum(padded)
    p_start = p_end - padded
    s_start = jnp.cumsum(counts) - counts
    dest = p_start[s_expert] + jnp.arange(TK, dtype=jnp.int32) - s_start[s_expert]
    n_blocks = -(-TK // MOE_BLOCK) + N_EXPERTS
    P = n_blocks * MOE_BLOCK
    tok_pad = jnp.full((P,), T, jnp.int32).at[dest].set(s_token)
    w_pad = jnp.zeros((P,), jnp.float32).at[dest].set(s_weight)
    block_start = jnp.arange(n_blocks, dtype=jnp.int32) * MOE_BLOCK
    block_expert = jnp.minimum(jnp.searchsorted(p_end, block_start, side='right'),
                               N_EXPERTS - 1).astype(jnp.int32)
    x_pad = jnp.concatenate([xf, jnp.zeros((1, D), xf.dtype)], axis=0)
    x_blocks = x_pad[tok_pad].reshape(n_blocks, MOE_BLOCK, D)

    def expert_block(args):
        xb, e = args
        gu = xb @ w_gate_up[layer, e] + b_gate_up[layer, e]
        g, u = jnp.split(gu, 2, axis=-1)
        g = jnp.minimum(g, SWIGLU_LIMIT)
        u = jnp.clip(u, -SWIGLU_LIMIT, SWIGLU_LIMIT)
        act = (u + 1.0) * g * jax.nn.sigmoid(SWIGLU_ALPHA * g)
        return act @ w_down[layer, e] + b_down[layer, e]

    y_blocks = lax.map(expert_block, (x_blocks, block_expert))
    y = y_blocks.reshape(P, D).astype(jnp.float32) * w_pad[:, None]
    y = jax.ops.segment_sum(y, tok_pad, num_segments=T + 1)[:T]
    return y.reshape(B, L, D).astype(h.dtype)


def setup_inputs(seed: int = 0) -> dict:
    key = jax.random.key(seed)
    ks = iter(jax.random.split(key, 32))
    nrm = lambda shape, scale: jax.random.normal(next(ks), shape, jnp.float32) * scale
    return {
        'x': nrm((BATCH, SEQ, D_MODEL), 1.0),
        'ln_in_g': 1.0 + nrm((D_MODEL,), 0.02),
        'ln_in_b': nrm((D_MODEL,), 0.02),
        'w_in': nrm((DEPTH, D_MODEL, IN_COLS), D_MODEL ** -0.5),
        'hg_lower_bound': 1.0 + nrm((DEPTH, 2 * HG_KW), 0.1),
        'hg_norm_g': 1.0 + nrm((DEPTH, HG_VW), 0.02),
        'hy_conv_w': nrm((DEPTH, HY_SHORT, 3 * HY_W), HY_SHORT ** -0.5),
        'hy_conv_b': nrm((DEPTH, 3 * HY_W), 0.02),
        'hy_filt_w1': nrm((DEPTH, HY_EMB, HY_ORDER), HY_EMB ** -0.5),
        'hy_filt_b1': nrm((DEPTH, HY_ORDER), 0.1),
        'hy_filt_w2': nrm((DEPTH, HY_INNER, HY_ORDER, HY_ORDER), HY_ORDER ** -0.5),
        'hy_filt_b2': nrm((DEPTH, HY_INNER, HY_ORDER), 0.1),
        'hy_filt_freq': 1.0 + nrm((DEPTH, HY_ORDER), 0.1),
        'hy_filt_w3': nrm((DEPTH, HY_ORDER, 2 * HY_W), HY_FILTER_SCALE * HY_ORDER ** -0.5),
        'hy_skip': nrm((DEPTH, HY_W), 1.0),
        'att_sink': nrm((DEPTH, ATT_HEADS), 1.0),
        'rel_bias': nrm((REL_BUCKETS, ATT_HEADS), 0.1),
        'w_branch': nrm((DEPTH, N_BRANCH, MIX_W, D_MODEL), MIX_W ** -0.5),
        'w_out': nrm((DEPTH, D_MODEL, D_MODEL), DEEPNORM_BETA * D_MODEL ** -0.5),
        'ln_mix_g': 1.0 + nrm((DEPTH, D_MODEL), 0.02),
        'ln_mix_b': nrm((DEPTH, D_MODEL), 0.02),
        'router_w': nrm((DEPTH, D_MODEL, N_EXPERTS), D_MODEL ** -0.5),
        'router_b': nrm((DEPTH, N_EXPERTS), 0.01),
        'w_gate_up': nrm((DEPTH, N_EXPERTS, D_MODEL, 2 * D_FF), D_MODEL ** -0.5),
        'b_gate_up': nrm((DEPTH, N_EXPERTS, 2 * D_FF), 0.01),
        'w_down': nrm((DEPTH, N_EXPERTS, D_FF, D_MODEL), DEEPNORM_BETA * D_FF ** -0.5),
        'b_down': nrm((DEPTH, N_EXPERTS, D_MODEL), 0.01),
        'ln_moe_g': 1.0 + nrm((DEPTH, D_MODEL), 0.02),
        'ln_moe_b': nrm((DEPTH, D_MODEL), 0.02),
    }


def reference(x, ln_in_g, ln_in_b, w_in, hg_lower_bound, hg_norm_g, hy_conv_w, hy_conv_b,
              hy_filt_w1, hy_filt_b1, hy_filt_w2, hy_filt_b2, hy_filt_freq, hy_filt_w3, hy_skip,
              att_sink, rel_bias, w_branch, w_out, ln_mix_g, ln_mix_b, router_w, router_b,
              w_gate_up, b_gate_up, w_down, b_down, ln_moe_g, ln_moe_b):
    h = layer_norm(x, ln_in_g, ln_in_b)
    for layer in range(DEPTH):
        lb = hgrn2_lower_bound(hg_lower_bound, layer)
        mix = mixer_sublayer(h, w_in[layer], lb, hg_norm_g[layer], hy_conv_w[layer], hy_conv_b[layer],
                             hy_filt_w1[layer], hy_filt_b1[layer], hy_filt_w2[layer], hy_filt_b2[layer],
                             hy_filt_freq[layer], hy_filt_w3[layer], hy_skip[layer], att_sink[layer],
                             rel_bias, w_branch[layer], w_out[layer])
        h = layer_norm(DEEPNORM_ALPHA * h + mix, ln_mix_g[layer], ln_mix_b[layer])
        ffn = routed_moe(h, layer, router_w, router_b, w_gate_up, b_gate_up, w_down, b_down)
        h = layer_norm(DEEPNORM_ALPHA * h + ffn, ln_moe_g[layer], ln_moe_b[layer])
    return h
```

```python
import functools
import math

import jax
import jax.numpy as jnp
from jax import lax
from jax.experimental import pallas as pl
from jax.experimental.pallas import tpu as pltpu

F32 = jnp.float32
BF16 = jnp.bfloat16

D_MODEL = 2048
DEPTH = 2
MIX_W = 1024
N_BRANCH = 3
HG_HEADS = 8
HG_DK = 128
HG_DV = 128
HG_CHUNK = 16
F_FLOOR = 1e-30
HY_W = 1024
HY_SHORT = 3
HY_EMB = 33
HY_BANDS = (HY_EMB - 1) // 2
HY_ORDER = 64
HY_INNER = 2
HY_FAST_DECAY = 0.3
HY_SLOW_DECAY = 1.5
HY_TARGET = 1e-2
ATT_HEADS = 16
ATT_KV_HEADS = 2
ATT_DH = 64
WINDOW = 128
ATT_BLOCK = 128
REL_BUCKETS = 32
REL_MAX_DIST = 128
MASK_VALUE = -1e30
N_EXPERTS = 32
TOP_K = 4
D_FF = 2048
SWIGLU_ALPHA = 1.702
SWIGLU_LIMIT = 7.0
LN_EPS = 1e-5
RMS_EPS = 1e-6
DEEPNORM_ALPHA = (2 * DEPTH) ** 0.25

HG_KW = HG_HEADS * HG_DK
HG_VW = HG_HEADS * HG_DV
ATT_QW = ATT_HEADS * ATT_DH
ATT_KVW = ATT_KV_HEADS * ATT_DH
IN_SIZES = (HG_KW, HG_KW, HG_KW, HG_VW, HG_VW, 3 * HY_W, ATT_QW, ATT_KVW, ATT_KVW, N_BRANCH * D_MODEL)
IN_COLS = sum(IN_SIZES)
OFF_HQ, OFF_HFF, OFF_HFB, OFF_HI, OFF_HOG = 0, HG_KW, 2 * HG_KW, 3 * HG_KW, 3 * HG_KW + HG_VW
OFF_HY = OFF_HOG + HG_VW
OFF_AQ = OFF_HY + 3 * HY_W
OFF_AK = OFF_AQ + ATT_QW
OFF_AV = OFF_AK + ATT_KVW
OFF_GATES = OFF_AV + ATT_KVW
MAIN_COLS = OFF_GATES

V7X_LANES = 128
V7X_VMEM_BYTES = 64 * 1024 * 1024

LN_ROWS = 512
PROJ_TM = 2048
PROJ_TN = 256
HY_TC = 512
MERGE_TM = 256
MOE_TM = 1024
MOE_SUB = 512
MOE_TF = 256
TOK_TILE = 256
ROUTE_LANES = 128


def _params(semantics, vmem_mb):
    return pltpu.CompilerParams(dimension_semantics=semantics, vmem_limit_bytes=vmem_mb * 1024 * 1024)


def _layer_norm_rows(x, g, b):
    mu = jnp.mean(x, axis=-1, keepdims=True)
    xc = x - mu
    var = jnp.mean(xc * xc, axis=-1, keepdims=True)
    return xc * lax.rsqrt(var + LN_EPS) * g + b


def _pack_halves(x):
    c = x.shape[-1] // 2
    lo = pltpu.bitcast(x[:, :c].astype(BF16).astype(F32), jnp.uint32)
    hi = pltpu.bitcast(x[:, c:].astype(BF16).astype(F32), jnp.uint32)
    return (lo >> 16) | (hi & jnp.uint32(0xFFFF0000))


def _unpack_halves(w):
    lo = pltpu.bitcast(w << 16, F32)
    hi = pltpu.bitcast(w & jnp.uint32(0xFFFF0000), F32)
    return lo, hi


def _ln_in_kernel(x_ref, g_ref, b_ref, h_ref, hb_ref):
    y = _layer_norm_rows(x_ref[...], g_ref[...], b_ref[...])
    h_ref[...] = y
    hb_ref[...] = y.astype(BF16)


def _ln_in(x2, g, b):
    t, d = x2.shape
    tm = min(LN_ROWS, t)
    row = pl.BlockSpec((tm, d), lambda i: (i, 0))
    vec = pl.BlockSpec((1, d), lambda i: (0, 0))
    return pl.pallas_call(
        _ln_in_kernel,
        grid=(t // tm,),
        in_specs=[row, vec, vec],
        out_specs=[row, row],
        out_shape=[jax.ShapeDtypeStruct((t, d), F32), jax.ShapeDtypeStruct((t, d), BF16)],
        compiler_params=_params(("arbitrary",), 32),
        name="ln_in",
    )(x2, g.reshape(1, d), b.reshape(1, d))


def _inproj_kernel(a_ref, w_ref, o_ref, wb_ref):
    @pl.when(pl.program_id(1) == 0)
    def _():
        wb_ref[...] = w_ref[...].astype(BF16)

    o_ref[...] = jnp.dot(a_ref[...], wb_ref[...], preferred_element_type=F32).astype(o_ref.dtype)


def _inproj(hb, w_in, layer, col0, ncols):
    t, d = hb.shape
    tm = min(PROJ_TM, t)
    tn = PROJ_TN
    cb0 = col0 // tn
    return pl.pallas_call(
        _inproj_kernel,
        grid=(ncols // tn, t // tm),
        in_specs=[
            pl.BlockSpec((tm, d), lambda j, i: (i, 0)),
            pl.BlockSpec((None, d, tn), lambda j, i: (layer, 0, cb0 + j)),
        ],
        out_specs=pl.BlockSpec((tm, tn), lambda j, i: (i, j)),
        out_shape=jax.ShapeDtypeStruct((t, ncols), BF16),
        scratch_shapes=[pltpu.VMEM((d, tn), BF16)],
        compiler_params=_params(("arbitrary", "arbitrary"), 48),
        name="inproj",
    )(hb, w_in)


def _hgrn_kernel(q_ref, ff_ref, fb_ref, i_ref, og_ref, lbf_ref, lbb_ref, g_ref, o_ref,
                 stage, qh_t, v_t, cum_t, kk_t, o_t, qtil_c, ktil_c, oint_c, dec_c, *, seq):
    c = HG_CHUNK
    nc = seq // c
    half = max(nc // 2, 8)
    n_half = nc // half

    def rows(j):
        return pl.ds(j, nc, stride=c)

    stage[0] = q_ref[...].astype(F32)
    stage[1] = ff_ref[...].astype(F32)
    stage[2] = fb_ref[...].astype(F32)
    stage[3] = i_ref[...].astype(F32)
    lbs = (lbf_ref[...], lbb_ref[...])

    for j in range(c):
        qj = stage[0, rows(j), :]
        qh_t[j] = qj * jax.nn.sigmoid(qj) * (HG_DK ** -0.5)
        v_t[j] = stage[3, rows(j), :]
        for d in range(2):
            z = stage[1 + d, rows(j), :]
            lb = lbs[d]
            f = lb + (1.0 - lb) * jax.nn.sigmoid(z)
            cum_t[d, j] = jnp.log(jnp.maximum(f, F_FLOOR))
            kk_t[d, j] = (1.0 - lb) * jax.nn.sigmoid(-z)

    lasts = []
    for d in range(2):
        acc = jnp.zeros((nc, HG_DK), F32)
        for j in (range(c) if d == 0 else reversed(range(c))):
            acc = acc + cum_t[d, j]
            cum_t[d, j] = acc
        lasts.append(acc)
        dec_c[d] = jnp.exp(acc)
    for d in range(2):
        for j in range(c):
            cj = cum_t[d, j]
            qtil_c[d, rows(j), :] = qh_t[j] * jnp.exp(cj)
            ktil_c[d, rows(j), :] = kk_t[d, j] * jnp.exp(lasts[d] - cj)

    o_t[...] = jnp.zeros(o_t.shape, F32)
    ones_b = jnp.ones((HG_DK, HG_DV), BF16)

    def pair_body(it, carry):
        d = it // n_half
        hs = pl.multiple_of((it % n_half) * half, 8)
        sl = pl.ds(hs, half)
        for t in range(c):
            rt = t + d * (c - 1 - 2 * t)
            ct = cum_t[d, rt, sl, :]
            qt = qh_t[rt, sl, :]
            acc = jnp.zeros((half, HG_DV), F32)
            for s in range(t + 1):
                rs = s + d * (c - 1 - 2 * s)
                e = jnp.exp(jnp.minimum(ct - cum_t[d, rs, sl, :], 0.0))
                a = (qt * e * kk_t[d, rs, sl, :]).astype(BF16)
                p = jnp.dot(a, ones_b, preferred_element_type=F32)
                acc = acc + p * v_t[rs, sl, :]
            o_t[rt, sl, :] += acc
        return carry

    lax.fori_loop(0, 2 * n_half, pair_body, 0)

    def rec_body(idx, carry):
        new = []
        for d in range(2):
            s = carry[d]
            n = idx if d == 0 else nc - 1 - idx
            r0 = pl.multiple_of(n * c, c)
            qn = qtil_c[d, pl.ds(r0, c), :].astype(BF16)
            kn = ktil_c[d, pl.ds(r0, c), :].astype(BF16)
            vn = i_ref[pl.ds(r0, c), :]
            oint_c[d, pl.ds(r0, c), :] = lax.dot_general(
                qn, s.astype(BF16), (((1,), (1,)), ((), ())), preferred_element_type=F32)
            kv = lax.dot_general(vn, kn, (((0,), (0,)), ((), ())), preferred_element_type=F32)
            new.append(dec_c[d, pl.ds(n, 1), :] * s + kv)
        return tuple(new)

    zero_state = jnp.zeros((HG_DV, HG_DK), F32)
    lax.fori_loop(0, nc, rec_body, (zero_state, zero_state))

    for j in range(c):
        stage[0, rows(j), :] = o_t[j]
    o = stage[0] + oint_c[0] + oint_c[1]
    o = o * lax.rsqrt(jnp.mean(o * o, axis=-1, keepdims=True) + RMS_EPS) * g_ref[...]
    og = og_ref[...].astype(F32)
    o_ref[...] = (o * (og * jax.nn.sigmoid(og))).astype(o_ref.dtype)


def _hgrn(proj, lb, norm_g, batch, seq):
    t = batch * seq
    nc = seq // HG_CHUNK

    def col(off):
        return pl.BlockSpec((seq, HG_DK), lambda b, h: (b, off // HG_DK + h))

    vec = pl.BlockSpec((None, 1, HG_DK), lambda b, h: (h, 0, 0))
    lbf = lb[:HG_KW].reshape(HG_HEADS, 1, HG_DK)
    lbb = lb[HG_KW:].reshape(HG_HEADS, 1, HG_DK)
    g = norm_g.reshape(HG_HEADS, 1, HG_DV)
    slab = (HG_CHUNK, nc, HG_DK)
    return pl.pallas_call(
        functools.partial(_hgrn_kernel, seq=seq),
        grid=(batch, HG_HEADS),
        in_specs=[col(OFF_HQ), col(OFF_HFF), col(OFF_HFB), col(OFF_HI), col(OFF_HOG), vec, vec, vec],
        out_specs=pl.BlockSpec((seq, HG_DV), lambda b, h: (b, h)),
        out_shape=jax.ShapeDtypeStruct((t, HG_VW), BF16),
        scratch_shapes=[
            pltpu.VMEM((4, seq, HG_DK), F32),
            pltpu.VMEM(slab, F32),
            pltpu.VMEM(slab, F32),
            pltpu.VMEM((2,) + slab, F32),
            pltpu.VMEM((2,) + slab, F32),
            pltpu.VMEM(slab, F32),
            pltpu.VMEM((2, seq, HG_DK), F32),
            pltpu.VMEM((2, seq, HG_DK), F32),
            pltpu.VMEM((2, seq, HG_DV), F32),
            pltpu.VMEM((2, nc, HG_DK), F32),
        ],
        compiler_params=_params(("arbitrary", "arbitrary"), 48),
        name="hgrn2",
    )(proj, proj, proj, proj, proj, lbf, lbb, g)


def _attn_kernel(sink_ref, q_ref, kp_ref, ko_ref, kn_ref, vp_ref, vo_ref, vn_ref, bias_ref, o_ref, *, seq):
    n = pl.program_id(1)
    w = ATT_BLOCK
    group = ATT_HEADS // ATT_KV_HEADS
    kband = jnp.concatenate([kp_ref[...], ko_ref[...], kn_ref[...]], axis=0)
    vband = jnp.concatenate([vp_ref[...], vo_ref[...], vn_ref[...]], axis=0)
    kpos = n * w - w + lax.broadcasted_iota(jnp.int32, (1, 3 * w), 1)
    posmask = jnp.where((kpos >= 0) & (kpos < seq), 0.0, MASK_VALUE).astype(F32)
    q = q_ref[...]
    outs = []
    for h in range(ATT_HEADS):
        g = h // group
        qh = q[:, h * ATT_DH:(h + 1) * ATT_DH]
        kg = kband[:, g * ATT_DH:(g + 1) * ATT_DH]
        vg = vband[:, g * ATT_DH:(g + 1) * ATT_DH]
        s = lax.dot_general(qh, kg, (((1,), (1,)), ((), ())), preferred_element_type=F32) * (ATT_DH ** -0.5)
        s = s + bias_ref[h] + posmask
        sk = sink_ref[h]
        m = jnp.maximum(jnp.max(s, axis=-1, keepdims=True), sk)
        p = jnp.exp(s - m)
        denom = jnp.sum(p, axis=-1, keepdims=True) + jnp.exp(sk - m)
        outs.append(jnp.dot(p.astype(BF16), vg, preferred_element_type=F32) / denom)
    o_ref[...] = jnp.concatenate(outs, axis=1).astype(o_ref.dtype)


def _t5_relative_bucket(rel):
    half = REL_BUCKETS // 2
    max_exact = half // 2
    bucket = (rel > 0).astype(jnp.int32) * half
    n = jnp.abs(rel)
    n_safe = jnp.maximum(n, 1).astype(F32)
    large = max_exact + (jnp.log(n_safe / max_exact) / math.log(REL_MAX_DIST / max_exact)
                         * (half - max_exact)).astype(jnp.int32)
    large = jnp.clip(large, 0, half - 1)
    return bucket + jnp.where(n < max_exact, n, large)


def _attn_bias_table(rel_bias):
    w = ATT_BLOCK
    kofs = jnp.arange(3 * w, dtype=jnp.int32)[None, :] - w
    rel = kofs - jnp.arange(w, dtype=jnp.int32)[:, None]
    bias = jnp.transpose(rel_bias[_t5_relative_bucket(rel)], (2, 0, 1)).astype(F32)
    return jnp.where((jnp.abs(rel) <= WINDOW)[None], bias, MASK_VALUE)


def _attn(proj, sink, bias_tab, batch, seq):
    t = batch * seq
    w = ATT_BLOCK
    nb = seq // w
    kcol = OFF_AK // ATT_KVW
    vcol = OFF_AV // ATT_KVW

    def kv(col, delta):
        return pl.BlockSpec((w, ATT_KVW), lambda b, n: (b * nb + jnp.clip(n + delta, 0, nb - 1), col))

    return pl.pallas_call(
        functools.partial(_attn_kernel, seq=seq),
        grid=(batch, nb),
        in_specs=[
            pl.BlockSpec(memory_space=pltpu.SMEM),
            pl.BlockSpec((w, ATT_QW), lambda b, n: (b * nb + n, OFF_AQ // ATT_QW)),
            kv(kcol, -1), kv(kcol, 0), kv(kcol, 1),
            kv(vcol, -1), kv(vcol, 0), kv(vcol, 1),
            pl.BlockSpec((ATT_HEADS, w, 3 * w), lambda b, n: (0, 0, 0)),
        ],
        out_specs=pl.BlockSpec((w, ATT_QW), lambda b, n: (b * nb + n, 0)),
        out_shape=jax.ShapeDtypeStruct((t, ATT_QW), BF16),
        compiler_params=_params(("arbitrary", "arbitrary"), 32),
        name="win_attn",
    )(sink.astype(F32), proj, proj, proj, proj, proj, proj, proj, bias_tab)


def _dft_tables(seq):
    n = 2 * seq
    k = jnp.arange(seq, dtype=jnp.int32)[:, None]
    s = jnp.arange(seq, dtype=jnp.int32)[None, :]
    ang = ((k * s) % n).astype(F32) * (2.0 * math.pi / n)
    cm = jnp.cos(ang)
    sm = -jnp.sin(ang)
    nyq = jnp.where(s % 2 == 0, 1.0, -1.0).astype(F32)
    sm = jnp.where(k == 0, nyq, sm)
    f = jnp.concatenate([cm, sm], axis=0).astype(BF16)
    return f, f.T


def _hy_positions(seq):
    t = jnp.linspace(0.0, 1.0, seq, dtype=F32)[:, None]
    w = 2.0 * math.pi * jnp.arange(seq, dtype=F32)[:, None] / seq
    f = jnp.linspace(1e-4, HY_BANDS - 1, HY_BANDS, dtype=F32)[None]
    z = jnp.concatenate([t, jnp.cos(f * w), -jnp.sin(f * w)], axis=-1)
    z = jnp.pad(z, ((0, 0), (0, V7X_LANES - HY_EMB)))
    max_decay = math.log(HY_TARGET) / HY_FAST_DECAY
    min_decay = math.log(HY_TARGET) / HY_SLOW_DECAY
    deltas = jnp.linspace(min_decay, max_decay, HY_W, dtype=F32)
    window = jnp.exp(-t * jnp.abs(deltas))
    return z, window


def _hy_filter_kernel(z_ref, w1_ref, b1_ref, w2_ref, b2_ref, fr_ref, w3_ref, win_ref, h_ref):
    dot = functools.partial(jnp.dot, precision=lax.Precision.HIGHEST, preferred_element_type=F32)
    fr = fr_ref[...]
    h = jnp.sin(fr * (dot(z_ref[...], w1_ref[...]) + b1_ref[...]))
    for j in range(HY_INNER):
        h = jnp.sin(fr * (dot(h, w2_ref[j]) + b2_ref[j]))
    h = dot(h, w3_ref[...]) * win_ref[...]
    row = lax.broadcasted_iota(jnp.int32, h.shape, 0)
    backward = pl.program_id(0) >= pl.num_programs(0) // 2
    h_ref[...] = jnp.where((row == 0) & backward, 0.0, h).astype(h_ref.dtype)


def _hy_filters(z, window, w1, b1, w2, b2, freq, w3):
    seq = z.shape[0]
    tn = HY_TC
    per_dir = HY_W // tn
    full = lambda shape: pl.BlockSpec(shape, lambda j: (0,) * len(shape))
    w1p = jnp.pad(w1, ((0, V7X_LANES - HY_EMB), (0, 0)))
    return pl.pallas_call(
        _hy_filter_kernel,
        grid=(2 * per_dir,),
        in_specs=[
            full((seq, V7X_LANES)), full((V7X_LANES, HY_ORDER)), full((1, HY_ORDER)),
            full((HY_INNER, HY_ORDER, HY_ORDER)), full((HY_INNER, 1, HY_ORDER)), full((1, HY_ORDER)),
            pl.BlockSpec((HY_ORDER, tn), lambda j: (0, j)),
            pl.BlockSpec((seq, tn), lambda j: (0, j % per_dir)),
        ],
        out_specs=pl.BlockSpec((seq, tn), lambda j: (0, j)),
        out_shape=jax.ShapeDtypeStruct((seq, 2 * HY_W), BF16),
        compiler_params=_params(("arbitrary",), 32),
        name="hy_filter",
    )(z, w1p, b1.reshape(1, HY_ORDER), w2, b2.reshape(HY_INNER, 1, HY_ORDER), freq.reshape(1, HY_ORDER), w3, window)


def _mm_kernel(a_ref, b_ref, o_ref):
    o_ref[...] = jnp.dot(a_ref[...], b_ref[...], preferred_element_type=F32).astype(o_ref.dtype)


def _mm(a, b, tm, tn, out_dtype):
    m, k = a.shape
    n = b.shape[1]
    return pl.pallas_call(
        _mm_kernel,
        grid=(m // tm, n // tn),
        in_specs=[pl.BlockSpec((tm, k), lambda i, j: (i, 0)), pl.BlockSpec((k, tn), lambda i, j: (0, j))],
        out_specs=pl.BlockSpec((tm, tn), lambda i, j: (i, j)),
        out_shape=jax.ShapeDtypeStruct((m, n), out_dtype),
        compiler_params=_params(("arbitrary", "arbitrary"), 32),
        name="mm",
    )(a, b)


def _hy_spectrum(fmat, hcat, seq):
    spec = _mm(fmat, hcat, min(1024, 2 * seq), 512, F32)
    top, bot = spec[:seq], spec[seq:]
    kr = top[:, :HY_W] + top[:, HY_W:]
    ki = bot[:, :HY_W] - bot[:, HY_W:]
    n = 2 * seq
    first = (jnp.arange(seq) == 0)[:, None]
    pr = jnp.where(first, kr / n, kr * (2.0 / n))
    pi = jnp.where(first, 0.0, ki * (2.0 / n))
    nyq = (bot[:1, :HY_W] + bot[:1, HY_W:]) / n
    return pr, pi, nyq


def _short_conv(x, w_ref, b_ref):
    seq = x.shape[0]
    row = lax.broadcasted_iota(jnp.int32, x.shape, 0)
    prev = jnp.where(row == 0, 0.0, pltpu.roll(x, 1, 0))
    nxt = jnp.where(row == seq - 1, 0.0, pltpu.roll(x, seq - 1, 0))
    return w_ref[0:1, :] * prev + w_ref[1:2, :] * x + w_ref[2:3, :] * nxt + b_ref[...]


def _hy_fwd_kernel(x1_ref, v_ref, w1_ref, b1_ref, wv_ref, bv_ref, f_ref, pr_ref, pi_ref, nyq_ref, y_ref):
    seq = x1_ref.shape[0]
    x1 = _short_conv(x1_ref[...].astype(F32), w1_ref, b1_ref)
    v = _short_conv(v_ref[...].astype(F32), wv_ref, bv_ref)
    u = (x1 * v).astype(BF16)
    w = jnp.dot(f_ref[...], u, preferred_element_type=F32)
    a, b = w[:seq], w[seq:]
    pr, pi = pr_ref[...], pi_ref[...]
    row = lax.broadcasted_iota(jnp.int32, pr.shape, 0)
    pd = jnp.where(row == 0, nyq_ref[...], pr)
    y_ref[:seq, :] = (a * pr - b * pi).astype(y_ref.dtype)
    y_ref[seq:, :] = (a * pi + b * pd).astype(y_ref.dtype)


def _hy_inv_kernel(y_ref, x0_ref, x1_ref, v_ref, w0_ref, b0_ref, w1_ref, b1_ref, wv_ref, bv_ref, skip_ref,
                   g_ref, o_ref):
    y = jnp.dot(g_ref[...], y_ref[...], preferred_element_type=F32)
    x0 = _short_conv(x0_ref[...].astype(F32), w0_ref, b0_ref)
    x1 = _short_conv(x1_ref[...].astype(F32), w1_ref, b1_ref)
    v = _short_conv(v_ref[...].astype(F32), wv_ref, bv_ref)
    u = x1 * v
    o_ref[...] = (x0 * (y + u * skip_ref[...])).astype(o_ref.dtype)


def _hyena(proj, conv_w, conv_b, skip, fmat, gmat, pr, pi, nyq, batch, seq):
    t = batch * seq
    tc = HY_TC // 2
    nct = HY_W // tc
    conv_b2 = conv_b.reshape(1, 3 * HY_W)

    def xcol(part):
        return pl.BlockSpec((seq, tc), lambda c, b: (b, (OFF_HY + part * HY_W) // tc + c))

    def wcol(part):
        return pl.BlockSpec((HY_SHORT, tc), lambda c, b: (0, part * nct + c))

    def bcol(part):
        return pl.BlockSpec((1, tc), lambda c, b: (0, part * nct + c))

    chan = pl.BlockSpec((seq, tc), lambda c, b: (0, c))
    chan1 = pl.BlockSpec((1, tc), lambda c, b: (0, c))
    once = pl.Buffered(1)
    yspec = pl.BlockSpec((None, 2 * seq, tc), lambda c, b: (b, 0, c))
    yfreq = pl.pallas_call(
        _hy_fwd_kernel,
        grid=(nct, batch),
        in_specs=[xcol(1), xcol(2), wcol(1), bcol(1), wcol(2), bcol(2),
                  pl.BlockSpec((2 * seq, seq), lambda c, b: (0, 0), pipeline_mode=once),
                  chan, chan, chan1],
        out_specs=yspec,
        out_shape=jax.ShapeDtypeStruct((batch, 2 * seq, HY_W), BF16),
        compiler_params=_params(("arbitrary", "arbitrary"), 56),
        name="hy_fwd",
    )(proj, proj, conv_w, conv_b2, conv_w, conv_b2, fmat, pr, pi, nyq)
    return pl.pallas_call(
        _hy_inv_kernel,
        grid=(nct, batch),
        in_specs=[yspec, xcol(0), xcol(1), xcol(2), wcol(0), bcol(0), wcol(1), bcol(1), wcol(2), bcol(2), chan1,
                  pl.BlockSpec((seq, 2 * seq), lambda c, b: (0, 0), pipeline_mode=once)],
        out_specs=pl.BlockSpec((seq, tc), lambda c, b: (b, c)),
        out_shape=jax.ShapeDtypeStruct((t, HY_W), BF16),
        compiler_params=_params(("arbitrary", "arbitrary"), 56),
        name="hy_inv",
    )(yfreq, proj, proj, proj, conv_w, conv_b2, conv_w, conv_b2, conv_w, conv_b2, skip.reshape(1, HY_W), gmat)


def _merge_kernel(ohg_ref, ohy_ref, oat_ref, gates_ref, h_ref, wb_ref, wo_ref, lng_ref, lnb_ref, rw_ref, rb_ref,
                  hmid_ref, hp_ref, idx_ref, gate_ref):
    d = D_MODEL
    m = None
    for n, o_ref in enumerate((ohg_ref, ohy_ref, oat_ref)):
        br = jnp.dot(o_ref[...], wb_ref[n], preferred_element_type=F32)
        term = jax.nn.sigmoid(gates_ref[:, n * d:(n + 1) * d].astype(F32)) * br
        m = term if m is None else m + term
    y = jnp.dot(m.astype(BF16), wo_ref[...], preferred_element_type=F32)
    hn = _layer_norm_rows(DEEPNORM_ALPHA * h_ref[...] + y, lng_ref[...], lnb_ref[...])
    hmid_ref[...] = hn
    hp_ref[...] = _pack_halves(hn)

    logits = jnp.dot(hn, rw_ref[...], precision=lax.Precision.HIGHEST, preferred_element_type=F32) + rb_ref[...]
    lane = lax.broadcasted_iota(jnp.int32, logits.shape, 1)
    vals, idxs = [], []
    for _ in range(TOP_K):
        mx = jnp.max(logits, axis=-1, keepdims=True)
        ix = jnp.min(jnp.where(logits == mx, lane, ROUTE_LANES), axis=-1, keepdims=True)
        vals.append(mx)
        idxs.append(ix)
        logits = jnp.where(lane == ix, -jnp.inf, logits)
    exps = [jnp.exp(v - vals[0]) for v in vals]
    total = exps[0]
    for e in exps[1:]:
        total = total + e
    gate_out = jnp.zeros(logits.shape, F32)
    idx_out = jnp.zeros(logits.shape, jnp.int32)
    for r in range(TOP_K):
        gate_out = jnp.where(lane == r, exps[r] / total, gate_out)
        idx_out = jnp.where(lane == r, idxs[r], idx_out)
    gate_ref[...] = gate_out
    idx_ref[...] = idx_out


def _merge(o_hg, o_hy, o_at, gates, h, wb, wo, ln_g, ln_b, rw, rb):
    t, d = h.shape
    tm = min(MERGE_TM, t)
    once = pl.Buffered(1)
    row = lambda width: pl.BlockSpec((tm, width), lambda i: (i, 0))
    const = lambda shape: pl.BlockSpec(shape, lambda i: (0,) * len(shape), pipeline_mode=once)
    rwp = jnp.pad(rw, ((0, 0), (0, ROUTE_LANES - N_EXPERTS)))
    rbp = jnp.pad(rb, (0, ROUTE_LANES - N_EXPERTS), constant_values=MASK_VALUE).reshape(1, ROUTE_LANES)
    return pl.pallas_call(
        _merge_kernel,
        grid=(t // tm,),
        in_specs=[row(MIX_W), row(MIX_W), row(MIX_W), row(N_BRANCH * d), row(d),
                  const((N_BRANCH, MIX_W, d)), const((d, d)), const((1, d)), const((1, d)),
                  const((d, ROUTE_LANES)), const((1, ROUTE_LANES))],
        out_specs=[row(d), row(d // 2), row(ROUTE_LANES), row(ROUTE_LANES)],
        out_shape=[jax.ShapeDtypeStruct((t, d), F32), jax.ShapeDtypeStruct((t, d // 2), jnp.uint32),
                   jax.ShapeDtypeStruct((t, ROUTE_LANES), jnp.int32), jax.ShapeDtypeStruct((t, ROUTE_LANES), F32)],
        compiler_params=_params(("arbitrary",), 56),
        name="merge",
    )(o_hg, o_hy, o_at, gates, h, wb, wo, ln_g.reshape(1, d), ln_b.reshape(1, d), rwp, rbp)


def _route_plan(top_idx, n_tiles):
    e = top_idx.reshape(-1)
    onehot = (e[:, None] == jnp.arange(N_EXPERTS, dtype=jnp.int32)[None, :]).astype(jnp.int32)
    csum = jnp.cumsum(onehot, axis=0)
    rank = jnp.sum(csum * onehot, axis=1) - 1
    counts = csum[-1]
    padded = (counts + MOE_TM - 1) // MOE_TM * MOE_TM
    p_end = jnp.cumsum(padded)
    p_start = p_end - padded
    pos = (p_start[e] + rank).astype(jnp.int32)
    n_used = (p_end[-1] // MOE_TM).astype(jnp.int32)
    tile_start = jnp.arange(n_tiles, dtype=jnp.int32) * MOE_TM
    tile_expert = jnp.minimum(jnp.searchsorted(p_end, tile_start, side='right'), N_EXPERTS - 1).astype(jnp.int32)
    tile_rows = jnp.clip(counts[tile_expert] - (tile_start - p_start[tile_expert]), 0, MOE_TM).astype(jnp.int32)
    return pos, tile_expert, tile_rows, n_used.reshape(1)


def _dispatch_kernel(pos_ref, hp_ref, xs_ref, sem):
    def body(r, carry):
        for k in range(TOP_K):
            dst = pos_ref[0, r * TOP_K + k]
            pltpu.make_async_copy(hp_ref.at[pl.ds(r, 1), :], xs_ref.at[pl.ds(dst, 1), :], sem).start()
        return carry

    lax.fori_loop(0, hp_ref.shape[0], body, 0)
    for k in range(TOP_K):
        pltpu.make_async_copy(hp_ref, xs_ref.at[pl.ds(0, hp_ref.shape[0]), :], sem).wait()


def _dispatch(hp, pos, n_slots):
    t, dw = hp.shape
    tq = min(TOK_TILE, t)
    return pl.pallas_call(
        _dispatch_kernel,
        grid=(t // tq,),
        in_specs=[pl.BlockSpec((None, 1, tq * TOP_K), lambda i: (i, 0, 0), memory_space=pltpu.SMEM),
                  pl.BlockSpec((tq, dw), lambda i: (i, 0))],
        out_specs=pl.BlockSpec(memory_space=pl.ANY),
        out_shape=jax.ShapeDtypeStruct((n_slots, dw), jnp.uint32),
        scratch_shapes=[pltpu.SemaphoreType.DMA(())],
        compiler_params=_params(("arbitrary",), 32),
        name="moe_dispatch",
    )(pos.reshape(t // tq, 1, tq * TOP_K), hp)


def _ffn_kernel(te_ref, tr_ref, nu_ref, x_ref, wg_ref, wu_ref, bg_ref, bu_ref, wd_ref, bd_ref, y_ref, xb_ref):
    i = pl.program_id(0)
    j = pl.program_id(1)
    rows = tr_ref[i]
    active = i < nu_ref[0]
    half = x_ref.shape[1]

    @pl.when(active & (j == 0))
    def _():
        lo, hi = _unpack_halves(x_ref[...])
        keep = lax.broadcasted_iota(jnp.int32, lo.shape, 0) < rows
        xb_ref[:, :half] = jnp.where(keep, lo, 0.0).astype(BF16)
        xb_ref[:, half:] = jnp.where(keep, hi, 0.0).astype(BF16)

    @pl.when(active)
    def _():
        wg = wg_ref[...].astype(BF16)
        wu = wu_ref[...].astype(BF16)
        wd = wd_ref[...].astype(BF16)
        for sub in range(MOE_TM // MOE_SUB):
            rs = pl.ds(sub * MOE_SUB, MOE_SUB)

            @pl.when(sub * MOE_SUB < rows)
            def _():
                xs = xb_ref[rs, :]
                g = jnp.dot(xs, wg, preferred_element_type=F32) + bg_ref[...]
                u = jnp.dot(xs, wu, preferred_element_type=F32) + bu_ref[...]
                g = jnp.minimum(g, SWIGLU_LIMIT)
                u = jnp.clip(u, -SWIGLU_LIMIT, SWIGLU_LIMIT)
                act = (u + 1.0) * g * jax.nn.sigmoid(SWIGLU_ALPHA * g)
                c = jnp.dot(act.astype(BF16), wd, preferred_element_type=F32)

                @pl.when(j == 0)
                def _():
                    y_ref[rs, :] = c + bd_ref[...]

                @pl.when(j > 0)
                def _():
                    y_ref[rs, :] += c

            @pl.when((sub * MOE_SUB >= rows) & (j == 0))
            def _():
                y_ref[rs, :] = jnp.zeros((MOE_SUB, y_ref.shape[1]), F32)


def _ffn(xs, w_gate_up, b_gate_up, w_down, b_down, layer, tile_expert, tile_rows, n_used):
    n_slots, dw = xs.shape
    d = 2 * dw
    n_tiles = n_slots // MOE_TM
    nf = D_FF // MOE_TF
    bgu = b_gate_up.reshape(DEPTH, N_EXPERTS, 1, 2 * D_FF)
    bd = b_down.reshape(DEPTH, N_EXPERTS, 1, d)

    def tile(i, nu):
        return jnp.minimum(i, nu[0] - 1)

    def fcol(i, j, nu):
        return jnp.where(i < nu[0], j, nf - 1)

    xmap = lambda i, j, te, tr, nu: (tile(i, nu), 0)
    gmap = lambda i, j, te, tr, nu: (layer, te[tile(i, nu)], 0, fcol(i, j, nu))
    umap = lambda i, j, te, tr, nu: (layer, te[tile(i, nu)], 0, nf + fcol(i, j, nu))
    dmap = lambda i, j, te, tr, nu: (layer, te[tile(i, nu)], fcol(i, j, nu), 0)
    bdmap = lambda i, j, te, tr, nu: (layer, te[tile(i, nu)], 0, 0)
    grid_spec = pltpu.PrefetchScalarGridSpec(
        num_scalar_prefetch=3,
        grid=(n_tiles, nf),
        in_specs=[
            pl.BlockSpec((MOE_TM, dw), xmap),
            pl.BlockSpec((None, None, d, MOE_TF), gmap),
            pl.BlockSpec((None, None, d, MOE_TF), umap),
            pl.BlockSpec((None, None, 1, MOE_TF), gmap),
            pl.BlockSpec((None, None, 1, MOE_TF), umap),
            pl.BlockSpec((None, None, MOE_TF, d), dmap),
            pl.BlockSpec((None, None, 1, d), bdmap),
        ],
        out_specs=pl.BlockSpec((MOE_TM, d), xmap),
        scratch_shapes=[pltpu.VMEM((MOE_TM, d), BF16)],
    )
    return pl.pallas_call(
        _ffn_kernel,
        grid_spec=grid_spec,
        out_shape=jax.ShapeDtypeStruct((n_slots, d), F32),
        compiler_params=_params(("arbitrary", "arbitrary"), 56),
        name="moe_ffn",
    )(tile_expert, tile_rows, n_used, xs, w_gate_up, w_gate_up, bgu, bgu, w_down, bd)


def _combine_kernel(pos_ref, ys_ref, gate_ref, h_ref, lng_ref, lnb_ref, hout_ref, hb_ref, buf, sem):
    tq = h_ref.shape[0]

    def body(r, carry):
        for k in range(TOP_K):
            src = pos_ref[0, r * TOP_K + k]
            pltpu.make_async_copy(ys_ref.at[pl.ds(src, 1), :], buf.at[k, pl.ds(r, 1), :], sem).start()
        return carry

    lax.fori_loop(0, tq, body, 0)
    for k in range(TOP_K):
        pltpu.make_async_copy(ys_ref.at[pl.ds(0, tq), :], buf.at[k], sem).wait()
    gate = gate_ref[...]
    acc = DEEPNORM_ALPHA * h_ref[...]
    for k in range(TOP_K):
        acc = acc + gate[:, k:k + 1] * buf[k]
    hn = _layer_norm_rows(acc, lng_ref[...], lnb_ref[...])
    hout_ref[...] = hn
    hb_ref[...] = hn.astype(BF16)


def _combine(ys, pos, gate, h, ln_g, ln_b):
    t, d = h.shape
    tq = min(TOK_TILE, t)
    row = lambda width: pl.BlockSpec((tq, width), lambda i: (i, 0))
    vec = pl.BlockSpec((1, d), lambda i: (0, 0))
    return pl.pallas_call(
        _combine_kernel,
        grid=(t // tq,),
        in_specs=[pl.BlockSpec((None, 1, tq * TOP_K), lambda i: (i, 0, 0), memory_space=pltpu.SMEM),
                  pl.BlockSpec(memory_space=pl.ANY), row(ROUTE_LANES), row(d), vec, vec],
        out_specs=[row(d), row(d)],
        out_shape=[jax.ShapeDtypeStruct((t, d), F32), jax.ShapeDtypeStruct((t, d), BF16)],
        scratch_shapes=[pltpu.VMEM((TOP_K, tq, d), F32), pltpu.SemaphoreType.DMA(())],
        compiler_params=_params(("arbitrary",), 40),
        name="moe_combine",
    )(pos.reshape(t // tq, 1, tq * TOP_K), ys, gate, h, ln_g.reshape(1, d), ln_b.reshape(1, d))


def _hgrn_lower_bound(lb_table, layer):
    p = jax.nn.softmax(lb_table.astype(F32), axis=0)
    return jnp.cumsum(p, axis=0)[layer] - p[0]


def kernel(x, ln_in_g, ln_in_b, w_in, hg_lower_bound, hg_norm_g, hy_conv_w, hy_conv_b, hy_filt_w1, hy_filt_b1,
           hy_filt_w2, hy_filt_b2, hy_filt_freq, hy_filt_w3, hy_skip, att_sink, rel_bias, w_branch, w_out,
           ln_mix_g, ln_mix_b, router_w, router_b, w_gate_up, b_gate_up, w_down, b_down, ln_moe_g, ln_moe_b):
    batch, seq, d = x.shape
    t = batch * seq
    n_tiles = t * TOP_K // MOE_TM + N_EXPERTS
    n_slots = n_tiles * MOE_TM

    fmat, gmat = _dft_tables(seq)
    z_pos, window = _hy_positions(seq)
    bias_tab = _attn_bias_table(rel_bias)

    h, hb = _ln_in(x.reshape(t, d), ln_in_g, ln_in_b)
    for layer in range(DEPTH):
        proj = _inproj(hb, w_in, layer, 0, MAIN_COLS)
        gates = _inproj(hb, w_in, layer, OFF_GATES, N_BRANCH * d)

        lb = _hgrn_lower_bound(hg_lower_bound, layer)
        o_hg = _hgrn(proj, lb, hg_norm_g[layer], batch, seq)

        hcat = _hy_filters(z_pos, window, hy_filt_w1[layer], hy_filt_b1[layer], hy_filt_w2[layer],
                           hy_filt_b2[layer], hy_filt_freq[layer], hy_filt_w3[layer])
        pr, pi, nyq = _hy_spectrum(fmat, hcat, seq)
        o_hy = _hyena(proj, hy_conv_w[layer], hy_conv_b[layer], hy_skip[layer], fmat, gmat, pr, pi, nyq,
                      batch, seq)

        o_at = _attn(proj, att_sink[layer], bias_tab, batch, seq)

        h_mid, hp, top_idx, gate = _merge(
            o_hg, o_hy, o_at, gates, h, w_branch[layer].astype(BF16), w_out[layer].astype(BF16),
            ln_mix_g[layer], ln_mix_b[layer], router_w[layer], router_b[layer])

        pos, tile_expert, tile_rows, n_used = _route_plan(top_idx[:, :TOP_K], n_tiles)
        xs = _dispatch(hp, pos, n_slots)
        ys = _ffn(xs, w_gate_up, b_gate_up, w_down, b_down, layer, tile_expert, tile_rows, n_used)
        h, hb = _combine(ys, pos, gate, h_mid, ln_moe_g[layer], ln_moe_b[layer])
    return h.reshape(batch, seq, d)
```

```python
import functools
import math

import jax
import jax.numpy as jnp
from jax import lax
from jax.experimental import pallas as pl
from jax.experimental.pallas import tpu as pltpu

F32 = jnp.float32
BF16 = jnp.bfloat16

D_MODEL = 2048
DEPTH = 2
MIX_W = 1024
N_BRANCH = 3
HG_HEADS = 8
HG_DK = 128
HG_DV = 128
HG_CHUNK = 16
F_FLOOR = 1e-30
HY_W = 1024
HY_SHORT = 3
HY_EMB = 33
HY_BANDS = (HY_EMB - 1) // 2
HY_ORDER = 64
HY_INNER = 2
HY_FAST_DECAY = 0.3
HY_SLOW_DECAY = 1.5
HY_TARGET = 1e-2
ATT_HEADS = 16
ATT_KV_HEADS = 2
ATT_DH = 64
WINDOW = 128
ATT_BLOCK = 128
REL_BUCKETS = 32
REL_MAX_DIST = 128
MASK_VALUE = -1e30
N_EXPERTS = 32
TOP_K = 4
D_FF = 2048
SWIGLU_ALPHA = 1.702
SWIGLU_LIMIT = 7.0
LN_EPS = 1e-5
RMS_EPS = 1e-6
DEEPNORM_ALPHA = (2 * DEPTH) ** 0.25

HG_KW = HG_HEADS * HG_DK
HG_VW = HG_HEADS * HG_DV
ATT_QW = ATT_HEADS * ATT_DH
ATT_KVW = ATT_KV_HEADS * ATT_DH
IN_SIZES = (HG_KW, HG_KW, HG_KW, HG_VW, HG_VW, 3 * HY_W, ATT_QW, ATT_KVW, ATT_KVW, N_BRANCH * D_MODEL)
IN_COLS = sum(IN_SIZES)
OFF_HQ, OFF_HFF, OFF_HFB, OFF_HI, OFF_HOG = 0, HG_KW, 2 * HG_KW, 3 * HG_KW, 3 * HG_KW + HG_VW
OFF_HY = OFF_HOG + HG_VW
OFF_AQ = OFF_HY + 3 * HY_W
OFF_AK = OFF_AQ + ATT_QW
OFF_AV = OFF_AK + ATT_KVW
OFF_GATES = OFF_AV + ATT_KVW
MAIN_COLS = OFF_GATES

V7X_LANES = 128
V7X_VMEM_BYTES = 64 * 1024 * 1024

LN_ROWS = 512
PROJ_TM = 1024
PROJ_TN = 1024
PROJ_KV_COLS = 2 * ATT_KVW
HY_TC = 512
MERGE_TM = 512
MOE_TM = 1024
MOE_SUB = 512
MOE_TF = 256
MOE_TN = 512
TOK_TILE = 256
ROUTE_LANES = 128


def _params(semantics, vmem_mb):
    return pltpu.CompilerParams(dimension_semantics=semantics, vmem_limit_bytes=vmem_mb * 1024 * 1024)


def _layer_norm_rows(x, g, b):
    mu = jnp.mean(x, axis=-1, keepdims=True)
    xc = x - mu
    var = jnp.mean(xc * xc, axis=-1, keepdims=True)
    return xc * lax.rsqrt(var + LN_EPS) * g + b


def _pack_halves(x):
    c = x.shape[-1] // 2
    lo = pltpu.bitcast(x[:, :c].astype(BF16).astype(F32), jnp.uint32)
    hi = pltpu.bitcast(x[:, c:].astype(BF16).astype(F32), jnp.uint32)
    return (lo >> 16) | (hi & jnp.uint32(0xFFFF0000))


def _unpack_halves(w):
    lo = pltpu.bitcast(w << 16, F32)
    hi = pltpu.bitcast(w & jnp.uint32(0xFFFF0000), F32)
    return lo, hi


def _ln_in_kernel(x_ref, g_ref, b_ref, h_ref, hb_ref):
    y = _layer_norm_rows(x_ref[...], g_ref[...], b_ref[...])
    h_ref[...] = y
    hb_ref[...] = y.astype(BF16)


def _ln_in(x2, g, b):
    t, d = x2.shape
    tm = min(LN_ROWS, t)
    row = pl.BlockSpec((tm, d), lambda i: (i, 0))
    vec = pl.BlockSpec((1, d), lambda i: (0, 0))
    return pl.pallas_call(
        _ln_in_kernel,
        grid=(t // tm,),
        in_specs=[row, vec, vec],
        out_specs=[row, row],
        out_shape=[jax.ShapeDtypeStruct((t, d), F32), jax.ShapeDtypeStruct((t, d), BF16)],
        compiler_params=_params(("arbitrary",), 32),
        name="ln_in",
    )(x2, g.reshape(1, d), b.reshape(1, d))


def _inproj_kernel(a_ref, w_ref, o_ref, wb_ref):
    @pl.when(pl.program_id(1) == 0)
    def _():
        wb_ref[...] = w_ref[...].astype(BF16)

    o_ref[...] = jnp.dot(a_ref[...], wb_ref[...], preferred_element_type=F32).astype(o_ref.dtype)


def _inproj(hb, w_in, layer, col0, ncols, tn):
    t, d = hb.shape
    tm = min(PROJ_TM, t)
    cb0 = col0 // tn
    return pl.pallas_call(
        _inproj_kernel,
        grid=(ncols // tn, t // tm),
        in_specs=[
            pl.BlockSpec((tm, d), lambda j, i: (i, 0)),
            pl.BlockSpec((None, d, tn), lambda j, i: (layer, 0, cb0 + j)),
        ],
        out_specs=pl.BlockSpec((tm, tn), lambda j, i: (i, j)),
        out_shape=jax.ShapeDtypeStruct((t, ncols), BF16),
        scratch_shapes=[pltpu.VMEM((d, tn), BF16)],
        compiler_params=_params(("arbitrary", "arbitrary"), 48),
        name="inproj",
    )(hb, w_in)


HG_BLK = 128
HG_CPB = HG_BLK // HG_CHUNK
HG_UNROLL = 16
LOG2E = 1.4426950408889634


def _hgrn_perm():
    r = jnp.arange(HG_BLK)
    src = (r % HG_CPB) * HG_CHUNK + r // HG_CPB
    return (src[:, None] == jnp.arange(HG_BLK)[None, :]).astype(BF16)


def _hgrn_kernel(q_ref, ff_ref, fb_ref, i_ref, og_ref, lbf_ref, lbb_ref, g_ref, p_ref, pt_ref, o_ref,
                 x_t, kk_t, o_t, tmp_t, qtil_c, ktil_c, kv_s, oint_c, dec_c, *, seq):
    c = HG_CHUNK
    nc = seq // c
    nb = seq // HG_BLK
    nbh = max(nb // 2, 1)
    n_half = nb // nbh
    perm = p_ref[...]
    perm_t = pt_ref[...]

    for a, ref in enumerate((q_ref, ff_ref, fb_ref, i_ref)):
        for r in range(nb):
            xp = jnp.dot(perm, ref[r * HG_BLK:(r + 1) * HG_BLK, :], preferred_element_type=F32)
            x_t[a, r] = xp.reshape(c, HG_CPB, HG_DK)

    lbs = (lbf_ref[...], lbb_ref[...])
    for j in range(c):
        qj = x_t[0, :, j]
        x_t[0, :, j] = qj * jax.nn.sigmoid(qj) * (HG_DK ** -0.5)
        for d in range(2):
            z = x_t[1 + d, :, j]
            lb = lbs[d]
            e = jnp.exp(-jnp.abs(z))
            r_ = 1.0 / (1.0 + e)
            er = e * r_
            sig = jnp.where(z >= 0, r_, er)
            nsig = jnp.where(z >= 0, er, r_)
            f = lb + (1.0 - lb) * sig
            x_t[1 + d, :, j] = jnp.log(jnp.maximum(f, F_FLOOR)) * LOG2E
            kk_t[d, :, j] = (1.0 - lb) * nsig

    for d in range(2):
        acc = jnp.zeros((nb, HG_CPB, HG_DK), F32)
        for j in (range(c) if d == 0 else reversed(range(c))):
            acc = acc + x_t[1 + d, :, j]
            x_t[1 + d, :, j] = acc
        last = acc
        dec_c[d] = jnp.exp2(last).reshape(nc, HG_DK)
        for j in range(c):
            cj = x_t[1 + d, :, j]
            tmp_t[0, :, j] = x_t[0, :, j] * jnp.exp2(cj)
            tmp_t[1, :, j] = kk_t[d, :, j] * jnp.exp2(last - cj)
        for r in range(nb):
            rows = slice(r * HG_BLK, (r + 1) * HG_BLK)
            qb = tmp_t[0, r].reshape(HG_BLK, HG_DK).astype(BF16)
            kb = tmp_t[1, r].reshape(HG_BLK, HG_DK).astype(BF16)
            qtil_c[d, rows, :] = jnp.dot(perm_t, qb, preferred_element_type=F32).astype(BF16)
            ktil_c[rows, d * HG_DK:(d + 1) * HG_DK] = jnp.dot(perm_t, kb, preferred_element_type=F32).astype(BF16)

    o_t[...] = jnp.zeros(o_t.shape, F32)
    ones_b = jnp.ones((HG_DK, HG_DV), BF16)

    def pair_body(it, carry):
        d = it // n_half
        sl = pl.ds((it % n_half) * nbh, nbh)
        for t in range(c):
            rt = t + d * (c - 1 - 2 * t)
            ct = x_t[1 + d, sl, rt]
            qt = x_t[0, sl, rt]
            acc = jnp.zeros((nbh * HG_CPB, HG_DV), F32)
            for s in range(t + 1):
                rs = s + d * (c - 1 - 2 * s)
                e = jnp.exp2(jnp.minimum(ct - x_t[1 + d, sl, rs], 0.0))
                a = (qt * e * kk_t[d, sl, rs]).reshape(nbh * HG_CPB, HG_DK).astype(BF16)
                p = jnp.dot(a, ones_b, preferred_element_type=F32)
                acc = acc + p * x_t[3, sl, rs].reshape(nbh * HG_CPB, HG_DV)
            o_t[sl, rt] += acc.reshape(nbh, HG_CPB, HG_DV)
        return carry

    lax.fori_loop(0, 2 * n_half, pair_body, 0)

    def kv_body(n, carry):
        r0 = pl.multiple_of(n * c, c)
        kv = lax.dot_general(i_ref[pl.ds(r0, c), :], ktil_c[pl.ds(r0, c), :], (((0,), (0,)), ((), ())),
                             preferred_element_type=F32)
        kv_s[0, n] = kv[:, :HG_DK]
        kv_s[1, n] = kv[:, HG_DK:]
        return carry

    lax.fori_loop(0, nc, kv_body, 0, unroll=HG_UNROLL)

    def chain_body(idx, carry):
        new = []
        for d in range(2):
            n = idx if d == 0 else nc - 1 - idx
            s = carry[d]
            new.append(dec_c[d, pl.ds(n, 1), :] * s + kv_s[d, n])
            kv_s[d, n] = s
        return tuple(new)

    zero_state = jnp.zeros((HG_DV, HG_DK), F32)
    lax.fori_loop(0, nc, chain_body, (zero_state, zero_state), unroll=2)

    def out_body(n, carry):
        r0 = pl.multiple_of(n * c, c)
        for d in range(2):
            oint_c[d, pl.ds(r0, c), :] = lax.dot_general(
                qtil_c[d, pl.ds(r0, c), :], kv_s[d, n].astype(BF16), (((1,), (1,)), ((), ())),
                preferred_element_type=F32)
        return carry

    lax.fori_loop(0, nc, out_body, 0, unroll=HG_UNROLL)

    for r in range(nb):
        rows = slice(r * HG_BLK, (r + 1) * HG_BLK)
        ob = o_t[r].reshape(HG_BLK, HG_DV)
        hi = ob.astype(BF16)
        lo = (ob - hi.astype(F32)).astype(BF16)
        oc = jnp.dot(perm_t, hi, preferred_element_type=F32) + jnp.dot(perm_t, lo, preferred_element_type=F32)
        o = oc + oint_c[0, rows, :] + oint_c[1, rows, :]
        o = o * lax.rsqrt(jnp.mean(o * o, axis=-1, keepdims=True) + RMS_EPS) * g_ref[...]
        og = og_ref[rows, :].astype(F32)
        o_ref[rows, :] = (o * (og * jax.nn.sigmoid(og))).astype(o_ref.dtype)


def _hgrn(proj, lb, norm_g, batch, seq):
    t = batch * seq
    nc = seq // HG_CHUNK

    def col(off):
        return pl.BlockSpec((seq, HG_DK), lambda b, h: (b, off // HG_DK + h))

    vec = pl.BlockSpec((None, 1, HG_DK), lambda b, h: (h, 0, 0))
    lbf = lb[:HG_KW].reshape(HG_HEADS, 1, HG_DK)
    lbb = lb[HG_KW:].reshape(HG_HEADS, 1, HG_DK)
    g = norm_g.reshape(HG_HEADS, 1, HG_DV)
    slab = (seq // HG_BLK, HG_CHUNK, HG_CPB, HG_DK)
    perm = _hgrn_perm()
    pspec = pl.BlockSpec((HG_BLK, HG_BLK), lambda b, h: (0, 0))
    return pl.pallas_call(
        functools.partial(_hgrn_kernel, seq=seq),
        grid=(batch, HG_HEADS),
        in_specs=[col(OFF_HQ), col(OFF_HFF), col(OFF_HFB), col(OFF_HI), col(OFF_HOG), vec, vec, vec, pspec, pspec],
        out_specs=pl.BlockSpec((seq, HG_DV), lambda b, h: (b, h)),
        out_shape=jax.ShapeDtypeStruct((t, HG_VW), BF16),
        scratch_shapes=[
            pltpu.VMEM((4,) + slab, F32),
            pltpu.VMEM((2,) + slab, F32),
            pltpu.VMEM(slab, F32),
            pltpu.VMEM((2,) + slab, F32),
            pltpu.VMEM((2, seq, HG_DK), BF16),
            pltpu.VMEM((seq, 2 * HG_DK), BF16),
            pltpu.VMEM((2, nc, HG_DV, HG_DK), F32),
            pltpu.VMEM((2, seq, HG_DV), F32),
            pltpu.VMEM((2, nc, HG_DK), F32),
        ],
        compiler_params=_params(("arbitrary", "arbitrary"), 48),
        name="hgrn2",
    )(proj, proj, proj, proj, proj, lbf, lbb, g, perm, perm.T)


def _attn_kernel(sink_ref, q_ref, kp_ref, ko_ref, kn_ref, vp_ref, vo_ref, vn_ref, bias_ref, o_ref, *, seq):
    n = pl.program_id(1)
    w = ATT_BLOCK
    group = ATT_HEADS // ATT_KV_HEADS
    kband = jnp.concatenate([kp_ref[...], ko_ref[...], kn_ref[...]], axis=0)
    vband = jnp.concatenate([vp_ref[...], vo_ref[...], vn_ref[...]], axis=0)
    kpos = n * w - w + lax.broadcasted_iota(jnp.int32, (1, 3 * w), 1)
    posmask = jnp.where((kpos >= 0) & (kpos < seq), 0.0, MASK_VALUE).astype(F32)
    q = q_ref[...]
    outs = []
    for h in range(ATT_HEADS):
        g = h // group
        qh = q[:, h * ATT_DH:(h + 1) * ATT_DH]
        kg = kband[:, g * ATT_DH:(g + 1) * ATT_DH]
        vg = vband[:, g * ATT_DH:(g + 1) * ATT_DH]
        s = lax.dot_general(qh, kg, (((1,), (1,)), ((), ())), preferred_element_type=F32) * (ATT_DH ** -0.5)
        s = s + bias_ref[h] + posmask
        sk = sink_ref[h]
        m = jnp.maximum(jnp.max(s, axis=-1, keepdims=True), sk)
        p = jnp.exp(s - m)
        denom = jnp.sum(p, axis=-1, keepdims=True) + jnp.exp(sk - m)
        outs.append(jnp.dot(p.astype(BF16), vg, preferred_element_type=F32) / denom)
    o_ref[...] = jnp.concatenate(outs, axis=1).astype(o_ref.dtype)


def _t5_relative_bucket(rel):
    half = REL_BUCKETS // 2
    max_exact = half // 2
    bucket = (rel > 0).astype(jnp.int32) * half
    n = jnp.abs(rel)
    n_safe = jnp.maximum(n, 1).astype(F32)
    large = max_exact + (jnp.log(n_safe / max_exact) / math.log(REL_MAX_DIST / max_exact)
                         * (half - max_exact)).astype(jnp.int32)
    large = jnp.clip(large, 0, half - 1)
    return bucket + jnp.where(n < max_exact, n, large)


def _attn_bias_table(rel_bias):
    w = ATT_BLOCK
    kofs = jnp.arange(3 * w, dtype=jnp.int32)[None, :] - w
    rel = kofs - jnp.arange(w, dtype=jnp.int32)[:, None]
    bias = jnp.transpose(rel_bias[_t5_relative_bucket(rel)], (2, 0, 1)).astype(F32)
    return jnp.where((jnp.abs(rel) <= WINDOW)[None], bias, MASK_VALUE)


def _attn(proj, proj_kv, sink, bias_tab, batch, seq):
    t = batch * seq
    w = ATT_BLOCK
    nb = seq // w
    kcol, vcol = 0, 1

    def kv(col, delta):
        return pl.BlockSpec((w, ATT_KVW), lambda b, n: (b * nb + jnp.clip(n + delta, 0, nb - 1), col))

    return pl.pallas_call(
        functools.partial(_attn_kernel, seq=seq),
        grid=(batch, nb),
        in_specs=[
            pl.BlockSpec(memory_space=pltpu.SMEM),
            pl.BlockSpec((w, ATT_QW), lambda b, n: (b * nb + n, OFF_AQ // ATT_QW)),
            kv(kcol, -1), kv(kcol, 0), kv(kcol, 1),
            kv(vcol, -1), kv(vcol, 0), kv(vcol, 1),
            pl.BlockSpec((ATT_HEADS, w, 3 * w), lambda b, n: (0, 0, 0)),
        ],
        out_specs=pl.BlockSpec((w, ATT_QW), lambda b, n: (b * nb + n, 0)),
        out_shape=jax.ShapeDtypeStruct((t, ATT_QW), BF16),
        compiler_params=_params(("arbitrary", "arbitrary"), 32),
        name="win_attn",
    )(sink.astype(F32), proj, proj_kv, proj_kv, proj_kv, proj_kv, proj_kv, proj_kv, bias_tab)


def _dft_tables(seq):
    n = 2 * seq
    k = jnp.arange(seq, dtype=jnp.int32)[:, None]
    s = jnp.arange(seq, dtype=jnp.int32)[None, :]
    ang = ((k * s) % n).astype(F32) * (2.0 * math.pi / n)
    cm = jnp.cos(ang)
    sm = -jnp.sin(ang)
    nyq = jnp.where(s % 2 == 0, 1.0, -1.0).astype(F32)
    sm = jnp.where(k == 0, nyq, sm)
    f = jnp.concatenate([cm, sm], axis=0).astype(BF16)
    return f, f.T


def _hy_positions(seq):
    t = jnp.linspace(0.0, 1.0, seq, dtype=F32)[:, None]
    w = 2.0 * math.pi * jnp.arange(seq, dtype=F32)[:, None] / seq
    f = jnp.linspace(1e-4, HY_BANDS - 1, HY_BANDS, dtype=F32)[None]
    z = jnp.concatenate([t, jnp.cos(f * w), -jnp.sin(f * w)], axis=-1)
    z = jnp.pad(z, ((0, 0), (0, V7X_LANES - HY_EMB)))
    max_decay = math.log(HY_TARGET) / HY_FAST_DECAY
    min_decay = math.log(HY_TARGET) / HY_SLOW_DECAY
    deltas = jnp.linspace(min_decay, max_decay, HY_W, dtype=F32)
    window = jnp.exp(-t * jnp.abs(deltas))
    return z, window


def _hy_filter_kernel(z_ref, w1_ref, b1_ref, w2_ref, b2_ref, fr_ref, w3_ref, win_ref, h_ref):
    dot = functools.partial(jnp.dot, precision=lax.Precision.HIGHEST, preferred_element_type=F32)
    fr = fr_ref[...]
    h = jnp.sin(fr * (dot(z_ref[...], w1_ref[...]) + b1_ref[...]))
    for j in range(HY_INNER):
        h = jnp.sin(fr * (dot(h, w2_ref[j]) + b2_ref[j]))
    h = dot(h, w3_ref[...]) * win_ref[...]
    row = lax.broadcasted_iota(jnp.int32, h.shape, 0)
    backward = pl.program_id(0) >= pl.num_programs(0) // 2
    h_ref[...] = jnp.where((row == 0) & backward, 0.0, h).astype(h_ref.dtype)


def _hy_filters(z, window, w1, b1, w2, b2, freq, w3):
    seq = z.shape[0]
    tn = HY_TC
    per_dir = HY_W // tn
    full = lambda shape: pl.BlockSpec(shape, lambda j: (0,) * len(shape))
    w1p = jnp.pad(w1, ((0, V7X_LANES - HY_EMB), (0, 0)))
    return pl.pallas_call(
        _hy_filter_kernel,
        grid=(2 * per_dir,),
        in_specs=[
            full((seq, V7X_LANES)), full((V7X_LANES, HY_ORDER)), full((1, HY_ORDER)),
            full((HY_INNER, HY_ORDER, HY_ORDER)), full((HY_INNER, 1, HY_ORDER)), full((1, HY_ORDER)),
            pl.BlockSpec((HY_ORDER, tn), lambda j: (0, j)),
            pl.BlockSpec((seq, tn), lambda j: (0, j % per_dir)),
        ],
        out_specs=pl.BlockSpec((seq, tn), lambda j: (0, j)),
        out_shape=jax.ShapeDtypeStruct((seq, 2 * HY_W), BF16),
        compiler_params=_params(("arbitrary",), 32),
        name="hy_filter",
    )(z, w1p, b1.reshape(1, HY_ORDER), w2, b2.reshape(HY_INNER, 1, HY_ORDER), freq.reshape(1, HY_ORDER), w3, window)


def _mm_kernel(a_ref, b_ref, o_ref):
    o_ref[...] = jnp.dot(a_ref[...], b_ref[...], preferred_element_type=F32).astype(o_ref.dtype)


def _mm(a, b, tm, tn, out_dtype):
    m, k = a.shape
    n = b.shape[1]
    return pl.pallas_call(
        _mm_kernel,
        grid=(m // tm, n // tn),
        in_specs=[pl.BlockSpec((tm, k), lambda i, j: (i, 0)), pl.BlockSpec((k, tn), lambda i, j: (0, j))],
        out_specs=pl.BlockSpec((tm, tn), lambda i, j: (i, j)),
        out_shape=jax.ShapeDtypeStruct((m, n), out_dtype),
        compiler_params=_params(("arbitrary", "arbitrary"), 32),
        name="mm",
    )(a, b)


def _hy_spectrum(fmat, hcat, seq):
    spec = _mm(fmat, hcat, min(1024, 2 * seq), 512, F32)
    top, bot = spec[:seq], spec[seq:]
    kr = top[:, :HY_W] + top[:, HY_W:]
    ki = bot[:, :HY_W] - bot[:, HY_W:]
    n = 2 * seq
    first = (jnp.arange(seq) == 0)[:, None]
    pr = jnp.where(first, kr / n, kr * (2.0 / n))
    pi = jnp.where(first, 0.0, ki * (2.0 / n))
    nyq = (bot[:1, :HY_W] + bot[:1, HY_W:]) / n
    return pr, pi, nyq


def _short_conv(x, w_ref, b_ref):
    seq = x.shape[0]
    row = lax.broadcasted_iota(jnp.int32, x.shape, 0)
    prev = jnp.where(row == 0, 0.0, pltpu.roll(x, 1, 0))
    nxt = jnp.where(row == seq - 1, 0.0, pltpu.roll(x, seq - 1, 0))
    return w_ref[0:1, :] * prev + w_ref[1:2, :] * x + w_ref[2:3, :] * nxt + b_ref[...]


def _hy_fwd_kernel(x1_ref, v_ref, w1_ref, b1_ref, wv_ref, bv_ref, f_ref, pr_ref, pi_ref, nyq_ref, y_ref):
    seq = x1_ref.shape[0]
    x1 = _short_conv(x1_ref[...].astype(F32), w1_ref, b1_ref)
    v = _short_conv(v_ref[...].astype(F32), wv_ref, bv_ref)
    u = (x1 * v).astype(BF16)
    w = jnp.dot(f_ref[...], u, preferred_element_type=F32)
    a, b = w[:seq], w[seq:]
    pr, pi = pr_ref[...], pi_ref[...]
    row = lax.broadcasted_iota(jnp.int32, pr.shape, 0)
    pd = jnp.where(row == 0, nyq_ref[...], pr)
    y_ref[:seq, :] = (a * pr - b * pi).astype(y_ref.dtype)
    y_ref[seq:, :] = (a * pi + b * pd).astype(y_ref.dtype)


def _hy_inv_kernel(y_ref, x0_ref, x1_ref, v_ref, w0_ref, b0_ref, w1_ref, b1_ref, wv_ref, bv_ref, skip_ref,
                   g_ref, o_ref):
    y = jnp.dot(g_ref[...], y_ref[...], preferred_element_type=F32)
    x0 = _short_conv(x0_ref[...].astype(F32), w0_ref, b0_ref)
    x1 = _short_conv(x1_ref[...].astype(F32), w1_ref, b1_ref)
    v = _short_conv(v_ref[...].astype(F32), wv_ref, bv_ref)
    u = x1 * v
    o_ref[...] = (x0 * (y + u * skip_ref[...])).astype(o_ref.dtype)


def _hyena(proj, conv_w, conv_b, skip, fmat, gmat, pr, pi, nyq, batch, seq):
    t = batch * seq
    tc = HY_TC // 2
    nct = HY_W // tc
    conv_b2 = conv_b.reshape(1, 3 * HY_W)

    def xcol(part):
        return pl.BlockSpec((seq, tc), lambda c, b: (b, (OFF_HY + part * HY_W) // tc + c))

    def wcol(part):
        return pl.BlockSpec((HY_SHORT, tc), lambda c, b: (0, part * nct + c))

    def bcol(part):
        return pl.BlockSpec((1, tc), lambda c, b: (0, part * nct + c))

    chan = pl.BlockSpec((seq, tc), lambda c, b: (0, c))
    chan1 = pl.BlockSpec((1, tc), lambda c, b: (0, c))
    once = pl.Buffered(1)
    yspec = pl.BlockSpec((None, 2 * seq, tc), lambda c, b: (b, 0, c))
    yfreq = pl.pallas_call(
        _hy_fwd_kernel,
        grid=(nct, batch),
        in_specs=[xcol(1), xcol(2), wcol(1), bcol(1), wcol(2), bcol(2),
                  pl.BlockSpec((2 * seq, seq), lambda c, b: (0, 0), pipeline_mode=once),
                  chan, chan, chan1],
        out_specs=yspec,
        out_shape=jax.ShapeDtypeStruct((batch, 2 * seq, HY_W), BF16),
        compiler_params=_params(("arbitrary", "arbitrary"), 56),
        name="hy_fwd",
    )(proj, proj, conv_w, conv_b2, conv_w, conv_b2, fmat, pr, pi, nyq)
    return pl.pallas_call(
        _hy_inv_kernel,
        grid=(nct, batch),
        in_specs=[yspec, xcol(0), xcol(1), xcol(2), wcol(0), bcol(0), wcol(1), bcol(1), wcol(2), bcol(2), chan1,
                  pl.BlockSpec((seq, 2 * seq), lambda c, b: (0, 0), pipeline_mode=once)],
        out_specs=pl.BlockSpec((seq, tc), lambda c, b: (b, c)),
        out_shape=jax.ShapeDtypeStruct((t, HY_W), BF16),
        compiler_params=_params(("arbitrary", "arbitrary"), 56),
        name="hy_inv",
    )(yfreq, proj, proj, proj, conv_w, conv_b2, conv_w, conv_b2, conv_w, conv_b2, skip.reshape(1, HY_W), gmat)


def _branch_kernel(ohg_ref, ohy_ref, oat_ref, gates_ref, wb_ref, m_ref):
    d = D_MODEL
    m = None
    for n, o_ref in enumerate((ohg_ref, ohy_ref, oat_ref)):
        br = jnp.dot(o_ref[...], wb_ref[n], preferred_element_type=F32)
        term = jax.nn.sigmoid(gates_ref[:, n * d:(n + 1) * d].astype(F32)) * br
        m = term if m is None else m + term
    m_ref[...] = m.astype(m_ref.dtype)


def _merge_kernel(m_ref, h_ref, wo_ref, lng_ref, lnb_ref, rwh_ref, rwl_ref, rb_ref,
                  hmid_ref, hp_ref, idx_ref, gate_ref):
    y = jnp.dot(m_ref[...], wo_ref[...], preferred_element_type=F32)
    hn = _layer_norm_rows(DEEPNORM_ALPHA * h_ref[...] + y, lng_ref[...], lnb_ref[...])
    hmid_ref[...] = hn
    hp_ref[...] = _pack_halves(hn)

    h_hi = hn.astype(BF16)
    h_lo = (hn - h_hi.astype(F32)).astype(BF16)
    logits = (jnp.dot(h_hi, rwh_ref[...], preferred_element_type=F32)
              + jnp.dot(h_lo, rwh_ref[...], preferred_element_type=F32)
              + jnp.dot(h_hi, rwl_ref[...], preferred_element_type=F32)) + rb_ref[...]
    lane = lax.broadcasted_iota(jnp.int32, logits.shape, 1)
    vals, idxs = [], []
    for _ in range(TOP_K):
        mx = jnp.max(logits, axis=-1, keepdims=True)
        ix = jnp.min(jnp.where(logits == mx, lane, ROUTE_LANES), axis=-1, keepdims=True)
        vals.append(mx)
        idxs.append(ix)
        logits = jnp.where(lane == ix, -jnp.inf, logits)
    exps = [jnp.exp(v - vals[0]) for v in vals]
    total = exps[0]
    for e in exps[1:]:
        total = total + e
    gate_out = jnp.zeros(logits.shape, F32)
    idx_out = jnp.zeros(logits.shape, jnp.int32)
    for r in range(TOP_K):
        gate_out = jnp.where(lane == r, exps[r] / total, gate_out)
        idx_out = jnp.where(lane == r, idxs[r], idx_out)
    gate_ref[...] = gate_out
    idx_ref[...] = idx_out


def _merge(o_hg, o_hy, o_at, gates, h, wb, wo, ln_g, ln_b, rw, rb):
    t, d = h.shape
    tm = min(MERGE_TM, t)
    once = pl.Buffered(1)
    row = lambda width: pl.BlockSpec((tm, width), lambda i: (i, 0))
    const = lambda shape: pl.BlockSpec(shape, lambda i: (0,) * len(shape), pipeline_mode=once)
    rwp = jnp.pad(rw, ((0, 0), (0, ROUTE_LANES - N_EXPERTS)))
    rw_hi = rwp.astype(BF16)
    rw_lo = (rwp - rw_hi.astype(F32)).astype(BF16)
    rbp = jnp.pad(rb, (0, ROUTE_LANES - N_EXPERTS), constant_values=MASK_VALUE).reshape(1, ROUTE_LANES)
    m = pl.pallas_call(
        _branch_kernel,
        grid=(t // tm,),
        in_specs=[row(MIX_W), row(MIX_W), row(MIX_W), row(N_BRANCH * d), const((N_BRANCH, MIX_W, d))],
        out_specs=row(d),
        out_shape=jax.ShapeDtypeStruct((t, d), BF16),
        compiler_params=_params(("arbitrary",), 56),
        name="branch_merge",
    )(o_hg, o_hy, o_at, gates, wb)
    return pl.pallas_call(
        _merge_kernel,
        grid=(t // tm,),
        in_specs=[row(d), row(d), const((d, d)), const((1, d)), const((1, d)),
                  const((d, ROUTE_LANES)), const((d, ROUTE_LANES)), const((1, ROUTE_LANES))],
        out_specs=[row(d), row(d // 2), row(ROUTE_LANES), row(ROUTE_LANES)],
        out_shape=[jax.ShapeDtypeStruct((t, d), F32), jax.ShapeDtypeStruct((t, d // 2), jnp.uint32),
                   jax.ShapeDtypeStruct((t, ROUTE_LANES), jnp.int32), jax.ShapeDtypeStruct((t, ROUTE_LANES), F32)],
        compiler_params=_params(("arbitrary",), 56),
        name="merge",
    )(m, h, wo, ln_g.reshape(1, d), ln_b.reshape(1, d), rw_hi, rw_lo, rbp)


def _route_plan(top_idx, n_tiles):
    e = top_idx.reshape(-1)
    onehot = (e[:, None] == jnp.arange(N_EXPERTS, dtype=jnp.int32)[None, :]).astype(jnp.int32)
    csum = jnp.cumsum(onehot, axis=0)
    rank = jnp.sum(csum * onehot, axis=1) - 1
    counts = csum[-1]
    padded = (counts + MOE_TM - 1) // MOE_TM * MOE_TM
    p_end = jnp.cumsum(padded)
    p_start = p_end - padded
    pos = (p_start[e] + rank).astype(jnp.int32)
    n_used = (p_end[-1] // MOE_TM).astype(jnp.int32)
    tile_start = jnp.arange(n_tiles, dtype=jnp.int32) * MOE_TM
    tile_expert = jnp.minimum(jnp.searchsorted(p_end, tile_start, side='right'), N_EXPERTS - 1).astype(jnp.int32)
    tile_rows = jnp.clip(counts[tile_expert] - (tile_start - p_start[tile_expert]), 0, MOE_TM).astype(jnp.int32)
    return pos, tile_expert, tile_rows, n_used.reshape(1)


def _dispatch_kernel(pos_ref, hp_ref, xs_ref, sem):
    def body(r, carry):
        for k in range(TOP_K):
            dst = pos_ref[0, r * TOP_K + k]
            pltpu.make_async_copy(hp_ref.at[pl.ds(r, 1), :], xs_ref.at[pl.ds(dst, 1), :], sem).start()
        return carry

    lax.fori_loop(0, hp_ref.shape[0], body, 0)
    for k in range(TOP_K):
        pltpu.make_async_copy(hp_ref, xs_ref.at[pl.ds(0, hp_ref.shape[0]), :], sem).wait()


def _dispatch(hp, pos, n_slots):
    t, dw = hp.shape
    tq = min(TOK_TILE, t)
    return pl.pallas_call(
        _dispatch_kernel,
        grid=(t // tq,),
        in_specs=[pl.BlockSpec((None, 1, tq * TOP_K), lambda i: (i, 0, 0), memory_space=pltpu.SMEM),
                  pl.BlockSpec((tq, dw), lambda i: (i, 0))],
        out_specs=pl.BlockSpec(memory_space=pl.ANY),
        out_shape=jax.ShapeDtypeStruct((n_slots, dw), jnp.uint32),
        scratch_shapes=[pltpu.SemaphoreType.DMA(())],
        compiler_params=_params(("arbitrary",), 32),
        name="moe_dispatch",
    )(pos.reshape(t // tq, 1, tq * TOP_K), hp)


def _ffn_kernel(te_ref, tr_ref, nu_ref, x_ref, wg_ref, wu_ref, bg_ref, bu_ref, wd_ref, bd_ref, y_ref,
                xb_ref, act_ref):
    i = pl.program_id(0)
    j = pl.program_id(1)
    rows = tr_ref[i]
    active = i < nu_ref[0]
    half = x_ref.shape[1]
    nf = act_ref.shape[0]
    subs = [(sub, pl.ds(sub * MOE_SUB, MOE_SUB)) for sub in range(MOE_TM // MOE_SUB)]

    @pl.when(active & (j == 0))
    def _():
        lo, hi = _unpack_halves(x_ref[...])
        keep = lax.broadcasted_iota(jnp.int32, lo.shape, 0) < rows
        xb_ref[:, :half] = jnp.where(keep, lo, 0.0).astype(BF16)
        xb_ref[:, half:] = jnp.where(keep, hi, 0.0).astype(BF16)

    @pl.when(active & (j < nf))
    def _():
        wg = wg_ref[...].astype(BF16)
        wu = wu_ref[...].astype(BF16)
        for sub, rs in subs:
            @pl.when(sub * MOE_SUB < rows)
            def _():
                xs = xb_ref[rs, :]
                g = jnp.dot(xs, wg, preferred_element_type=F32) + bg_ref[...]
                u = jnp.dot(xs, wu, preferred_element_type=F32) + bu_ref[...]
                g = jnp.minimum(g, SWIGLU_LIMIT)
                u = jnp.clip(u, -SWIGLU_LIMIT, SWIGLU_LIMIT)
                act_ref[j, rs, :] = ((u + 1.0) * g * jax.nn.sigmoid(SWIGLU_ALPHA * g)).astype(BF16)

    @pl.when(active & (j >= nf))
    def _():
        wd = wd_ref[...].astype(BF16)
        for sub, rs in subs:
            @pl.when(sub * MOE_SUB < rows)
            def _():
                act = jnp.concatenate([act_ref[f, rs, :] for f in range(nf)], axis=1)
                y_ref[rs, :] = jnp.dot(act, wd, preferred_element_type=F32) + bd_ref[...]

            @pl.when(sub * MOE_SUB >= rows)
            def _():
                y_ref[rs, :] = jnp.zeros((MOE_SUB, y_ref.shape[1]), F32)


def _ffn(xs, w_gate_up, b_gate_up, w_down, b_down, layer, tile_expert, tile_rows, n_used):
    n_slots, dw = xs.shape
    d = 2 * dw
    n_tiles = n_slots // MOE_TM
    nf = D_FF // MOE_TF
    nn = d // MOE_TN
    bgu = b_gate_up.reshape(DEPTH, N_EXPERTS, 1, 2 * D_FF)
    bd = b_down.reshape(DEPTH, N_EXPERTS, 1, d)

    def tile(i, nu):
        return jnp.minimum(i, nu[0] - 1)

    def fcol(i, j, nu):
        return jnp.where(i < nu[0], jnp.minimum(j, nf - 1), nf - 1)

    def ncol(i, j, nu):
        return jnp.where(i < nu[0], jnp.maximum(j - nf, 0), nn - 1)

    xmap = lambda i, j, te, tr, nu: (tile(i, nu), 0)
    gmap = lambda i, j, te, tr, nu: (layer, te[tile(i, nu)], 0, fcol(i, j, nu))
    umap = lambda i, j, te, tr, nu: (layer, te[tile(i, nu)], 0, nf + fcol(i, j, nu))
    dmap = lambda i, j, te, tr, nu: (layer, te[tile(i, nu)], 0, ncol(i, j, nu))
    ymap = lambda i, j, te, tr, nu: (tile(i, nu), ncol(i, j, nu))
    grid_spec = pltpu.PrefetchScalarGridSpec(
        num_scalar_prefetch=3,
        grid=(n_tiles, nf + nn),
        in_specs=[
            pl.BlockSpec((MOE_TM, dw), xmap),
            pl.BlockSpec((None, None, d, MOE_TF), gmap),
            pl.BlockSpec((None, None, d, MOE_TF), umap),
            pl.BlockSpec((None, None, 1, MOE_TF), gmap),
            pl.BlockSpec((None, None, 1, MOE_TF), umap),
            pl.BlockSpec((None, None, D_FF, MOE_TN), dmap),
            pl.BlockSpec((None, None, 1, MOE_TN), dmap),
        ],
        out_specs=pl.BlockSpec((MOE_TM, MOE_TN), ymap),
        scratch_shapes=[pltpu.VMEM((MOE_TM, d), BF16), pltpu.VMEM((nf, MOE_TM, MOE_TF), BF16)],
    )
    return pl.pallas_call(
        _ffn_kernel,
        grid_spec=grid_spec,
        out_shape=jax.ShapeDtypeStruct((n_slots, d), F32),
        compiler_params=_params(("arbitrary", "arbitrary"), 56),
        name="moe_ffn",
    )(tile_expert, tile_rows, n_used, xs, w_gate_up, w_gate_up, bgu, bgu, w_down, bd)


def _combine_kernel(pos_ref, ys_ref, gate_ref, h_ref, lng_ref, lnb_ref, hout_ref, hb_ref, buf, sem):
    tq = h_ref.shape[0]

    def body(r, carry):
        for k in range(TOP_K):
            src = pos_ref[0, r * TOP_K + k]
            pltpu.make_async_copy(ys_ref.at[pl.ds(src, 1), :], buf.at[k, pl.ds(r, 1), :], sem).start()
        return carry

    lax.fori_loop(0, tq, body, 0)
    for k in range(TOP_K):
        pltpu.make_async_copy(ys_ref.at[pl.ds(0, tq), :], buf.at[k], sem).wait()
    gate = gate_ref[...]
    acc = DEEPNORM_ALPHA * h_ref[...]
    for k in range(TOP_K):
        acc = acc + gate[:, k:k + 1] * buf[k]
    hn = _layer_norm_rows(acc, lng_ref[...], lnb_ref[...])
    hout_ref[...] = hn
    hb_ref[...] = hn.astype(BF16)


def _combine(ys, pos, gate, h, ln_g, ln_b):
    t, d = h.shape
    tq = min(TOK_TILE, t)
    row = lambda width: pl.BlockSpec((tq, width), lambda i: (i, 0))
    vec = pl.BlockSpec((1, d), lambda i: (0, 0))
    return pl.pallas_call(
        _combine_kernel,
        grid=(t // tq,),
        in_specs=[pl.BlockSpec((None, 1, tq * TOP_K), lambda i: (i, 0, 0), memory_space=pltpu.SMEM),
                  pl.BlockSpec(memory_space=pl.ANY), row(ROUTE_LANES), row(d), vec, vec],
        out_specs=[row(d), row(d)],
        out_shape=[jax.ShapeDtypeStruct((t, d), F32), jax.ShapeDtypeStruct((t, d), BF16)],
        scratch_shapes=[pltpu.VMEM((TOP_K, tq, d), F32), pltpu.SemaphoreType.DMA(())],
        compiler_params=_params(("arbitrary",), 40),
        name="moe_combine",
    )(pos.reshape(t // tq, 1, tq * TOP_K), ys, gate, h, ln_g.reshape(1, d), ln_b.reshape(1, d))


def _hgrn_lower_bound(lb_table, layer):
    p = jax.nn.softmax(lb_table.astype(F32), axis=0)
    return jnp.cumsum(p, axis=0)[layer] - p[0]


def kernel(x, ln_in_g, ln_in_b, w_in, hg_lower_bound, hg_norm_g, hy_conv_w, hy_conv_b, hy_filt_w1, hy_filt_b1,
           hy_filt_w2, hy_filt_b2, hy_filt_freq, hy_filt_w3, hy_skip, att_sink, rel_bias, w_branch, w_out,
           ln_mix_g, ln_mix_b, router_w, router_b, w_gate_up, b_gate_up, w_down, b_down, ln_moe_g, ln_moe_b):
    batch, seq, d = x.shape
    t = batch * seq
    n_tiles = t * TOP_K // MOE_TM + N_EXPERTS
    n_slots = n_tiles * MOE_TM

    fmat, gmat = _dft_tables(seq)
    z_pos, window = _hy_positions(seq)
    bias_tab = _attn_bias_table(rel_bias)

    h, hb = _ln_in(x.reshape(t, d), ln_in_g, ln_in_b)
    for layer in range(DEPTH):
        proj = _inproj(hb, w_in, layer, 0, OFF_AK, PROJ_TN)
        proj_kv = _inproj(hb, w_in, layer, OFF_AK, PROJ_KV_COLS, PROJ_KV_COLS)
        w_gates = lax.slice(w_in, (layer, 0, OFF_GATES), (layer + 1, d, IN_COLS))
        gates = _inproj(hb, w_gates, 0, 0, N_BRANCH * d, PROJ_TN)

        lb = _hgrn_lower_bound(hg_lower_bound, layer)
        o_hg = _hgrn(proj, lb, hg_norm_g[layer], batch, seq)

        hcat = _hy_filters(z_pos, window, hy_filt_w1[layer], hy_filt_b1[layer], hy_filt_w2[layer],
                           hy_filt_b2[layer], hy_filt_freq[layer], hy_filt_w3[layer])
        pr, pi, nyq = _hy_spectrum(fmat, hcat, seq)
        o_hy = _hyena(proj, hy_conv_w[layer], hy_conv_b[layer], hy_skip[layer], fmat, gmat, pr, pi, nyq,
                      batch, seq)

        o_at = _attn(proj, proj_kv, att_sink[layer], bias_tab, batch, seq)

        h_mid, hp, top_idx, gate = _merge(
            o_hg, o_hy, o_at, gates, h, w_branch[layer].astype(BF16), w_out[layer].astype(BF16),
            ln_mix_g[layer], ln_mix_b[layer], router_w[layer], router_b[layer])

        pos, tile_expert, tile_rows, n_used = _route_plan(top_idx[:, :TOP_K], n_tiles)
        xs = _dispatch(hp, pos, n_slots)
        ys = _ffn(xs, w_gate_up, b_gate_up, w_down, b_down, layer, tile_expert, tile_rows, n_used)
        h, hb = _combine(ys, pos, gate, h_mid, ln_moe_g[layer], ln_moe_b[layer])
    return h.reshape(batch, seq, d)
```

```python
import functools
import math

import jax
import jax.numpy as jnp
from jax import lax
from jax.experimental import pallas as pl
from jax.experimental.pallas import tpu as pltpu

F32 = jnp.float32
BF16 = jnp.bfloat16

D_MODEL = 2048
DEPTH = 2
MIX_W = 1024
N_BRANCH = 3
HG_HEADS = 8
HG_DK = 128
HG_DV = 128
HG_CHUNK = 16
F_FLOOR = 1e-30
HY_W = 1024
HY_SHORT = 3
HY_EMB = 33
HY_BANDS = (HY_EMB - 1) // 2
HY_ORDER = 64
HY_INNER = 2
HY_FAST_DECAY = 0.3
HY_SLOW_DECAY = 1.5
HY_TARGET = 1e-2
ATT_HEADS = 16
ATT_KV_HEADS = 2
ATT_DH = 64
WINDOW = 128
ATT_BLOCK = 128
REL_BUCKETS = 32
REL_MAX_DIST = 128
MASK_VALUE = -1e30
N_EXPERTS = 32
TOP_K = 4
D_FF = 2048
SWIGLU_ALPHA = 1.702
SWIGLU_LIMIT = 7.0
LN_EPS = 1e-5
RMS_EPS = 1e-6
DEEPNORM_ALPHA = (2 * DEPTH) ** 0.25

HG_KW = HG_HEADS * HG_DK
HG_VW = HG_HEADS * HG_DV
ATT_QW = ATT_HEADS * ATT_DH
ATT_KVW = ATT_KV_HEADS * ATT_DH
IN_SIZES = (HG_KW, HG_KW, HG_KW, HG_VW, HG_VW, 3 * HY_W, ATT_QW, ATT_KVW, ATT_KVW, N_BRANCH * D_MODEL)
IN_COLS = sum(IN_SIZES)
OFF_HQ, OFF_HFF, OFF_HFB, OFF_HI, OFF_HOG = 0, HG_KW, 2 * HG_KW, 3 * HG_KW, 3 * HG_KW + HG_VW
OFF_HY = OFF_HOG + HG_VW
OFF_AQ = OFF_HY + 3 * HY_W
OFF_AK = OFF_AQ + ATT_QW
OFF_AV = OFF_AK + ATT_KVW
OFF_GATES = OFF_AV + ATT_KVW
MAIN_COLS = OFF_GATES

V7X_LANES = 128
V7X_VMEM_BYTES = 64 * 1024 * 1024

LN_ROWS = 512
PROJ_TM = 1024
PROJ_TN = 1024
PROJ_KV_COLS = 2 * ATT_KVW
HY_TC = 512
MERGE_TM = 512
MOE_TM = 2048
MOE_SUB = 512
MOE_TF = 256
MOE_TN = 256
TOK_TILE = 256
ROUTE_LANES = 128


def _params(semantics, vmem_mb):
    return pltpu.CompilerParams(dimension_semantics=semantics, vmem_limit_bytes=vmem_mb * 1024 * 1024)


def _layer_norm_rows(x, g, b):
    mu = jnp.mean(x, axis=-1, keepdims=True)
    xc = x - mu
    var = jnp.mean(xc * xc, axis=-1, keepdims=True)
    return xc * lax.rsqrt(var + LN_EPS) * g + b


def _pack_halves(x):
    c = x.shape[-1] // 2
    return _pack_pair(x[:, :c], x[:, c:])


def _pack_pair(lo, hi):
    lo = pltpu.bitcast(lo.astype(BF16).astype(F32), jnp.uint32)
    hi = pltpu.bitcast(hi.astype(BF16).astype(F32), jnp.uint32)
    return (lo >> 16) | (hi & jnp.uint32(0xFFFF0000))


def _unpack_halves(w):
    lo = pltpu.bitcast(w << 16, F32)
    hi = pltpu.bitcast(w & jnp.uint32(0xFFFF0000), F32)
    return lo, hi


def _ln_in_kernel(x_ref, g_ref, b_ref, h_ref, hb_ref):
    y = _layer_norm_rows(x_ref[...], g_ref[...], b_ref[...])
    h_ref[...] = y
    hb_ref[...] = y.astype(BF16)


def _ln_in(x2, g, b):
    t, d = x2.shape
    tm = min(LN_ROWS, t)
    row = pl.BlockSpec((tm, d), lambda i: (i, 0))
    vec = pl.BlockSpec((1, d), lambda i: (0, 0))
    return pl.pallas_call(
        _ln_in_kernel,
        grid=(t // tm,),
        in_specs=[row, vec, vec],
        out_specs=[row, row],
        out_shape=[jax.ShapeDtypeStruct((t, d), F32), jax.ShapeDtypeStruct((t, d), BF16)],
        compiler_params=_params(("arbitrary",), 32),
        name="ln_in",
    )(x2, g.reshape(1, d), b.reshape(1, d))


def _inproj_kernel(a_ref, w_ref, o_ref, wb_ref):
    @pl.when(pl.program_id(1) == 0)
    def _():
        wb_ref[...] = w_ref[...].astype(BF16)

    o_ref[...] = jnp.dot(a_ref[...], wb_ref[...], preferred_element_type=F32).astype(o_ref.dtype)


def _inproj(hb, w_in, layer, col0, ncols, tn):
    t, d = hb.shape
    tm = min(PROJ_TM, t)
    cb0 = col0 // tn
    return pl.pallas_call(
        _inproj_kernel,
        grid=(ncols // tn, t // tm),
        in_specs=[
            pl.BlockSpec((tm, d), lambda j, i: (i, 0)),
            pl.BlockSpec((None, d, tn), lambda j, i: (layer, 0, cb0 + j)),
        ],
        out_specs=pl.BlockSpec((tm, tn), lambda j, i: (i, j)),
        out_shape=jax.ShapeDtypeStruct((t, ncols), BF16),
        scratch_shapes=[pltpu.VMEM((d, tn), BF16)],
        compiler_params=_params(("arbitrary", "arbitrary"), 48),
        name="inproj",
    )(hb, w_in)


HG_BLK = 128
HG_CPB = HG_BLK // HG_CHUNK
HG_UNROLL = 16
LOG2E = 1.4426950408889634


def _hgrn_perm():
    r = jnp.arange(HG_BLK)
    src = (r % HG_CPB) * HG_CHUNK + r // HG_CPB
    return (src[:, None] == jnp.arange(HG_BLK)[None, :]).astype(BF16)


def _hgrn_kernel(q_ref, ff_ref, fb_ref, i_ref, og_ref, lbf_ref, lbb_ref, g_ref, p_ref, pt_ref, o_ref,
                 x_t, kk_t, o_t, tmp_t, qtil_c, ktil_c, kv_s, oint_c, dec_c, *, seq):
    c = HG_CHUNK
    nc = seq // c
    nb = seq // HG_BLK
    nbh = max(nb // 2, 1)
    n_half = nb // nbh
    perm = p_ref[...]
    perm_t = pt_ref[...]

    for a, ref in enumerate((q_ref, ff_ref, fb_ref, i_ref)):
        for r in range(nb):
            xp = jnp.dot(perm, ref[r * HG_BLK:(r + 1) * HG_BLK, :], preferred_element_type=F32)
            x_t[a, r] = xp.reshape(c, HG_CPB, HG_DK)

    lbs = (lbf_ref[...], lbb_ref[...])
    for j in range(c):
        qj = x_t[0, :, j]
        x_t[0, :, j] = qj * jax.nn.sigmoid(qj) * (HG_DK ** -0.5)
        for d in range(2):
            z = x_t[1 + d, :, j]
            lb = lbs[d]
            e = jnp.exp(-jnp.abs(z))
            r_ = 1.0 / (1.0 + e)
            er = e * r_
            sig = jnp.where(z >= 0, r_, er)
            nsig = jnp.where(z >= 0, er, r_)
            f = lb + (1.0 - lb) * sig
            x_t[1 + d, :, j] = jnp.log(jnp.maximum(f, F_FLOOR)) * LOG2E
            kk_t[d, :, j] = (1.0 - lb) * nsig

    for d in range(2):
        acc = jnp.zeros((nb, HG_CPB, HG_DK), F32)
        for j in (range(c) if d == 0 else reversed(range(c))):
            acc = acc + x_t[1 + d, :, j]
            x_t[1 + d, :, j] = acc
        last = acc
        dec_c[d] = jnp.exp2(last).reshape(nc, HG_DK)
        for j in range(c):
            cj = x_t[1 + d, :, j]
            tmp_t[0, :, j] = x_t[0, :, j] * jnp.exp2(cj)
            tmp_t[1, :, j] = kk_t[d, :, j] * jnp.exp2(last - cj)
        for r in range(nb):
            rows = slice(r * HG_BLK, (r + 1) * HG_BLK)
            qb = tmp_t[0, r].reshape(HG_BLK, HG_DK).astype(BF16)
            kb = tmp_t[1, r].reshape(HG_BLK, HG_DK).astype(BF16)
            qtil_c[d, rows, :] = jnp.dot(perm_t, qb, preferred_element_type=F32).astype(BF16)
            ktil_c[rows, d * HG_DK:(d + 1) * HG_DK] = jnp.dot(perm_t, kb, preferred_element_type=F32).astype(BF16)

    o_t[...] = jnp.zeros(o_t.shape, F32)
    ones_b = jnp.ones((HG_DK, HG_DV), BF16)

    def pair_body(it, carry):
        d = it // n_half
        sl = pl.ds((it % n_half) * nbh, nbh)
        for t in range(c):
            rt = t + d * (c - 1 - 2 * t)
            ct = x_t[1 + d, sl, rt]
            qt = x_t[0, sl, rt]
            acc = jnp.zeros((nbh * HG_CPB, HG_DV), F32)
            for s in range(t + 1):
                rs = s + d * (c - 1 - 2 * s)
                e = jnp.exp2(jnp.minimum(ct - x_t[1 + d, sl, rs], 0.0))
                a = (qt * e * kk_t[d, sl, rs]).reshape(nbh * HG_CPB, HG_DK).astype(BF16)
                p = jnp.dot(a, ones_b, preferred_element_type=F32)
                acc = acc + p * x_t[3, sl, rs].reshape(nbh * HG_CPB, HG_DV)
            o_t[sl, rt] += acc.reshape(nbh, HG_CPB, HG_DV)
        return carry

    lax.fori_loop(0, 2 * n_half, pair_body, 0)

    def kv_body(n, carry):
        r0 = pl.multiple_of(n * c, c)
        kv = lax.dot_general(i_ref[pl.ds(r0, c), :], ktil_c[pl.ds(r0, c), :], (((0,), (0,)), ((), ())),
                             preferred_element_type=F32)
        kv_s[0, n] = kv[:, :HG_DK]
        kv_s[1, n] = kv[:, HG_DK:]
        return carry

    lax.fori_loop(0, nc, kv_body, 0, unroll=HG_UNROLL)

    def chain_body(idx, carry):
        new = []
        for d in range(2):
            n = idx if d == 0 else nc - 1 - idx
            s = carry[d]
            new.append(dec_c[d, pl.ds(n, 1), :] * s + kv_s[d, n])
            kv_s[d, n] = s
        return tuple(new)

    zero_state = jnp.zeros((HG_DV, HG_DK), F32)
    lax.fori_loop(0, nc, chain_body, (zero_state, zero_state), unroll=2)

    def out_body(n, carry):
        r0 = pl.multiple_of(n * c, c)
        for d in range(2):
            oint_c[d, pl.ds(r0, c), :] = lax.dot_general(
                qtil_c[d, pl.ds(r0, c), :], kv_s[d, n].astype(BF16), (((1,), (1,)), ((), ())),
                preferred_element_type=F32)
        return carry

    lax.fori_loop(0, nc, out_body, 0, unroll=HG_UNROLL)

    for r in range(nb):
        rows = slice(r * HG_BLK, (r + 1) * HG_BLK)
        ob = o_t[r].reshape(HG_BLK, HG_DV)
        hi = ob.astype(BF16)
        lo = (ob - hi.astype(F32)).astype(BF16)
        oc = jnp.dot(perm_t, hi, preferred_element_type=F32) + jnp.dot(perm_t, lo, preferred_element_type=F32)
        o = oc + oint_c[0, rows, :] + oint_c[1, rows, :]
        o = o * lax.rsqrt(jnp.mean(o * o, axis=-1, keepdims=True) + RMS_EPS) * g_ref[...]
        og = og_ref[rows, :].astype(F32)
        o_ref[rows, :] = (o * (og * jax.nn.sigmoid(og))).astype(o_ref.dtype)


def _hgrn(proj, lb, norm_g, batch, seq):
    t = batch * seq
    nc = seq // HG_CHUNK

    def col(off):
        return pl.BlockSpec((seq, HG_DK), lambda b, h: (b, off // HG_DK + h))

    vec = pl.BlockSpec((None, 1, HG_DK), lambda b, h: (h, 0, 0))
    lbf = lb[:HG_KW].reshape(HG_HEADS, 1, HG_DK)
    lbb = lb[HG_KW:].reshape(HG_HEADS, 1, HG_DK)
    g = norm_g.reshape(HG_HEADS, 1, HG_DV)
    slab = (seq // HG_BLK, HG_CHUNK, HG_CPB, HG_DK)
    perm = _hgrn_perm()
    pspec = pl.BlockSpec((HG_BLK, HG_BLK), lambda b, h: (0, 0))
    return pl.pallas_call(
        functools.partial(_hgrn_kernel, seq=seq),
        grid=(batch, HG_HEADS),
        in_specs=[col(OFF_HQ), col(OFF_HFF), col(OFF_HFB), col(OFF_HI), col(OFF_HOG), vec, vec, vec, pspec, pspec],
        out_specs=pl.BlockSpec((seq, HG_DV), lambda b, h: (b, h)),
        out_shape=jax.ShapeDtypeStruct((t, HG_VW), BF16),
        scratch_shapes=[
            pltpu.VMEM((4,) + slab, F32),
            pltpu.VMEM((2,) + slab, F32),
            pltpu.VMEM(slab, F32),
            pltpu.VMEM((2,) + slab, F32),
            pltpu.VMEM((2, seq, HG_DK), BF16),
            pltpu.VMEM((seq, 2 * HG_DK), BF16),
            pltpu.VMEM((2, nc, HG_DV, HG_DK), F32),
            pltpu.VMEM((2, seq, HG_DV), F32),
            pltpu.VMEM((2, nc, HG_DK), F32),
        ],
        compiler_params=_params(("arbitrary", "arbitrary"), 48),
        name="hgrn2",
    )(proj, proj, proj, proj, proj, lbf, lbb, g, perm, perm.T)


def _attn_kernel(sink_ref, q_ref, kp_ref, ko_ref, kn_ref, vp_ref, vo_ref, vn_ref, bias_ref, o_ref, s_ref, p_ref):
    w = ATT_BLOCK
    group = ATT_HEADS // ATT_KV_HEADS
    kband = jnp.concatenate([kp_ref[...], ko_ref[...], kn_ref[...]], axis=0)
    vband = jnp.concatenate([vp_ref[...], vo_ref[...], vn_ref[...]], axis=0)
    ones = jnp.ones((3 * w, ATT_DH), BF16)
    q = q_ref[...] * jnp.asarray(ATT_DH ** -0.5, BF16)
    kgs = [kband[:, g * ATT_DH:(g + 1) * ATT_DH] for g in range(ATT_KV_HEADS)]
    vg1s = [jnp.concatenate([vband[:, g * ATT_DH:(g + 1) * ATT_DH], ones], axis=1) for g in range(ATT_KV_HEADS)]
    for h in range(ATT_HEADS):
        s_ref[h] = lax.dot_general(q[:, h * ATT_DH:(h + 1) * ATT_DH], kgs[h // group], (((1,), (1,)), ((), ())),
                                   preferred_element_type=F32)
    exps = []
    for h in range(ATT_HEADS):
        s = s_ref[h] * LOG2E + bias_ref[h]
        sk = sink_ref[h] * LOG2E
        m = jnp.maximum(jnp.max(s, axis=-1, keepdims=True), sk)
        p_ref[h] = jnp.exp2(s - m).astype(BF16)
        exps.append(jnp.exp2(sk - m))
    outs = []
    for h in range(ATT_HEADS):
        ov = jnp.dot(p_ref[h], vg1s[h // group], preferred_element_type=F32)
        outs.append(ov[:, :ATT_DH] / (ov[:, ATT_DH:ATT_DH + 1] + exps[h]))
    o_ref[...] = jnp.concatenate(outs, axis=1).astype(o_ref.dtype)


def _t5_relative_bucket(rel):
    half = REL_BUCKETS // 2
    max_exact = half // 2
    bucket = (rel > 0).astype(jnp.int32) * half
    n = jnp.abs(rel)
    n_safe = jnp.maximum(n, 1).astype(F32)
    large = max_exact + (jnp.log(n_safe / max_exact) / math.log(REL_MAX_DIST / max_exact)
                         * (half - max_exact)).astype(jnp.int32)
    large = jnp.clip(large, 0, half - 1)
    return bucket + jnp.where(n < max_exact, n, large)


def _attn_bias_table(rel_bias):
    w = ATT_BLOCK
    kofs = jnp.arange(3 * w, dtype=jnp.int32)[None, :] - w
    rel = kofs - jnp.arange(w, dtype=jnp.int32)[:, None]
    onehot = (_t5_relative_bucket(rel)[:, :, None] == jnp.arange(REL_BUCKETS)[None, None, :]).astype(F32)
    bias = jnp.einsum('qkb,bh->hqk', onehot, rel_bias.astype(F32), precision=lax.Precision.HIGHEST)
    band = jnp.abs(rel) <= WINDOW
    tabs = []
    for first, last in ((False, False), (True, False), (False, True), (True, True)):
        ok = band & ((kofs >= 0) | (not first)) & ((kofs < w) | (not last))
        tabs.append(jnp.where(ok[None], bias * LOG2E, MASK_VALUE))
    return jnp.stack(tabs)


def _attn(proj, proj_kv, sink, bias_tab, batch, seq):
    t = batch * seq
    w = ATT_BLOCK
    nb = seq // w
    kcol, vcol = 0, 1

    def kv(col, delta):
        return pl.BlockSpec((w, ATT_KVW), lambda b, n: (b * nb + jnp.clip(n + delta, 0, nb - 1), col))

    def variant(b, n):
        return ((n == 0).astype(jnp.int32) + 2 * (n == nb - 1).astype(jnp.int32), 0, 0, 0)

    return pl.pallas_call(
        _attn_kernel,
        grid=(batch, nb),
        in_specs=[
            pl.BlockSpec(memory_space=pltpu.SMEM),
            pl.BlockSpec((w, ATT_QW), lambda b, n: (b * nb + n, OFF_AQ // ATT_QW)),
            kv(kcol, -1), kv(kcol, 0), kv(kcol, 1),
            kv(vcol, -1), kv(vcol, 0), kv(vcol, 1),
            pl.BlockSpec((None, ATT_HEADS, w, 3 * w), variant),
        ],
        out_specs=pl.BlockSpec((w, ATT_QW), lambda b, n: (b * nb + n, 0)),
        out_shape=jax.ShapeDtypeStruct((t, ATT_QW), BF16),
        scratch_shapes=[pltpu.VMEM((ATT_HEADS, w, 3 * w), F32), pltpu.VMEM((ATT_HEADS, w, 3 * w), BF16)],
        compiler_params=_params(("arbitrary", "arbitrary"), 32),
        name="win_attn",
    )(sink.astype(F32), proj, proj_kv, proj_kv, proj_kv, proj_kv, proj_kv, proj_kv, bias_tab)


def _dft_tables(seq):
    n = 2 * seq
    k = jnp.arange(seq, dtype=jnp.int32)[:, None]
    s = jnp.arange(seq, dtype=jnp.int32)[None, :]
    ang = ((k * s) % n).astype(F32) * (2.0 * math.pi / n)
    cm = jnp.cos(ang)
    sm = -jnp.sin(ang)
    nyq = jnp.where(s % 2 == 0, 1.0, -1.0).astype(F32)
    sm = jnp.where(k == 0, nyq, sm)
    f = jnp.concatenate([cm, sm], axis=0).astype(BF16)
    return f, f.T


def _hy_positions(seq):
    t = jnp.linspace(0.0, 1.0, seq, dtype=F32)[:, None]
    w = 2.0 * math.pi * jnp.arange(seq, dtype=F32)[:, None] / seq
    f = jnp.linspace(1e-4, HY_BANDS - 1, HY_BANDS, dtype=F32)[None]
    z = jnp.concatenate([t, jnp.cos(f * w), -jnp.sin(f * w)], axis=-1)
    z = jnp.pad(z, ((0, 0), (0, V7X_LANES - HY_EMB)))
    max_decay = math.log(HY_TARGET) / HY_FAST_DECAY
    min_decay = math.log(HY_TARGET) / HY_SLOW_DECAY
    deltas = jnp.linspace(min_decay, max_decay, HY_W, dtype=F32)
    window = jnp.exp(-t * jnp.abs(deltas))
    return z, window


def _hy_filter_kernel(z_ref, w1_ref, b1_ref, w2_ref, b2_ref, fr_ref, w3_ref, win_ref, h_ref):
    dot = functools.partial(jnp.dot, precision=lax.Precision.HIGHEST, preferred_element_type=F32)
    fr = fr_ref[...]
    h = jnp.sin(fr * (dot(z_ref[...], w1_ref[...]) + b1_ref[...]))
    for j in range(HY_INNER):
        h = jnp.sin(fr * (dot(h, w2_ref[j]) + b2_ref[j]))
    h = dot(h, w3_ref[...]) * win_ref[...]
    row = lax.broadcasted_iota(jnp.int32, h.shape, 0)
    backward = pl.program_id(0) >= pl.num_programs(0) // 2
    h_ref[...] = jnp.where((row == 0) & backward, 0.0, h).astype(h_ref.dtype)


def _hy_filters(z, window, w1, b1, w2, b2, freq, w3):
    seq = z.shape[0]
    tn = HY_TC
    per_dir = HY_W // tn
    full = lambda shape: pl.BlockSpec(shape, lambda j: (0,) * len(shape))
    w1p = jnp.pad(w1, ((0, V7X_LANES - HY_EMB), (0, 0)))
    return pl.pallas_call(
        _hy_filter_kernel,
        grid=(2 * per_dir,),
        in_specs=[
            full((seq, V7X_LANES)), full((V7X_LANES, HY_ORDER)), full((1, HY_ORDER)),
            full((HY_INNER, HY_ORDER, HY_ORDER)), full((HY_INNER, 1, HY_ORDER)), full((1, HY_ORDER)),
            pl.BlockSpec((HY_ORDER, tn), lambda j: (0, j)),
            pl.BlockSpec((seq, tn), lambda j: (0, j % per_dir)),
        ],
        out_specs=pl.BlockSpec((seq, tn), lambda j: (0, j)),
        out_shape=jax.ShapeDtypeStruct((seq, 2 * HY_W), BF16),
        compiler_params=_params(("arbitrary",), 32),
        name="hy_filter",
    )(z, w1p, b1.reshape(1, HY_ORDER), w2, b2.reshape(HY_INNER, 1, HY_ORDER), freq.reshape(1, HY_ORDER), w3, window)


def _mm_kernel(a_ref, b_ref, o_ref):
    o_ref[...] = jnp.dot(a_ref[...], b_ref[...], preferred_element_type=F32).astype(o_ref.dtype)


def _mm(a, b, tm, tn, out_dtype):
    m, k = a.shape
    n = b.shape[1]
    return pl.pallas_call(
        _mm_kernel,
        grid=(m // tm, n // tn),
        in_specs=[pl.BlockSpec((tm, k), lambda i, j: (i, 0)), pl.BlockSpec((k, tn), lambda i, j: (0, j))],
        out_specs=pl.BlockSpec((tm, tn), lambda i, j: (i, j)),
        out_shape=jax.ShapeDtypeStruct((m, n), out_dtype),
        compiler_params=_params(("arbitrary", "arbitrary"), 32),
        name="mm",
    )(a, b)


def _hy_spectrum(fmat, hcat, seq):
    spec = _mm(fmat, hcat, min(1024, 2 * seq), 512, F32)
    top, bot = spec[:seq], spec[seq:]
    kr = top[:, :HY_W] + top[:, HY_W:]
    ki = bot[:, :HY_W] - bot[:, HY_W:]
    n = 2 * seq
    first = (jnp.arange(seq) == 0)[:, None]
    pr = jnp.where(first, kr / n, kr * (2.0 / n))
    pi = jnp.where(first, 0.0, ki * (2.0 / n))
    nyq = (bot[:1, :HY_W] + bot[:1, HY_W:]) / n
    return pr, pi, nyq


def _short_conv(x, w_ref, b_ref):
    seq = x.shape[0]
    row = lax.broadcasted_iota(jnp.int32, x.shape, 0)
    prev = jnp.where(row == 0, 0.0, pltpu.roll(x, 1, 0))
    nxt = jnp.where(row == seq - 1, 0.0, pltpu.roll(x, seq - 1, 0))
    return w_ref[0:1, :] * prev + w_ref[1:2, :] * x + w_ref[2:3, :] * nxt + b_ref[...]


def _hy_fwd_kernel(x1_ref, v_ref, w1_ref, b1_ref, wv_ref, bv_ref, f_ref, pr_ref, pi_ref, nyq_ref, y_ref):
    seq = x1_ref.shape[0]
    x1 = _short_conv(x1_ref[...].astype(F32), w1_ref, b1_ref)
    v = _short_conv(v_ref[...].astype(F32), wv_ref, bv_ref)
    u = (x1 * v).astype(BF16)
    w = jnp.dot(f_ref[...], u, preferred_element_type=F32)
    a, b = w[:seq], w[seq:]
    pr, pi = pr_ref[...], pi_ref[...]
    row = lax.broadcasted_iota(jnp.int32, pr.shape, 0)
    pd = jnp.where(row == 0, nyq_ref[...], pr)
    y_ref[:seq, :] = (a * pr - b * pi).astype(y_ref.dtype)
    y_ref[seq:, :] = (a * pi + b * pd).astype(y_ref.dtype)


def _hy_inv_kernel(y_ref, x0_ref, x1_ref, v_ref, w0_ref, b0_ref, w1_ref, b1_ref, wv_ref, bv_ref, skip_ref,
                   g_ref, o_ref):
    y = jnp.dot(g_ref[...], y_ref[...], preferred_element_type=F32)
    x0 = _short_conv(x0_ref[...].astype(F32), w0_ref, b0_ref)
    x1 = _short_conv(x1_ref[...].astype(F32), w1_ref, b1_ref)
    v = _short_conv(v_ref[...].astype(F32), wv_ref, bv_ref)
    u = x1 * v
    o_ref[...] = (x0 * (y + u * skip_ref[...])).astype(o_ref.dtype)


def _hyena(proj, conv_w, conv_b, skip, fmat, gmat, pr, pi, nyq, batch, seq):
    t = batch * seq
    tc = HY_TC // 2
    nct = HY_W // tc
    conv_b2 = conv_b.reshape(1, 3 * HY_W)

    def xcol(part):
        return pl.BlockSpec((seq, tc), lambda c, b: (b, (OFF_HY + part * HY_W) // tc + c))

    def wcol(part):
        return pl.BlockSpec((HY_SHORT, tc), lambda c, b: (0, part * nct + c))

    def bcol(part):
        return pl.BlockSpec((1, tc), lambda c, b: (0, part * nct + c))

    chan = pl.BlockSpec((seq, tc), lambda c, b: (0, c))
    chan1 = pl.BlockSpec((1, tc), lambda c, b: (0, c))
    once = pl.Buffered(1)
    yspec = pl.BlockSpec((None, 2 * seq, tc), lambda c, b: (b, 0, c))
    yfreq = pl.pallas_call(
        _hy_fwd_kernel,
        grid=(nct, batch),
        in_specs=[xcol(1), xcol(2), wcol(1), bcol(1), wcol(2), bcol(2),
                  pl.BlockSpec((2 * seq, seq), lambda c, b: (0, 0), pipeline_mode=once),
                  chan, chan, chan1],
        out_specs=yspec,
        out_shape=jax.ShapeDtypeStruct((batch, 2 * seq, HY_W), BF16),
        compiler_params=_params(("arbitrary", "arbitrary"), 56),
        name="hy_fwd",
    )(proj, proj, conv_w, conv_b2, conv_w, conv_b2, fmat, pr, pi, nyq)
    return pl.pallas_call(
        _hy_inv_kernel,
        grid=(nct, batch),
        in_specs=[yspec, xcol(0), xcol(1), xcol(2), wcol(0), bcol(0), wcol(1), bcol(1), wcol(2), bcol(2), chan1,
                  pl.BlockSpec((seq, 2 * seq), lambda c, b: (0, 0), pipeline_mode=once)],
        out_specs=pl.BlockSpec((seq, tc), lambda c, b: (b, c)),
        out_shape=jax.ShapeDtypeStruct((t, HY_W), BF16),
        compiler_params=_params(("arbitrary", "arbitrary"), 56),
        name="hy_inv",
    )(yfreq, proj, proj, proj, conv_w, conv_b2, conv_w, conv_b2, conv_w, conv_b2, skip.reshape(1, HY_W), gmat)


def _branch_kernel(ohg_ref, ohy_ref, oat_ref, gates_ref, wb_ref, m_ref):
    d = D_MODEL
    m = None
    for n, o_ref in enumerate((ohg_ref, ohy_ref, oat_ref)):
        br = jnp.dot(o_ref[...], wb_ref[n], preferred_element_type=F32)
        term = jax.nn.sigmoid(gates_ref[:, n * d:(n + 1) * d].astype(F32)) * br
        m = term if m is None else m + term
    m_ref[...] = m.astype(m_ref.dtype)


def _merge_kernel(m_ref, h_ref, wo_ref, lng_ref, lnb_ref, rwh_ref, rwl_ref, rb_ref,
                  hmid_ref, hp_ref, idx_ref, gate_ref):
    y = jnp.dot(m_ref[...], wo_ref[...], preferred_element_type=F32)
    hn = _layer_norm_rows(DEEPNORM_ALPHA * h_ref[...] + y, lng_ref[...], lnb_ref[...])
    hmid_ref[...] = hn
    hp_ref[...] = _pack_halves(hn)

    h_hi = hn.astype(BF16)
    h_lo = (hn - h_hi.astype(F32)).astype(BF16)
    logits = (jnp.dot(h_hi, rwh_ref[...], preferred_element_type=F32)
              + jnp.dot(h_lo, rwh_ref[...], preferred_element_type=F32)
              + jnp.dot(h_hi, rwl_ref[...], preferred_element_type=F32)) + rb_ref[...]
    lane = lax.broadcasted_iota(jnp.int32, logits.shape, 1)
    vals, idxs = [], []
    for _ in range(TOP_K):
        mx = jnp.max(logits, axis=-1, keepdims=True)
        ix = jnp.min(jnp.where(logits == mx, lane, ROUTE_LANES), axis=-1, keepdims=True)
        vals.append(mx)
        idxs.append(ix)
        logits = jnp.where(lane == ix, -jnp.inf, logits)
    exps = [jnp.exp(v - vals[0]) for v in vals]
    total = exps[0]
    for e in exps[1:]:
        total = total + e
    gate_out = jnp.zeros(logits.shape, F32)
    idx_out = jnp.zeros(logits.shape, jnp.int32)
    for r in range(TOP_K):
        gate_out = jnp.where(lane == r, exps[r] / total, gate_out)
        idx_out = jnp.where(lane == r, idxs[r], idx_out)
    gate_ref[...] = gate_out
    idx_ref[...] = idx_out


def _merge(o_hg, o_hy, o_at, gates, h, wb, wo, ln_g, ln_b, rw, rb):
    t, d = h.shape
    tm = min(MERGE_TM, t)
    once = pl.Buffered(1)
    row = lambda width: pl.BlockSpec((tm, width), lambda i: (i, 0))
    const = lambda shape: pl.BlockSpec(shape, lambda i: (0,) * len(shape), pipeline_mode=once)
    rwp = jnp.pad(rw, ((0, 0), (0, ROUTE_LANES - N_EXPERTS)))
    rw_hi = rwp.astype(BF16)
    rw_lo = (rwp - rw_hi.astype(F32)).astype(BF16)
    rbp = jnp.pad(rb, (0, ROUTE_LANES - N_EXPERTS), constant_values=MASK_VALUE).reshape(1, ROUTE_LANES)
    m = pl.pallas_call(
        _branch_kernel,
        grid=(t // tm,),
        in_specs=[row(MIX_W), row(MIX_W), row(MIX_W), row(N_BRANCH * d), const((N_BRANCH, MIX_W, d))],
        out_specs=row(d),
        out_shape=jax.ShapeDtypeStruct((t, d), BF16),
        compiler_params=_params(("arbitrary",), 56),
        name="branch_merge",
    )(o_hg, o_hy, o_at, gates, wb)
    return pl.pallas_call(
        _merge_kernel,
        grid=(t // tm,),
        in_specs=[row(d), row(d), const((d, d)), const((1, d)), const((1, d)),
                  const((d, ROUTE_LANES)), const((d, ROUTE_LANES)), const((1, ROUTE_LANES))],
        out_specs=[row(d), row(d // 2), row(ROUTE_LANES), row(ROUTE_LANES)],
        out_shape=[jax.ShapeDtypeStruct((t, d), F32), jax.ShapeDtypeStruct((t, d // 2), jnp.uint32),
                   jax.ShapeDtypeStruct((t, ROUTE_LANES), jnp.int32), jax.ShapeDtypeStruct((t, ROUTE_LANES), F32)],
        compiler_params=_params(("arbitrary",), 56),
        name="merge",
    )(m, h, wo, ln_g.reshape(1, d), ln_b.reshape(1, d), rw_hi, rw_lo, rbp)


def _route_plan(top_idx, n_tiles):
    e = top_idx.reshape(-1)
    onehot = (e[:, None] == jnp.arange(N_EXPERTS, dtype=jnp.int32)[None, :]).astype(jnp.int32)
    csum = jnp.cumsum(onehot, axis=0)
    rank = jnp.sum(csum * onehot, axis=1) - 1
    counts = csum[-1]
    padded = (counts + MOE_TM - 1) // MOE_TM * MOE_TM
    p_end = jnp.cumsum(padded)
    p_start = p_end - padded
    pos = (p_start[e] + rank).astype(jnp.int32)
    n_used = (p_end[-1] // MOE_TM).astype(jnp.int32)
    tile_start = jnp.arange(n_tiles, dtype=jnp.int32) * MOE_TM
    tile_expert = jnp.minimum(jnp.searchsorted(p_end, tile_start, side='right'), N_EXPERTS - 1).astype(jnp.int32)
    tile_rows = jnp.clip(counts[tile_expert] - (tile_start - p_start[tile_expert]), 0, MOE_TM).astype(jnp.int32)
    return pos, tile_expert, tile_rows, n_used.reshape(1)


def _dispatch_kernel(pos_ref, hp_ref, xs_ref, sem):
    def body(r, carry):
        for k in range(TOP_K):
            dst = pos_ref[0, r * TOP_K + k]
            pltpu.make_async_copy(hp_ref.at[pl.ds(r, 1), :], xs_ref.at[pl.ds(dst, 1), :], sem).start()
        return carry

    lax.fori_loop(0, hp_ref.shape[0], body, 0)
    for k in range(TOP_K):
        pltpu.make_async_copy(hp_ref, xs_ref.at[pl.ds(0, hp_ref.shape[0]), :], sem).wait()


def _dispatch(hp, pos, n_slots):
    t, dw = hp.shape
    tq = min(TOK_TILE, t)
    return pl.pallas_call(
        _dispatch_kernel,
        grid=(t // tq,),
        in_specs=[pl.BlockSpec((None, 1, tq * TOP_K), lambda i: (i, 0, 0), memory_space=pltpu.SMEM),
                  pl.BlockSpec((tq, dw), lambda i: (i, 0))],
        out_specs=pl.BlockSpec(memory_space=pl.ANY),
        out_shape=jax.ShapeDtypeStruct((n_slots, dw), jnp.uint32),
        scratch_shapes=[pltpu.SemaphoreType.DMA(())],
        compiler_params=_params(("arbitrary",), 32),
        name="moe_dispatch",
    )(pos.reshape(t // tq, 1, tq * TOP_K), hp)


def _ffn_kernel(te_ref, tr_ref, nu_ref, x_ref, wg_ref, wu_ref, bg_ref, bu_ref, wdl_ref, wdh_ref, bdl_ref, bdh_ref,
                y_ref, xb_ref, act_ref):
    i = pl.program_id(0)
    j = pl.program_id(1)
    rows = tr_ref[i]
    active = i < nu_ref[0]
    half = x_ref.shape[1]
    nf = D_FF // MOE_TF
    nsub = (rows + MOE_SUB - 1) // MOE_SUB

    @pl.when(active & (j == 0))
    def _():
        lo, hi = _unpack_halves(x_ref[...])
        keep = lax.broadcasted_iota(jnp.int32, lo.shape, 0) < rows
        xb_ref[:, :half] = jnp.where(keep, lo, 0.0).astype(BF16)
        xb_ref[:, half:] = jnp.where(keep, hi, 0.0).astype(BF16)

    for k in range(1, MOE_TM // MOE_SUB + 1):
        m = k * MOE_SUB

        @pl.when(active & (j < nf) & (nsub == k))
        def _():
            xs = xb_ref[:m, :]
            g = jnp.dot(xs, wg_ref[...].astype(BF16), preferred_element_type=F32) + bg_ref[...]
            u = jnp.dot(xs, wu_ref[...].astype(BF16), preferred_element_type=F32) + bu_ref[...]
            g = jnp.minimum(g, SWIGLU_LIMIT)
            u = jnp.clip(u, -SWIGLU_LIMIT, SWIGLU_LIMIT)
            col = pl.ds(pl.multiple_of(j * MOE_TF, MOE_TF), MOE_TF)
            act_ref[:m, col] = ((u + 1.0) * g * jax.nn.sigmoid(SWIGLU_ALPHA * g)).astype(BF16)

        @pl.when(active & (j >= nf) & (nsub == k))
        def _():
            act = act_ref[:m, :]
            lo = jnp.dot(act, wdl_ref[...].astype(BF16), preferred_element_type=F32) + bdl_ref[...]
            hi = jnp.dot(act, wdh_ref[...].astype(BF16), preferred_element_type=F32) + bdh_ref[...]
            y_ref[:m, :] = _pack_pair(lo, hi)
            if m < MOE_TM:
                y_ref[m:, :] = jnp.zeros((MOE_TM - m, y_ref.shape[1]), jnp.uint32)


def _ffn(xs, w_gate_up, b_gate_up, w_down, b_down, layer, tile_expert, tile_rows, n_used):
    n_slots, dw = xs.shape
    d = 2 * dw
    n_tiles = n_slots // MOE_TM
    nf = D_FF // MOE_TF
    nn = dw // MOE_TN
    bgu = b_gate_up.reshape(DEPTH, N_EXPERTS, 1, 2 * D_FF)
    bd = b_down.reshape(DEPTH, N_EXPERTS, 1, d)

    def tile(i, nu):
        return jnp.minimum(i, nu[0] - 1)

    def fcol(i, j, nu):
        return jnp.where(i < nu[0], jnp.minimum(j, nf - 1), nf - 1)

    def ncol(i, j, nu):
        return jnp.where(i < nu[0], jnp.maximum(j - nf, 0), nn - 1)

    xmap = lambda i, j, te, tr, nu: (tile(i, nu), 0)
    gmap = lambda i, j, te, tr, nu: (layer, te[tile(i, nu)], 0, fcol(i, j, nu))
    umap = lambda i, j, te, tr, nu: (layer, te[tile(i, nu)], 0, nf + fcol(i, j, nu))
    dlmap = lambda i, j, te, tr, nu: (layer, te[tile(i, nu)], 0, ncol(i, j, nu))
    dhmap = lambda i, j, te, tr, nu: (layer, te[tile(i, nu)], 0, nn + ncol(i, j, nu))
    ymap = lambda i, j, te, tr, nu: (tile(i, nu), ncol(i, j, nu))
    grid_spec = pltpu.PrefetchScalarGridSpec(
        num_scalar_prefetch=3,
        grid=(n_tiles, nf + nn),
        in_specs=[
            pl.BlockSpec((MOE_TM, dw), xmap, pipeline_mode=pl.Buffered(1)),
            pl.BlockSpec((None, None, d, MOE_TF), gmap),
            pl.BlockSpec((None, None, d, MOE_TF), umap),
            pl.BlockSpec((None, None, 1, MOE_TF), gmap),
            pl.BlockSpec((None, None, 1, MOE_TF), umap),
            pl.BlockSpec((None, None, D_FF, MOE_TN), dlmap),
            pl.BlockSpec((None, None, D_FF, MOE_TN), dhmap),
            pl.BlockSpec((None, None, 1, MOE_TN), dlmap),
            pl.BlockSpec((None, None, 1, MOE_TN), dhmap),
        ],
        out_specs=pl.BlockSpec((MOE_TM, MOE_TN), ymap),
        scratch_shapes=[pltpu.VMEM((MOE_TM, d), BF16), pltpu.VMEM((MOE_TM, D_FF), BF16)],
    )
    return pl.pallas_call(
        _ffn_kernel,
        grid_spec=grid_spec,
        out_shape=jax.ShapeDtypeStruct((n_slots, dw), jnp.uint32),
        compiler_params=_params(("arbitrary", "arbitrary"), 56),
        name="moe_ffn",
    )(tile_expert, tile_rows, n_used, xs, w_gate_up, w_gate_up, bgu, bgu, w_down, w_down, bd, bd)


def _combine_kernel(pos_ref, posn_ref, ys_ref, gate_ref, h_ref, lng_ref, lnb_ref, hout_ref, hb_ref, buf, sem):
    i = pl.program_id(0)
    tq, d = h_ref.shape
    half = d // 2
    slot = i % 2

    def issue(p_ref, s):
        def body(r, carry):
            for k in range(TOP_K):
                src = p_ref[0, r * TOP_K + k]
                pltpu.make_async_copy(ys_ref.at[pl.ds(src, 1), :], buf.at[s, k, pl.ds(r, 1), :], sem.at[s]).start()
            return carry

        lax.fori_loop(0, tq, body, 0)

    @pl.when(i == 0)
    def _():
        issue(pos_ref, 0)

    @pl.when(i + 1 < pl.num_programs(0))
    def _():
        issue(posn_ref, 1 - slot)

    for k in range(TOP_K):
        pltpu.make_async_copy(ys_ref.at[pl.ds(0, tq), :], buf.at[slot, k], sem.at[slot]).wait()
    gate = gate_ref[...]
    acc_lo = DEEPNORM_ALPHA * h_ref[:, :half]
    acc_hi = DEEPNORM_ALPHA * h_ref[:, half:]
    for k in range(TOP_K):
        lo, hi = _unpack_halves(buf[slot, k])
        acc_lo = acc_lo + gate[:, k:k + 1] * lo
        acc_hi = acc_hi + gate[:, k:k + 1] * hi
    mu = (jnp.sum(acc_lo, axis=-1, keepdims=True) + jnp.sum(acc_hi, axis=-1, keepdims=True)) / d
    acc_lo = acc_lo - mu
    acc_hi = acc_hi - mu
    var = (jnp.sum(acc_lo * acc_lo, axis=-1, keepdims=True) + jnp.sum(acc_hi * acc_hi, axis=-1, keepdims=True)) / d
    inv = lax.rsqrt(var + LN_EPS)
    for sl, acc in ((slice(0, half), acc_lo), (slice(half, d), acc_hi)):
        hn = acc * inv * lng_ref[:, sl] + lnb_ref[:, sl]
        hout_ref[:, sl] = hn
        hb_ref[:, sl] = hn.astype(BF16)


def _combine(ys, pos, gate, h, ln_g, ln_b):
    t, d = h.shape
    tq = min(TOK_TILE, t)
    nt = t // tq
    row = lambda width: pl.BlockSpec((tq, width), lambda i: (i, 0))
    vec = pl.BlockSpec((1, d), lambda i: (0, 0))
    pos3 = pos.reshape(nt, 1, tq * TOP_K)
    return pl.pallas_call(
        _combine_kernel,
        grid=(nt,),
        in_specs=[pl.BlockSpec((None, 1, tq * TOP_K), lambda i: (i, 0, 0), memory_space=pltpu.SMEM),
                  pl.BlockSpec((None, 1, tq * TOP_K), lambda i: (jnp.minimum(i + 1, nt - 1), 0, 0),
                               memory_space=pltpu.SMEM),
                  pl.BlockSpec(memory_space=pl.ANY), row(ROUTE_LANES), row(d), vec, vec],
        out_specs=[row(d), row(d)],
        out_shape=[jax.ShapeDtypeStruct((t, d), F32), jax.ShapeDtypeStruct((t, d), BF16)],
        scratch_shapes=[pltpu.VMEM((2, TOP_K, tq, d // 2), jnp.uint32), pltpu.SemaphoreType.DMA((2,))],
        compiler_params=_params(("arbitrary",), 40),
        name="moe_combine",
    )(pos3, pos3, ys, gate, h, ln_g.reshape(1, d), ln_b.reshape(1, d))


def _hgrn_lower_bound(lb_table, layer):
    p = jax.nn.softmax(lb_table.astype(F32), axis=0)
    return jnp.cumsum(p, axis=0)[layer] - p[0]


def kernel(x, ln_in_g, ln_in_b, w_in, hg_lower_bound, hg_norm_g, hy_conv_w, hy_conv_b, hy_filt_w1, hy_filt_b1,
           hy_filt_w2, hy_filt_b2, hy_filt_freq, hy_filt_w3, hy_skip, att_sink, rel_bias, w_branch, w_out,
           ln_mix_g, ln_mix_b, router_w, router_b, w_gate_up, b_gate_up, w_down, b_down, ln_moe_g, ln_moe_b):
    batch, seq, d = x.shape
    t = batch * seq
    n_tiles = t * TOP_K // MOE_TM + N_EXPERTS
    n_slots = n_tiles * MOE_TM

    fmat, gmat = _dft_tables(seq)
    z_pos, window = _hy_positions(seq)
    bias_tab = _attn_bias_table(rel_bias)

    h, hb = _ln_in(x.reshape(t, d), ln_in_g, ln_in_b)
    for layer in range(DEPTH):
        proj = _inproj(hb, w_in, layer, 0, OFF_AK, PROJ_TN)
        proj_kv = _inproj(hb, w_in, layer, OFF_AK, PROJ_KV_COLS, PROJ_KV_COLS)
        w_gates = lax.slice(w_in, (layer, 0, OFF_GATES), (layer + 1, d, IN_COLS))
        gates = _inproj(hb, w_gates, 0, 0, N_BRANCH * d, PROJ_TN)

        lb = _hgrn_lower_bound(hg_lower_bound, layer)
        o_hg = _hgrn(proj, lb, hg_norm_g[layer], batch, seq)

        hcat = _hy_filters(z_pos, window, hy_filt_w1[layer], hy_filt_b1[layer], hy_filt_w2[layer],
                           hy_filt_b2[layer], hy_filt_freq[layer], hy_filt_w3[layer])
        pr, pi, nyq = _hy_spectrum(fmat, hcat, seq)
        o_hy = _hyena(proj, hy_conv_w[layer], hy_conv_b[layer], hy_skip[layer], fmat, gmat, pr, pi, nyq,
                      batch, seq)

        o_at = _attn(proj, proj_kv, att_sink[layer], bias_tab, batch, seq)

        h_mid, hp, top_idx, gate = _merge(
            o_hg, o_hy, o_at, gates, h, w_branch[layer].astype(BF16), w_out[layer].astype(BF16),
            ln_mix_g[layer], ln_mix_b[layer], router_w[layer], router_b[layer])

        pos, tile_expert, tile_rows, n_used = _route_plan(top_idx[:, :TOP_K], n_tiles)
        xs = _dispatch(hp, pos, n_slots)
        ys = _ffn(xs, w_gate_up, b_gate_up, w_down, b_down, layer, tile_expert, tile_rows, n_used)
        h, hb = _combine(ys, pos, gate, h_mid, ln_moe_g[layer], ln_moe_b[layer])
    return h.reshape(batch, seq, d)
```

```python
import functools
import math

import jax
import jax.numpy as jnp
from jax import lax
from jax.experimental import pallas as pl
from jax.experimental.pallas import tpu as pltpu

F32 = jnp.float32
BF16 = jnp.bfloat16

D_MODEL = 2048
DEPTH = 2
MIX_W = 1024
N_BRANCH = 3
HG_HEADS = 8
HG_DK = 128
HG_DV = 128
HG_CHUNK = 16
F_FLOOR = 1e-30
HY_W = 1024
HY_SHORT = 3
HY_EMB = 33
HY_BANDS = (HY_EMB - 1) // 2
HY_ORDER = 64
HY_INNER = 2
HY_FAST_DECAY = 0.3
HY_SLOW_DECAY = 1.5
HY_TARGET = 1e-2
ATT_HEADS = 16
ATT_KV_HEADS = 2
ATT_DH = 64
WINDOW = 128
ATT_BLOCK = 128
REL_BUCKETS = 32
REL_MAX_DIST = 128
MASK_VALUE = -1e30
N_EXPERTS = 32
TOP_K = 4
D_FF = 2048
SWIGLU_ALPHA = 1.702
SWIGLU_LIMIT = 7.0
LN_EPS = 1e-5
RMS_EPS = 1e-6
DEEPNORM_ALPHA = (2 * DEPTH) ** 0.25

HG_KW = HG_HEADS * HG_DK
HG_VW = HG_HEADS * HG_DV
ATT_QW = ATT_HEADS * ATT_DH
ATT_KVW = ATT_KV_HEADS * ATT_DH
IN_SIZES = (HG_KW, HG_KW, HG_KW, HG_VW, HG_VW, 3 * HY_W, ATT_QW, ATT_KVW, ATT_KVW, N_BRANCH * D_MODEL)
IN_COLS = sum(IN_SIZES)
OFF_HQ, OFF_HFF, OFF_HFB, OFF_HI, OFF_HOG = 0, HG_KW, 2 * HG_KW, 3 * HG_KW, 3 * HG_KW + HG_VW
OFF_HY = OFF_HOG + HG_VW
OFF_AQ = OFF_HY + 3 * HY_W
OFF_AK = OFF_AQ + ATT_QW
OFF_AV = OFF_AK + ATT_KVW
OFF_GATES = OFF_AV + ATT_KVW
MAIN_COLS = OFF_GATES

V7X_LANES = 128
V7X_VMEM_BYTES = 64 * 1024 * 1024

LN_ROWS = 512
PROJ_TM = 1024
PROJ_TN = 1024
PROJ_KV_COLS = 2 * ATT_KVW
HY_TC = 512
MERGE_TM = 512
MOE_TM = 2048
MOE_SUB = 256
MOE_TF = 256
MOE_TN = 256
TOK_TILE = 256
ROUTE_LANES = 128


def _params(semantics, vmem_mb):
    return pltpu.CompilerParams(dimension_semantics=semantics, vmem_limit_bytes=vmem_mb * 1024 * 1024)


def _layer_norm_rows(x, g, b):
    mu = jnp.mean(x, axis=-1, keepdims=True)
    xc = x - mu
    var = jnp.mean(xc * xc, axis=-1, keepdims=True)
    return xc * lax.rsqrt(var + LN_EPS) * g + b


def _pack_halves(x):
    c = x.shape[-1] // 2
    return _pack_pair(x[:, :c], x[:, c:])


def _pack_pair(lo, hi):
    lo = pltpu.bitcast(lo.astype(BF16).astype(F32), jnp.uint32)
    hi = pltpu.bitcast(hi.astype(BF16).astype(F32), jnp.uint32)
    return (lo >> 16) | (hi & jnp.uint32(0xFFFF0000))


def _unpack_halves(w):
    lo = pltpu.bitcast(w << 16, F32)
    hi = pltpu.bitcast(w & jnp.uint32(0xFFFF0000), F32)
    return lo, hi


def _ln_in_kernel(x_ref, g_ref, b_ref, h_ref, hb_ref):
    y = _layer_norm_rows(x_ref[...], g_ref[...], b_ref[...])
    h_ref[...] = y
    hb_ref[...] = y.astype(BF16)


def _ln_in(x2, g, b):
    t, d = x2.shape
    tm = min(LN_ROWS, t)
    row = pl.BlockSpec((tm, d), lambda i: (i, 0))
    vec = pl.BlockSpec((1, d), lambda i: (0, 0))
    return pl.pallas_call(
        _ln_in_kernel,
        grid=(t // tm,),
        in_specs=[row, vec, vec],
        out_specs=[row, row],
        out_shape=[jax.ShapeDtypeStruct((t, d), F32), jax.ShapeDtypeStruct((t, d), BF16)],
        compiler_params=_params(("arbitrary",), 32),
        name="ln_in",
    )(x2, g.reshape(1, d), b.reshape(1, d))


def _inproj_kernel(a_ref, w_ref, o_ref, wb_ref):
    @pl.when(pl.program_id(1) == 0)
    def _():
        wb_ref[...] = w_ref[...].astype(BF16)

    o_ref[...] = jnp.dot(a_ref[...], wb_ref[...], preferred_element_type=F32).astype(o_ref.dtype)


def _inproj(hb, w_in, layer, col0, ncols, tn):
    t, d = hb.shape
    tm = min(PROJ_TM, t)
    cb0 = col0 // tn
    return pl.pallas_call(
        _inproj_kernel,
        grid=(ncols // tn, t // tm),
        in_specs=[
            pl.BlockSpec((tm, d), lambda j, i: (i, 0)),
            pl.BlockSpec((None, d, tn), lambda j, i: (layer, 0, cb0 + j)),
        ],
        out_specs=pl.BlockSpec((tm, tn), lambda j, i: (i, j)),
        out_shape=jax.ShapeDtypeStruct((t, ncols), BF16),
        scratch_shapes=[pltpu.VMEM((d, tn), BF16)],
        compiler_params=_params(("arbitrary", "arbitrary"), 48),
        name="inproj",
    )(hb, w_in)


HG_BLK = 128
HG_CPB = HG_BLK // HG_CHUNK
HG_UNROLL = 16
LOG2E = 1.4426950408889634


def _hgrn_perm():
    r = jnp.arange(HG_BLK)
    src = (r % HG_CPB) * HG_CHUNK + r // HG_CPB
    return (src[:, None] == jnp.arange(HG_BLK)[None, :]).astype(BF16)


def _hgrn_kernel(q_ref, ff_ref, fb_ref, i_ref, og_ref, lbf_ref, lbb_ref, g_ref, p_ref, pt_ref, o_ref,
                 x_t, kk_t, o_t, tmp_t, qtil_c, ktil_c, kv_s, oint_c, dec_c, *, seq):
    c = HG_CHUNK
    nc = seq // c
    nb = seq // HG_BLK
    nbh = max(nb // 2, 1)
    n_half = nb // nbh
    perm = p_ref[...]
    perm_t = pt_ref[...]

    for a, ref in enumerate((q_ref, ff_ref, fb_ref, i_ref)):
        for r in range(nb):
            xp = jnp.dot(perm, ref[r * HG_BLK:(r + 1) * HG_BLK, :], preferred_element_type=F32)
            x_t[a, r] = xp.reshape(c, HG_CPB, HG_DK)

    lbs = (lbf_ref[...], lbb_ref[...])
    for j in range(c):
        qj = x_t[0, :, j]
        x_t[0, :, j] = qj * jax.nn.sigmoid(qj) * (HG_DK ** -0.5)
        for d in range(2):
            z = x_t[1 + d, :, j]
            lb = lbs[d]
            e = jnp.exp(-jnp.abs(z))
            r_ = 1.0 / (1.0 + e)
            er = e * r_
            sig = jnp.where(z >= 0, r_, er)
            nsig = jnp.where(z >= 0, er, r_)
            f = lb + (1.0 - lb) * sig
            x_t[1 + d, :, j] = jnp.log(jnp.maximum(f, F_FLOOR)) * LOG2E
            kk_t[d, :, j] = (1.0 - lb) * nsig

    for d in range(2):
        acc = jnp.zeros((nb, HG_CPB, HG_DK), F32)
        for j in (range(c) if d == 0 else reversed(range(c))):
            acc = acc + x_t[1 + d, :, j]
            x_t[1 + d, :, j] = acc
        last = acc
        dec_c[d] = jnp.exp2(last).reshape(nc, HG_DK)
        for j in range(c):
            cj = x_t[1 + d, :, j]
            tmp_t[0, :, j] = x_t[0, :, j] * jnp.exp2(cj)
            tmp_t[1, :, j] = kk_t[d, :, j] * jnp.exp2(last - cj)
        for r in range(nb):
            rows = slice(r * HG_BLK, (r + 1) * HG_BLK)
            qb = tmp_t[0, r].reshape(HG_BLK, HG_DK).astype(BF16)
            kb = tmp_t[1, r].reshape(HG_BLK, HG_DK).astype(BF16)
            qtil_c[d, rows, :] = jnp.dot(perm_t, qb, preferred_element_type=F32).astype(BF16)
            ktil_c[rows, d * HG_DK:(d + 1) * HG_DK] = jnp.dot(perm_t, kb, preferred_element_type=F32).astype(BF16)

    o_t[...] = jnp.zeros(o_t.shape, F32)
    ones_b = jnp.ones((HG_DK, HG_DV), BF16)

    def pair_body(it, carry):
        d = it // n_half
        sl = pl.ds((it % n_half) * nbh, nbh)
        for t in range(c):
            rt = t + d * (c - 1 - 2 * t)
            ct = x_t[1 + d, sl, rt]
            qt = x_t[0, sl, rt]
            acc = jnp.zeros((nbh * HG_CPB, HG_DV), F32)
            for s in range(t + 1):
                rs = s + d * (c - 1 - 2 * s)
                e = jnp.exp2(jnp.minimum(ct - x_t[1 + d, sl, rs], 0.0))
                a = (qt * e * kk_t[d, sl, rs]).reshape(nbh * HG_CPB, HG_DK).astype(BF16)
                p = jnp.dot(a, ones_b, preferred_element_type=F32)
                acc = acc + p * x_t[3, sl, rs].reshape(nbh * HG_CPB, HG_DV)
            o_t[sl, rt] += acc.reshape(nbh, HG_CPB, HG_DV)
        return carry

    lax.fori_loop(0, 2 * n_half, pair_body, 0)

    def kv_body(n, carry):
        r0 = pl.multiple_of(n * c, c)
        kv = lax.dot_general(i_ref[pl.ds(r0, c), :], ktil_c[pl.ds(r0, c), :], (((0,), (0,)), ((), ())),
                             preferred_element_type=F32)
        kv_s[0, n] = kv[:, :HG_DK]
        kv_s[1, n] = kv[:, HG_DK:]
        return carry

    lax.fori_loop(0, nc, kv_body, 0, unroll=HG_UNROLL)

    def chain_body(idx, carry):
        new = []
        for d in range(2):
            n = idx if d == 0 else nc - 1 - idx
            s = carry[d]
            new.append(dec_c[d, pl.ds(n, 1), :] * s + kv_s[d, n])
            kv_s[d, n] = s
        return tuple(new)

    zero_state = jnp.zeros((HG_DV, HG_DK), F32)
    lax.fori_loop(0, nc, chain_body, (zero_state, zero_state), unroll=2)

    def out_body(n, carry):
        r0 = pl.multiple_of(n * c, c)
        for d in range(2):
            oint_c[d, pl.ds(r0, c), :] = lax.dot_general(
                qtil_c[d, pl.ds(r0, c), :], kv_s[d, n].astype(BF16), (((1,), (1,)), ((), ())),
                preferred_element_type=F32)
        return carry

    lax.fori_loop(0, nc, out_body, 0, unroll=HG_UNROLL)

    for r in range(nb):
        rows = slice(r * HG_BLK, (r + 1) * HG_BLK)
        ob = o_t[r].reshape(HG_BLK, HG_DV)
        hi = ob.astype(BF16)
        lo = (ob - hi.astype(F32)).astype(BF16)
        oc = jnp.dot(perm_t, hi, preferred_element_type=F32) + jnp.dot(perm_t, lo, preferred_element_type=F32)
        o = oc + oint_c[0, rows, :] + oint_c[1, rows, :]
        o = o * lax.rsqrt(jnp.mean(o * o, axis=-1, keepdims=True) + RMS_EPS) * g_ref[...]
        og = og_ref[rows, :].astype(F32)
        o_ref[rows, :] = (o * (og * jax.nn.sigmoid(og))).astype(o_ref.dtype)


def _hgrn(proj, lb, norm_g, batch, seq):
    t = batch * seq
    nc = seq // HG_CHUNK

    def col(off):
        return pl.BlockSpec((seq, HG_DK), lambda b, h: (b, off // HG_DK + h))

    vec = pl.BlockSpec((None, 1, HG_DK), lambda b, h: (h, 0, 0))
    lbf = lb[:HG_KW].reshape(HG_HEADS, 1, HG_DK)
    lbb = lb[HG_KW:].reshape(HG_HEADS, 1, HG_DK)
    g = norm_g.reshape(HG_HEADS, 1, HG_DV)
    slab = (seq // HG_BLK, HG_CHUNK, HG_CPB, HG_DK)
    perm = _hgrn_perm()
    pspec = pl.BlockSpec((HG_BLK, HG_BLK), lambda b, h: (0, 0))
    return pl.pallas_call(
        functools.partial(_hgrn_kernel, seq=seq),
        grid=(batch, HG_HEADS),
        in_specs=[col(OFF_HQ), col(OFF_HFF), col(OFF_HFB), col(OFF_HI), col(OFF_HOG), vec, vec, vec, pspec, pspec],
        out_specs=pl.BlockSpec((seq, HG_DV), lambda b, h: (b, h)),
        out_shape=jax.ShapeDtypeStruct((t, HG_VW), BF16),
        scratch_shapes=[
            pltpu.VMEM((4,) + slab, F32),
            pltpu.VMEM((2,) + slab, F32),
            pltpu.VMEM(slab, F32),
            pltpu.VMEM((2,) + slab, F32),
            pltpu.VMEM((2, seq, HG_DK), BF16),
            pltpu.VMEM((seq, 2 * HG_DK), BF16),
            pltpu.VMEM((2, nc, HG_DV, HG_DK), F32),
            pltpu.VMEM((2, seq, HG_DV), F32),
            pltpu.VMEM((2, nc, HG_DK), F32),
        ],
        compiler_params=_params(("arbitrary", "arbitrary"), 48),
        name="hgrn2",
    )(proj, proj, proj, proj, proj, lbf, lbb, g, perm, perm.T)


def _attn_kernel(sink_ref, q_ref, kp_ref, ko_ref, kn_ref, vp_ref, vo_ref, vn_ref, bias_ref, o_ref, s_ref, p_ref):
    w = ATT_BLOCK
    group = ATT_HEADS // ATT_KV_HEADS
    kband = jnp.concatenate([kp_ref[...], ko_ref[...], kn_ref[...]], axis=0)
    vband = jnp.concatenate([vp_ref[...], vo_ref[...], vn_ref[...]], axis=0)
    ones = jnp.ones((3 * w, ATT_DH), BF16)
    q = q_ref[...] * jnp.asarray(ATT_DH ** -0.5, BF16)
    kgs = [kband[:, g * ATT_DH:(g + 1) * ATT_DH] for g in range(ATT_KV_HEADS)]
    vg1s = [jnp.concatenate([vband[:, g * ATT_DH:(g + 1) * ATT_DH], ones], axis=1) for g in range(ATT_KV_HEADS)]
    for h in range(ATT_HEADS):
        s_ref[h] = lax.dot_general(q[:, h * ATT_DH:(h + 1) * ATT_DH], kgs[h // group], (((1,), (1,)), ((), ())),
                                   preferred_element_type=F32)
    exps = []
    for h in range(ATT_HEADS):
        s = s_ref[h] * LOG2E + bias_ref[h]
        sk = sink_ref[h] * LOG2E
        m = jnp.maximum(jnp.max(s, axis=-1, keepdims=True), sk)
        p_ref[h] = jnp.exp2(s - m).astype(BF16)
        exps.append(jnp.exp2(sk - m))
    outs = []
    for h in range(ATT_HEADS):
        ov = jnp.dot(p_ref[h], vg1s[h // group], preferred_element_type=F32)
        outs.append(ov[:, :ATT_DH] / (ov[:, ATT_DH:ATT_DH + 1] + exps[h]))
    o_ref[...] = jnp.concatenate(outs, axis=1).astype(o_ref.dtype)


def _t5_relative_bucket(rel):
    half = REL_BUCKETS // 2
    max_exact = half // 2
    bucket = (rel > 0).astype(jnp.int32) * half
    n = jnp.abs(rel)
    n_safe = jnp.maximum(n, 1).astype(F32)
    large = max_exact + (jnp.log(n_safe / max_exact) / math.log(REL_MAX_DIST / max_exact)
                         * (half - max_exact)).astype(jnp.int32)
    large = jnp.clip(large, 0, half - 1)
    return bucket + jnp.where(n < max_exact, n, large)


def _attn_bias_table(rel_bias):
    w = ATT_BLOCK
    kofs = jnp.arange(3 * w, dtype=jnp.int32)[None, :] - w
    rel = kofs - jnp.arange(w, dtype=jnp.int32)[:, None]
    onehot = (_t5_relative_bucket(rel)[:, :, None] == jnp.arange(REL_BUCKETS)[None, None, :]).astype(F32)
    bias = jnp.einsum('qkb,bh->hqk', onehot, rel_bias.astype(F32), precision=lax.Precision.HIGHEST)
    band = jnp.abs(rel) <= WINDOW
    tabs = []
    for first, last in ((False, False), (True, False), (False, True), (True, True)):
        ok = band & ((kofs >= 0) | (not first)) & ((kofs < w) | (not last))
        tabs.append(jnp.where(ok[None], bias * LOG2E, MASK_VALUE))
    return jnp.stack(tabs)


def _attn(proj, proj_kv, sink, bias_tab, batch, seq):
    t = batch * seq
    w = ATT_BLOCK
    nb = seq // w
    kcol, vcol = 0, 1

    def kv(col, delta):
        return pl.BlockSpec((w, ATT_KVW), lambda b, n: (b * nb + jnp.clip(n + delta, 0, nb - 1), col))

    def variant(b, n):
        return ((n == 0).astype(jnp.int32) + 2 * (n == nb - 1).astype(jnp.int32), 0, 0, 0)

    return pl.pallas_call(
        _attn_kernel,
        grid=(batch, nb),
        in_specs=[
            pl.BlockSpec(memory_space=pltpu.SMEM),
            pl.BlockSpec((w, ATT_QW), lambda b, n: (b * nb + n, OFF_AQ // ATT_QW)),
            kv(kcol, -1), kv(kcol, 0), kv(kcol, 1),
            kv(vcol, -1), kv(vcol, 0), kv(vcol, 1),
            pl.BlockSpec((None, ATT_HEADS, w, 3 * w), variant),
        ],
        out_specs=pl.BlockSpec((w, ATT_QW), lambda b, n: (b * nb + n, 0)),
        out_shape=jax.ShapeDtypeStruct((t, ATT_QW), BF16),
        scratch_shapes=[pltpu.VMEM((ATT_HEADS, w, 3 * w), F32), pltpu.VMEM((ATT_HEADS, w, 3 * w), BF16)],
        compiler_params=_params(("arbitrary", "arbitrary"), 32),
        name="win_attn",
    )(sink.astype(F32), proj, proj_kv, proj_kv, proj_kv, proj_kv, proj_kv, proj_kv, bias_tab)


DFT_ROWS = 64


def _load_once(src_hbm, dst_vmem, sem, first):
    @pl.when(first)
    def _():
        cp = pltpu.make_async_copy(src_hbm, dst_vmem, sem)
        cp.start()
        cp.wait()


def _dft_tables(seq):
    n = 2 * seq
    k = jnp.arange(seq, dtype=jnp.int32)[:, None]
    s = jnp.arange(seq, dtype=jnp.int32)[None, :]
    theta = 2.0 * math.pi / n
    ang_a = ((jnp.arange(0, seq, DFT_ROWS, dtype=jnp.int32)[:, None] * s) % n).astype(F32) * theta
    ang_b = ((jnp.arange(DFT_ROWS, dtype=jnp.int32)[:, None] * s) % n).astype(F32) * theta
    ca, sa = jnp.cos(ang_a)[:, None, :], jnp.sin(ang_a)[:, None, :]
    cb, sb = jnp.cos(ang_b)[None], jnp.sin(ang_b)[None]
    cm = (ca * cb - sa * sb).reshape(seq, seq)
    sm = -(sa * cb + ca * sb).reshape(seq, seq)
    nyq = jnp.where(s % 2 == 0, 1.0, -1.0).astype(F32)
    sm = jnp.where(k == 0, nyq, sm)
    f = jnp.concatenate([cm, sm], axis=0).astype(BF16)
    return f, f.T


def _hy_positions(seq):
    t = jnp.linspace(0.0, 1.0, seq, dtype=F32)[:, None]
    w = 2.0 * math.pi * jnp.arange(seq, dtype=F32)[:, None] / seq
    f = jnp.linspace(1e-4, HY_BANDS - 1, HY_BANDS, dtype=F32)[None]
    z = jnp.concatenate([t, jnp.cos(f * w), -jnp.sin(f * w)], axis=-1)
    z = jnp.pad(z, ((0, 0), (0, V7X_LANES - HY_EMB)))
    max_decay = math.log(HY_TARGET) / HY_FAST_DECAY
    min_decay = math.log(HY_TARGET) / HY_SLOW_DECAY
    deltas = jnp.linspace(min_decay, max_decay, HY_W, dtype=F32)
    window = jnp.exp(-t * jnp.abs(deltas))
    return z, window


def _hy_filter_kernel(z_ref, w1_ref, b1_ref, w2_ref, b2_ref, fr_ref, w3_ref, win_ref, h_ref):
    dot = functools.partial(jnp.dot, precision=lax.Precision.HIGHEST, preferred_element_type=F32)
    fr = fr_ref[...]
    h = jnp.sin(fr * (dot(z_ref[...], w1_ref[...]) + b1_ref[...]))
    for j in range(HY_INNER):
        h = jnp.sin(fr * (dot(h, w2_ref[j]) + b2_ref[j]))
    h = dot(h, w3_ref[...]) * win_ref[...]
    row = lax.broadcasted_iota(jnp.int32, h.shape, 0)
    backward = pl.program_id(0) >= pl.num_programs(0) // 2
    h_ref[...] = jnp.where((row == 0) & backward, 0.0, h).astype(h_ref.dtype)


def _hy_filters(z, window, w1, b1, w2, b2, freq, w3):
    seq = z.shape[0]
    tn = HY_TC
    per_dir = HY_W // tn
    full = lambda shape: pl.BlockSpec(shape, lambda j: (0,) * len(shape))
    w1p = jnp.pad(w1, ((0, V7X_LANES - HY_EMB), (0, 0)))
    return pl.pallas_call(
        _hy_filter_kernel,
        grid=(2 * per_dir,),
        in_specs=[
            full((seq, V7X_LANES)), full((V7X_LANES, HY_ORDER)), full((1, HY_ORDER)),
            full((HY_INNER, HY_ORDER, HY_ORDER)), full((HY_INNER, 1, HY_ORDER)), full((1, HY_ORDER)),
            pl.BlockSpec((HY_ORDER, tn), lambda j: (0, j)),
            pl.BlockSpec((seq, tn), lambda j: (0, j % per_dir)),
        ],
        out_specs=pl.BlockSpec((seq, tn), lambda j: (0, j)),
        out_shape=jax.ShapeDtypeStruct((seq, 2 * HY_W), BF16),
        compiler_params=_params(("arbitrary",), 32),
        name="hy_filter",
    )(z, w1p, b1.reshape(1, HY_ORDER), w2, b2.reshape(HY_INNER, 1, HY_ORDER), freq.reshape(1, HY_ORDER), w3, window)


def _mm_kernel(a_ref, b_ref, o_ref):
    o_ref[...] = jnp.dot(a_ref[...], b_ref[...], preferred_element_type=F32).astype(o_ref.dtype)


def _mm(a, b, tm, tn, out_dtype):
    m, k = a.shape
    n = b.shape[1]
    return pl.pallas_call(
        _mm_kernel,
        grid=(m // tm, n // tn),
        in_specs=[pl.BlockSpec((tm, k), lambda i, j: (i, 0)), pl.BlockSpec((k, tn), lambda i, j: (0, j))],
        out_specs=pl.BlockSpec((tm, tn), lambda i, j: (i, j)),
        out_shape=jax.ShapeDtypeStruct((m, n), out_dtype),
        compiler_params=_params(("arbitrary", "arbitrary"), 32),
        name="mm",
    )(a, b)


def _hy_spectrum(fmat, hcat, seq):
    spec = _mm(fmat, hcat, min(1024, 2 * seq), 512, F32)
    top, bot = spec[:seq], spec[seq:]
    kr = top[:, :HY_W] + top[:, HY_W:]
    ki = bot[:, :HY_W] - bot[:, HY_W:]
    n = 2 * seq
    first = (jnp.arange(seq) == 0)[:, None]
    pr = jnp.where(first, kr / n, kr * (2.0 / n))
    pi = jnp.where(first, 0.0, ki * (2.0 / n))
    nyq = (bot[:1, :HY_W] + bot[:1, HY_W:]) / n
    return pr, pi, nyq


def _short_conv(x, w_ref, b_ref):
    seq = x.shape[0]
    row = lax.broadcasted_iota(jnp.int32, x.shape, 0)
    prev = jnp.where(row == 0, 0.0, pltpu.roll(x, 1, 0))
    nxt = jnp.where(row == seq - 1, 0.0, pltpu.roll(x, seq - 1, 0))
    return w_ref[0:1, :] * prev + w_ref[1:2, :] * x + w_ref[2:3, :] * nxt + b_ref[...]


def _first_step():
    return (pl.program_id(0) == 0) & (pl.program_id(1) == 0)


def _hy_fwd_kernel(x1_ref, v_ref, w1_ref, b1_ref, wv_ref, bv_ref, f_hbm, pr_ref, pi_ref, nyq_ref, y_ref,
                   f_ref, sem):
    seq = x1_ref.shape[0]
    _load_once(f_hbm, f_ref, sem, _first_step())
    x1 = _short_conv(x1_ref[...].astype(F32), w1_ref, b1_ref)
    v = _short_conv(v_ref[...].astype(F32), wv_ref, bv_ref)
    u = (x1 * v).astype(BF16)
    w = jnp.dot(f_ref[...], u, preferred_element_type=F32)
    a, b = w[:seq], w[seq:]
    pr, pi = pr_ref[...], pi_ref[...]
    row = lax.broadcasted_iota(jnp.int32, pr.shape, 0)
    pd = jnp.where(row == 0, nyq_ref[...], pr)
    y_ref[:seq, :] = (a * pr - b * pi).astype(y_ref.dtype)
    y_ref[seq:, :] = (a * pi + b * pd).astype(y_ref.dtype)


def _hy_inv_kernel(y_ref, x0_ref, x1_ref, v_ref, w0_ref, b0_ref, w1_ref, b1_ref, wv_ref, bv_ref, skip_ref,
                   g_hbm, o_ref, g_ref, sem):
    _load_once(g_hbm, g_ref, sem, _first_step())
    y = jnp.dot(g_ref[...], y_ref[...], preferred_element_type=F32)
    x0 = _short_conv(x0_ref[...].astype(F32), w0_ref, b0_ref)
    x1 = _short_conv(x1_ref[...].astype(F32), w1_ref, b1_ref)
    v = _short_conv(v_ref[...].astype(F32), wv_ref, bv_ref)
    u = x1 * v
    o_ref[...] = (x0 * (y + u * skip_ref[...])).astype(o_ref.dtype)


def _hyena(proj, conv_w, conv_b, skip, fmat, gmat, pr, pi, nyq, batch, seq):
    t = batch * seq
    tc = HY_TC // 2
    nct = HY_W // tc
    conv_b2 = conv_b.reshape(1, 3 * HY_W)

    def xcol(part):
        return pl.BlockSpec((seq, tc), lambda c, b: (b, (OFF_HY + part * HY_W) // tc + c))

    def wcol(part):
        return pl.BlockSpec((HY_SHORT, tc), lambda c, b: (0, part * nct + c))

    def bcol(part):
        return pl.BlockSpec((1, tc), lambda c, b: (0, part * nct + c))

    chan = pl.BlockSpec((seq, tc), lambda c, b: (0, c))
    chan1 = pl.BlockSpec((1, tc), lambda c, b: (0, c))
    whole = pl.BlockSpec(memory_space=pl.ANY)
    dft_scratch = [pltpu.VMEM(fmat.shape, BF16), pltpu.SemaphoreType.DMA(())]
    yspec = pl.BlockSpec((None, 2 * seq, tc), lambda c, b: (b, 0, c))
    yfreq = pl.pallas_call(
        _hy_fwd_kernel,
        grid=(nct, batch),
        in_specs=[xcol(1), xcol(2), wcol(1), bcol(1), wcol(2), bcol(2), whole, chan, chan, chan1],
        out_specs=yspec,
        out_shape=jax.ShapeDtypeStruct((batch, 2 * seq, HY_W), BF16),
        scratch_shapes=dft_scratch,
        compiler_params=_params(("arbitrary", "arbitrary"), 56),
        name="hy_fwd",
    )(proj, proj, conv_w, conv_b2, conv_w, conv_b2, fmat, pr, pi, nyq)
    return pl.pallas_call(
        _hy_inv_kernel,
        grid=(nct, batch),
        in_specs=[yspec, xcol(0), xcol(1), xcol(2), wcol(0), bcol(0), wcol(1), bcol(1), wcol(2), bcol(2), chan1,
                  whole],
        out_specs=pl.BlockSpec((seq, tc), lambda c, b: (b, c)),
        out_shape=jax.ShapeDtypeStruct((t, HY_W), BF16),
        scratch_shapes=[pltpu.VMEM(gmat.shape, BF16), pltpu.SemaphoreType.DMA(())],
        compiler_params=_params(("arbitrary", "arbitrary"), 56),
        name="hy_inv",
    )(yfreq, proj, proj, proj, conv_w, conv_b2, conv_w, conv_b2, conv_w, conv_b2, skip.reshape(1, HY_W), gmat)


def _branch_kernel(ohg_ref, ohy_ref, oat_ref, gates_ref, wb_hbm, m_ref, wb_ref, sem):
    d = D_MODEL
    _load_once(wb_hbm, wb_ref, sem, pl.program_id(0) == 0)
    m = None
    for n, o_ref in enumerate((ohg_ref, ohy_ref, oat_ref)):
        br = jnp.dot(o_ref[...], wb_ref[n], preferred_element_type=F32)
        term = jax.nn.sigmoid(gates_ref[:, n * d:(n + 1) * d].astype(F32)) * br
        m = term if m is None else m + term
    m_ref[...] = m.astype(m_ref.dtype)


def _merge_kernel(m_ref, h_ref, wo_hbm, lng_ref, lnb_ref, rwh_ref, rwl_ref, rb_ref,
                  hmid_ref, hp_ref, idx_ref, gate_ref, wo_ref, sem):
    _load_once(wo_hbm, wo_ref, sem, pl.program_id(0) == 0)
    y = jnp.dot(m_ref[...], wo_ref[...], preferred_element_type=F32)
    hn = _layer_norm_rows(DEEPNORM_ALPHA * h_ref[...] + y, lng_ref[...], lnb_ref[...])
    hmid_ref[...] = hn
    hp_ref[...] = _pack_halves(hn)

    h_hi = hn.astype(BF16)
    h_lo = (hn - h_hi.astype(F32)).astype(BF16)
    logits = (jnp.dot(h_hi, rwh_ref[...], preferred_element_type=F32)
              + jnp.dot(h_lo, rwh_ref[...], preferred_element_type=F32)
              + jnp.dot(h_hi, rwl_ref[...], preferred_element_type=F32)) + rb_ref[...]
    lane = lax.broadcasted_iota(jnp.int32, logits.shape, 1)
    vals, idxs = [], []
    for _ in range(TOP_K):
        mx = jnp.max(logits, axis=-1, keepdims=True)
        ix = jnp.min(jnp.where(logits == mx, lane, ROUTE_LANES), axis=-1, keepdims=True)
        vals.append(mx)
        idxs.append(ix)
        logits = jnp.where(lane == ix, -jnp.inf, logits)
    exps = [jnp.exp(v - vals[0]) for v in vals]
    total = exps[0]
    for e in exps[1:]:
        total = total + e
    gate_out = jnp.zeros(logits.shape, F32)
    idx_out = jnp.zeros(logits.shape, jnp.int32)
    for r in range(TOP_K):
        gate_out = jnp.where(lane == r, exps[r] / total, gate_out)
        idx_out = jnp.where(lane == r, idxs[r], idx_out)
    gate_ref[...] = gate_out
    idx_ref[...] = idx_out


def _merge(o_hg, o_hy, o_at, gates, h, wb, wo, ln_g, ln_b, rw, rb):
    t, d = h.shape
    tm = min(MERGE_TM, t)
    row = lambda width: pl.BlockSpec((tm, width), lambda i: (i, 0))
    const = lambda shape: pl.BlockSpec(shape, lambda i: (0,) * len(shape))
    whole = pl.BlockSpec(memory_space=pl.ANY)
    rwp = jnp.pad(rw, ((0, 0), (0, ROUTE_LANES - N_EXPERTS)))
    rw_hi = rwp.astype(BF16)
    rw_lo = (rwp - rw_hi.astype(F32)).astype(BF16)
    rbp = jnp.pad(rb, (0, ROUTE_LANES - N_EXPERTS), constant_values=MASK_VALUE).reshape(1, ROUTE_LANES)
    m = pl.pallas_call(
        _branch_kernel,
        grid=(t // tm,),
        in_specs=[row(MIX_W), row(MIX_W), row(MIX_W), row(N_BRANCH * d), whole],
        out_specs=row(d),
        out_shape=jax.ShapeDtypeStruct((t, d), BF16),
        scratch_shapes=[pltpu.VMEM((N_BRANCH, MIX_W, d), BF16), pltpu.SemaphoreType.DMA(())],
        compiler_params=_params(("arbitrary",), 56),
        name="branch_merge",
    )(o_hg, o_hy, o_at, gates, wb)
    return pl.pallas_call(
        _merge_kernel,
        grid=(t // tm,),
        in_specs=[row(d), row(d), whole, const((1, d)), const((1, d)),
                  const((d, ROUTE_LANES)), const((d, ROUTE_LANES)), const((1, ROUTE_LANES))],
        out_specs=[row(d), row(d // 2), row(ROUTE_LANES), row(ROUTE_LANES)],
        out_shape=[jax.ShapeDtypeStruct((t, d), F32), jax.ShapeDtypeStruct((t, d // 2), jnp.uint32),
                   jax.ShapeDtypeStruct((t, ROUTE_LANES), jnp.int32), jax.ShapeDtypeStruct((t, ROUTE_LANES), F32)],
        scratch_shapes=[pltpu.VMEM((d, d), BF16), pltpu.SemaphoreType.DMA(())],
        compiler_params=_params(("arbitrary",), 56),
        name="merge",
    )(m, h, wo, ln_g.reshape(1, d), ln_b.reshape(1, d), rw_hi, rw_lo, rbp)


def _route_plan(top_idx, n_tiles):
    e = top_idx.reshape(-1)
    onehot = (e[:, None] == jnp.arange(N_EXPERTS, dtype=jnp.int32)[None, :]).astype(jnp.int32)
    csum = jnp.cumsum(onehot, axis=0)
    rank = jnp.sum(csum * onehot, axis=1) - 1
    counts = csum[-1]
    padded = (counts + MOE_TM - 1) // MOE_TM * MOE_TM
    p_end = jnp.cumsum(padded)
    p_start = p_end - padded
    pos = (p_start[e] + rank).astype(jnp.int32)
    n_used = (p_end[-1] // MOE_TM).astype(jnp.int32)
    tile_start = jnp.arange(n_tiles, dtype=jnp.int32) * MOE_TM
    tile_expert = jnp.minimum(jnp.searchsorted(p_end, tile_start, side='right'), N_EXPERTS - 1).astype(jnp.int32)
    tile_rows = jnp.clip(counts[tile_expert] - (tile_start - p_start[tile_expert]), 0, MOE_TM).astype(jnp.int32)
    return pos, tile_expert, tile_rows, n_used.reshape(1)


def _dispatch_kernel(pos_ref, hp_ref, xs_ref, sem):
    def body(r, carry):
        for k in range(TOP_K):
            dst = pos_ref[0, r * TOP_K + k]
            pltpu.make_async_copy(hp_ref.at[pl.ds(r, 1), :], xs_ref.at[pl.ds(dst, 1), :], sem).start(priority=k % 2)
        return carry

    lax.fori_loop(0, hp_ref.shape[0], body, 0)
    for k in range(TOP_K):
        pltpu.make_async_copy(hp_ref, xs_ref.at[pl.ds(0, hp_ref.shape[0]), :], sem).wait()


def _dispatch(hp, pos, n_slots):
    t, dw = hp.shape
    tq = min(TOK_TILE, t)
    return pl.pallas_call(
        _dispatch_kernel,
        grid=(t // tq,),
        in_specs=[pl.BlockSpec((None, 1, tq * TOP_K), lambda i: (i, 0, 0), memory_space=pltpu.SMEM),
                  pl.BlockSpec((tq, dw), lambda i: (i, 0))],
        out_specs=pl.BlockSpec(memory_space=pl.ANY),
        out_shape=jax.ShapeDtypeStruct((n_slots, dw), jnp.uint32),
        scratch_shapes=[pltpu.SemaphoreType.DMA(())],
        compiler_params=_params(("arbitrary",), 32),
        name="moe_dispatch",
    )(pos.reshape(t // tq, 1, tq * TOP_K), hp)


def _ffn_kernel(te_ref, tr_ref, nu_ref, xs_hbm, wg_ref, wu_ref, bgu_ref, wdl_ref, wdh_ref, bd_ref,
                y_ref, raw_ref, xb_ref, act_ref, sem):
    i = pl.program_id(0)
    j = pl.program_id(1)
    rows = tr_ref[i]
    n_used = nu_ref[0]
    active = i < n_used
    half = raw_ref.shape[1]
    nf = D_FF // MOE_TF
    nsub = (rows + MOE_SUB - 1) // MOE_SUB

    def x_copy(tile):
        return pltpu.make_async_copy(xs_hbm.at[pl.ds(pl.multiple_of(tile * MOE_TM, MOE_TM), MOE_TM), :],
                                     raw_ref, sem)

    @pl.when(active & (j == 0))
    def _():
        @pl.when(i == 0)
        def _():
            x_copy(0).start()

        x_copy(i).wait()
        lo, hi = _unpack_halves(raw_ref[...])
        keep = lax.broadcasted_iota(jnp.int32, lo.shape, 0) < rows
        xb_ref[:, :half] = jnp.where(keep, lo, 0.0).astype(BF16)
        xb_ref[:, half:] = jnp.where(keep, hi, 0.0).astype(BF16)

    @pl.when((j == 1) & (i + 1 < n_used))
    def _():
        x_copy(i + 1).start()

    fcol = pl.multiple_of(jnp.minimum(j, nf - 1) * MOE_TF, MOE_TF)
    ncol = pl.multiple_of(jnp.maximum(j - nf, 0) * MOE_TN, MOE_TN)

    for k in range(1, MOE_TM // MOE_SUB + 1):
        m = k * MOE_SUB

        @pl.when(active & (j < nf) & (nsub == k))
        def _():
            xs = xb_ref[:m, :]
            g = jnp.dot(xs, wg_ref[...].astype(BF16), preferred_element_type=F32) + bgu_ref[:, pl.ds(fcol, MOE_TF)]
            u = (jnp.dot(xs, wu_ref[...].astype(BF16), preferred_element_type=F32)
                 + bgu_ref[:, pl.ds(D_FF + fcol, MOE_TF)])
            g = jnp.minimum(g, SWIGLU_LIMIT)
            u = jnp.clip(u, -SWIGLU_LIMIT, SWIGLU_LIMIT)
            act_ref[:m, pl.ds(fcol, MOE_TF)] = ((u + 1.0) * g * jax.nn.sigmoid(SWIGLU_ALPHA * g)).astype(BF16)

        @pl.when(active & (j >= nf) & (nsub == k))
        def _():
            act = act_ref[:m, :]
            lo = (jnp.dot(act, wdl_ref[...].astype(BF16), preferred_element_type=F32)
                  + bd_ref[:, pl.ds(ncol, MOE_TN)])
            hi = (jnp.dot(act, wdh_ref[...].astype(BF16), preferred_element_type=F32)
                  + bd_ref[:, pl.ds(half + ncol, MOE_TN)])
            y_ref[:m, :] = _pack_pair(lo, hi)
            if m < MOE_TM:
                y_ref[m:, :] = jnp.zeros((MOE_TM - m, y_ref.shape[1]), jnp.uint32)


def _ffn(xs, w_gate_up, b_gate_up, w_down, b_down, layer, tile_expert, tile_rows, n_used):
    n_slots, dw = xs.shape
    d = 2 * dw
    n_tiles = n_slots // MOE_TM
    nf = D_FF // MOE_TF
    nn = dw // MOE_TN
    bgu = b_gate_up.reshape(DEPTH, N_EXPERTS, 1, 2 * D_FF)
    bd = b_down.reshape(DEPTH, N_EXPERTS, 1, d)

    def tile(i, nu):
        return jnp.minimum(i, nu[0] - 1)

    def fcol(i, j, nu):
        return jnp.where(i < nu[0], jnp.minimum(j, nf - 1), nf - 1)

    def ncol(i, j, nu):
        return jnp.where(i < nu[0], jnp.maximum(j - nf, 0), nn - 1)

    gmap = lambda i, j, te, tr, nu: (layer, te[tile(i, nu)], 0, fcol(i, j, nu))
    umap = lambda i, j, te, tr, nu: (layer, te[tile(i, nu)], 0, nf + fcol(i, j, nu))
    dlmap = lambda i, j, te, tr, nu: (layer, te[tile(i, nu)], 0, ncol(i, j, nu))
    dhmap = lambda i, j, te, tr, nu: (layer, te[tile(i, nu)], 0, nn + ncol(i, j, nu))
    bmap = lambda i, j, te, tr, nu: (layer, te[tile(i, nu)], 0, 0)
    ymap = lambda i, j, te, tr, nu: (tile(i, nu), ncol(i, j, nu))
    grid_spec = pltpu.PrefetchScalarGridSpec(
        num_scalar_prefetch=3,
        grid=(n_tiles, nf + nn),
        in_specs=[
            pl.BlockSpec(memory_space=pl.ANY),
            pl.BlockSpec((None, None, d, MOE_TF), gmap),
            pl.BlockSpec((None, None, d, MOE_TF), umap),
            pl.BlockSpec((None, None, 1, 2 * D_FF), bmap),
            pl.BlockSpec((None, None, D_FF, MOE_TN), dlmap),
            pl.BlockSpec((None, None, D_FF, MOE_TN), dhmap),
            pl.BlockSpec((None, None, 1, d), bmap),
        ],
        out_specs=pl.BlockSpec((MOE_TM, MOE_TN), ymap),
        scratch_shapes=[pltpu.VMEM((MOE_TM, dw), jnp.uint32), pltpu.VMEM((MOE_TM, d), BF16),
                        pltpu.VMEM((MOE_TM, D_FF), BF16), pltpu.SemaphoreType.DMA(())],
    )
    return pl.pallas_call(
        _ffn_kernel,
        grid_spec=grid_spec,
        out_shape=jax.ShapeDtypeStruct((n_slots, dw), jnp.uint32),
        compiler_params=_params(("arbitrary", "arbitrary"), 56),
        name="moe_ffn",
    )(tile_expert, tile_rows, n_used, xs, w_gate_up, w_gate_up, bgu, w_down, w_down, bd)


def _combine_kernel(pos_ref, posn_ref, ys_ref, gate_ref, h_ref, lng_ref, lnb_ref, hout_ref, hb_ref, buf, sem):
    i = pl.program_id(0)
    tq, d = h_ref.shape
    half = d // 2
    slot = i % 2

    def issue(p_ref, s):
        def body(r, carry):
            for k in range(TOP_K):
                src = p_ref[0, r * TOP_K + k]
                pltpu.make_async_copy(ys_ref.at[pl.ds(src, 1), :], buf.at[s, k, pl.ds(r, 1), :],
                                      sem.at[s]).start(priority=k % 2)
            return carry

        lax.fori_loop(0, tq, body, 0)

    @pl.when(i == 0)
    def _():
        issue(pos_ref, 0)

    @pl.when(i + 1 < pl.num_programs(0))
    def _():
        issue(posn_ref, 1 - slot)

    for k in range(TOP_K):
        pltpu.make_async_copy(ys_ref.at[pl.ds(0, tq), :], buf.at[slot, k], sem.at[slot]).wait()
    gate = gate_ref[...]
    acc_lo = DEEPNORM_ALPHA * h_ref[:, :half]
    acc_hi = DEEPNORM_ALPHA * h_ref[:, half:]
    for k in range(TOP_K):
        lo, hi = _unpack_halves(buf[slot, k])
        acc_lo = acc_lo + gate[:, k:k + 1] * lo
        acc_hi = acc_hi + gate[:, k:k + 1] * hi
    mu = (jnp.sum(acc_lo, axis=-1, keepdims=True) + jnp.sum(acc_hi, axis=-1, keepdims=True)) / d
    acc_lo = acc_lo - mu
    acc_hi = acc_hi - mu
    var = (jnp.sum(acc_lo * acc_lo, axis=-1, keepdims=True) + jnp.sum(acc_hi * acc_hi, axis=-1, keepdims=True)) / d
    inv = lax.rsqrt(var + LN_EPS)
    for sl, acc in ((slice(0, half), acc_lo), (slice(half, d), acc_hi)):
        hn = acc * inv * lng_ref[:, sl] + lnb_ref[:, sl]
        hout_ref[:, sl] = hn
        hb_ref[:, sl] = hn.astype(BF16)


def _combine(ys, pos, gate, h, ln_g, ln_b):
    t, d = h.shape
    tq = min(TOK_TILE, t)
    nt = t // tq
    row = lambda width: pl.BlockSpec((tq, width), lambda i: (i, 0))
    vec = pl.BlockSpec((1, d), lambda i: (0, 0))
    pos3 = pos.reshape(nt, 1, tq * TOP_K)
    return pl.pallas_call(
        _combine_kernel,
        grid=(nt,),
        in_specs=[pl.BlockSpec((None, 1, tq * TOP_K), lambda i: (i, 0, 0), memory_space=pltpu.SMEM),
                  pl.BlockSpec((None, 1, tq * TOP_K), lambda i: (jnp.minimum(i + 1, nt - 1), 0, 0),
                               memory_space=pltpu.SMEM),
                  pl.BlockSpec(memory_space=pl.ANY), row(ROUTE_LANES), row(d), vec, vec],
        out_specs=[row(d), row(d)],
        out_shape=[jax.ShapeDtypeStruct((t, d), F32), jax.ShapeDtypeStruct((t, d), BF16)],
        scratch_shapes=[pltpu.VMEM((2, TOP_K, tq, d // 2), jnp.uint32), pltpu.SemaphoreType.DMA((2,))],
        compiler_params=_params(("arbitrary",), 40),
        name="moe_combine",
    )(pos3, pos3, ys, gate, h, ln_g.reshape(1, d), ln_b.reshape(1, d))


def _hgrn_lower_bound(lb_table, layer):
    p = jax.nn.softmax(lb_table.astype(F32), axis=0)
    return jnp.cumsum(p, axis=0)[layer] - p[0]


def kernel(x, ln_in_g, ln_in_b, w_in, hg_lower_bound, hg_norm_g, hy_conv_w, hy_conv_b, hy_filt_w1, hy_filt_b1,
           hy_filt_w2, hy_filt_b2, hy_filt_freq, hy_filt_w3, hy_skip, att_sink, rel_bias, w_branch, w_out,
           ln_mix_g, ln_mix_b, router_w, router_b, w_gate_up, b_gate_up, w_down, b_down, ln_moe_g, ln_moe_b):
    batch, seq, d = x.shape
    t = batch * seq
    n_tiles = t * TOP_K // MOE_TM + N_EXPERTS
    n_slots = n_tiles * MOE_TM

    fmat, gmat = _dft_tables(seq)
    z_pos, window = _hy_positions(seq)
    bias_tab = _attn_bias_table(rel_bias)

    h, hb = _ln_in(x.reshape(t, d), ln_in_g, ln_in_b)
    for layer in range(DEPTH):
        proj = _inproj(hb, w_in, layer, 0, OFF_AK, PROJ_TN)
        proj_kv = _inproj(hb, w_in, layer, OFF_AK, PROJ_KV_COLS, PROJ_KV_COLS)
        w_gates = lax.slice(w_in, (layer, 0, OFF_GATES), (layer + 1, d, IN_COLS))
        gates = _inproj(hb, w_gates, 0, 0, N_BRANCH * d, PROJ_TN)

        lb = _hgrn_lower_bound(hg_lower_bound, layer)
        o_hg = _hgrn(proj, lb, hg_norm_g[layer], batch, seq)

        hcat = _hy_filters(z_pos, window, hy_filt_w1[layer], hy_filt_b1[layer], hy_filt_w2[layer],
                           hy_filt_b2[layer], hy_filt_freq[layer], hy_filt_w3[layer])
        pr, pi, nyq = _hy_spectrum(fmat, hcat, seq)
        o_hy = _hyena(proj, hy_conv_w[layer], hy_conv_b[layer], hy_skip[layer], fmat, gmat, pr, pi, nyq,
                      batch, seq)

        o_at = _attn(proj, proj_kv, att_sink[layer], bias_tab, batch, seq)

        h_mid, hp, top_idx, gate = _merge(
            o_hg, o_hy, o_at, gates, h, w_branch[layer].astype(BF16), w_out[layer].astype(BF16),
            ln_mix_g[layer], ln_mix_b[layer], router_w[layer], router_b[layer])

        pos, tile_expert, tile_rows, n_used = _route_plan(top_idx[:, :TOP_K], n_tiles)
        xs = _dispatch(hp, pos, n_slots)
        ys = _ffn(xs, w_gate_up, b_gate_up, w_down, b_down, layer, tile_expert, tile_rows, n_used)
        h, hb = _combine(ys, pos, gate, h_mid, ln_moe_g[layer], ln_moe_b[layer])
    return h.reshape(batch, seq, d)
```

```python
import functools
import math

import jax
import jax.numpy as jnp
from jax import lax
from jax.experimental import pallas as pl
from jax.experimental.pallas import tpu as pltpu

F32 = jnp.float32
BF16 = jnp.bfloat16

D_MODEL = 2048
DEPTH = 2
MIX_W = 1024
N_BRANCH = 3
HG_HEADS = 8
HG_DK = 128
HG_DV = 128
HG_CHUNK = 16
F_FLOOR = 1e-30
HY_W = 1024
HY_SHORT = 3
HY_EMB = 33
HY_BANDS = (HY_EMB - 1) // 2
HY_ORDER = 64
HY_INNER = 2
HY_FAST_DECAY = 0.3
HY_SLOW_DECAY = 1.5
HY_TARGET = 1e-2
ATT_HEADS = 16
ATT_KV_HEADS = 2
ATT_DH = 64
WINDOW = 128
ATT_BLOCK = 128
REL_BUCKETS = 32
REL_MAX_DIST = 128
MASK_VALUE = -1e30
N_EXPERTS = 32
TOP_K = 4
D_FF = 2048
SWIGLU_ALPHA = 1.702
SWIGLU_LIMIT = 7.0
LN_EPS = 1e-5
RMS_EPS = 1e-6
DEEPNORM_ALPHA = (2 * DEPTH) ** 0.25

HG_KW = HG_HEADS * HG_DK
HG_VW = HG_HEADS * HG_DV
ATT_QW = ATT_HEADS * ATT_DH
ATT_KVW = ATT_KV_HEADS * ATT_DH
IN_SIZES = (HG_KW, HG_KW, HG_KW, HG_VW, HG_VW, 3 * HY_W, ATT_QW, ATT_KVW, ATT_KVW, N_BRANCH * D_MODEL)
IN_COLS = sum(IN_SIZES)
OFF_HQ, OFF_HFF, OFF_HFB, OFF_HI, OFF_HOG = 0, HG_KW, 2 * HG_KW, 3 * HG_KW, 3 * HG_KW + HG_VW
OFF_HY = OFF_HOG + HG_VW
OFF_AQ = OFF_HY + 3 * HY_W
OFF_AK = OFF_AQ + ATT_QW
OFF_AV = OFF_AK + ATT_KVW
OFF_GATES = OFF_AV + ATT_KVW
MAIN_COLS = OFF_GATES

V7X_LANES = 128
V7X_VMEM_BYTES = 64 * 1024 * 1024

LN_ROWS = 512
PROJ_TM = 1024
PROJ_TN = 1024
PROJ_KV_COLS = 2 * ATT_KVW
HY_TC = 512
MERGE_TM = 512
MOE_TM = 2048
MOE_SUB = 512
MOE_TF = 256
MOE_TN = 256
TOK_TILE = 256
ROUTE_LANES = 128


def _params(semantics, vmem_mb):
    return pltpu.CompilerParams(dimension_semantics=semantics, vmem_limit_bytes=vmem_mb * 1024 * 1024)


def _layer_norm_rows(x, g, b):
    mu = jnp.mean(x, axis=-1, keepdims=True)
    xc = x - mu
    var = jnp.mean(xc * xc, axis=-1, keepdims=True)
    return xc * lax.rsqrt(var + LN_EPS) * g + b


def _pack_halves(x):
    c = x.shape[-1] // 2
    return _pack_pair(x[:, :c], x[:, c:])


def _pack_pair(lo, hi):
    lo = pltpu.bitcast(lo.astype(BF16).astype(F32), jnp.uint32)
    hi = pltpu.bitcast(hi.astype(BF16).astype(F32), jnp.uint32)
    return (lo >> 16) | (hi & jnp.uint32(0xFFFF0000))


def _unpack_halves(w):
    lo = pltpu.bitcast(w << 16, F32)
    hi = pltpu.bitcast(w & jnp.uint32(0xFFFF0000), F32)
    return lo, hi


def _ln_in_kernel(x_ref, g_ref, b_ref, h_ref, hb_ref):
    y = _layer_norm_rows(x_ref[...], g_ref[...], b_ref[...])
    h_ref[...] = y
    hb_ref[...] = y.astype(BF16)


def _ln_in(x2, g, b):
    t, d = x2.shape
    tm = min(LN_ROWS, t)
    row = pl.BlockSpec((tm, d), lambda i: (i, 0))
    vec = pl.BlockSpec((1, d), lambda i: (0, 0))
    return pl.pallas_call(
        _ln_in_kernel,
        grid=(t // tm,),
        in_specs=[row, vec, vec],
        out_specs=[row, row],
        out_shape=[jax.ShapeDtypeStruct((t, d), F32), jax.ShapeDtypeStruct((t, d), BF16)],
        compiler_params=_params(("arbitrary",), 32),
        name="ln_in",
    )(x2, g.reshape(1, d), b.reshape(1, d))


def _inproj_kernel(a_ref, w_ref, o_ref, wb_ref):
    @pl.when(pl.program_id(1) == 0)
    def _():
        wb_ref[...] = w_ref[...].astype(BF16)

    o_ref[...] = jnp.dot(a_ref[...], wb_ref[...], preferred_element_type=F32).astype(o_ref.dtype)


def _inproj(hb, w_in, layer, col0, ncols, tn):
    t, d = hb.shape
    tm = min(PROJ_TM, t)
    cb0 = col0 // tn
    return pl.pallas_call(
        _inproj_kernel,
        grid=(ncols // tn, t // tm),
        in_specs=[
            pl.BlockSpec((tm, d), lambda j, i: (i, 0)),
            pl.BlockSpec((None, d, tn), lambda j, i: (layer, 0, cb0 + j)),
        ],
        out_specs=pl.BlockSpec((tm, tn), lambda j, i: (i, j)),
        out_shape=jax.ShapeDtypeStruct((t, ncols), BF16),
        scratch_shapes=[pltpu.VMEM((d, tn), BF16)],
        compiler_params=_params(("arbitrary", "arbitrary"), 48),
        name="inproj",
    )(hb, w_in)


HG_BLK = 128
HG_CPB = HG_BLK // HG_CHUNK
HG_UNROLL = 16
LOG2E = 1.4426950408889634


def _hgrn_perm():
    r = jnp.arange(HG_BLK)
    src = (r % HG_CPB) * HG_CHUNK + r // HG_CPB
    return (src[:, None] == jnp.arange(HG_BLK)[None, :]).astype(BF16)


def _hgrn_kernel(q_ref, ff_ref, fb_ref, i_ref, og_ref, lbf_ref, lbb_ref, g_ref, p_ref, pt_ref, o_ref,
                 x_t, kk_t, o_t, tmp_t, qtil_c, ktil_c, kv_s, oint_c, dec_c, *, seq):
    c = HG_CHUNK
    nc = seq // c
    nb = seq // HG_BLK
    nbh = max(nb // 2, 1)
    n_half = nb // nbh
    perm = p_ref[...]
    perm_t = pt_ref[...]

    for a, ref in enumerate((q_ref, ff_ref, fb_ref, i_ref)):
        for r in range(nb):
            xp = jnp.dot(perm, ref[r * HG_BLK:(r + 1) * HG_BLK, :], preferred_element_type=F32)
            x_t[a, r] = xp.reshape(c, HG_CPB, HG_DK)

    lbs = (lbf_ref[...], lbb_ref[...])
    for j in range(c):
        qj = x_t[0, :, j]
        x_t[0, :, j] = qj * jax.nn.sigmoid(qj) * (HG_DK ** -0.5)
        for d in range(2):
            z = x_t[1 + d, :, j]
            lb = lbs[d]
            e = jnp.exp(-jnp.abs(z))
            r_ = 1.0 / (1.0 + e)
            er = e * r_
            sig = jnp.where(z >= 0, r_, er)
            nsig = jnp.where(z >= 0, er, r_)
            f = lb + (1.0 - lb) * sig
            x_t[1 + d, :, j] = jnp.log(jnp.maximum(f, F_FLOOR)) * LOG2E
            kk_t[d, :, j] = (1.0 - lb) * nsig

    for d in range(2):
        acc = jnp.zeros((nb, HG_CPB, HG_DK), F32)
        for j in (range(c) if d == 0 else reversed(range(c))):
            acc = acc + x_t[1 + d, :, j]
            x_t[1 + d, :, j] = acc
        last = acc
        dec_c[d] = jnp.exp2(last).reshape(nc, HG_DK)
        for j in range(c):
            cj = x_t[1 + d, :, j]
            tmp_t[0, :, j] = x_t[0, :, j] * jnp.exp2(cj)
            tmp_t[1, :, j] = kk_t[d, :, j] * jnp.exp2(last - cj)
        for r in range(nb):
            rows = slice(r * HG_BLK, (r + 1) * HG_BLK)
            qb = tmp_t[0, r].reshape(HG_BLK, HG_DK).astype(BF16)
            kb = tmp_t[1, r].reshape(HG_BLK, HG_DK).astype(BF16)
            qtil_c[d, rows, :] = jnp.dot(perm_t, qb, preferred_element_type=F32).astype(BF16)
            ktil_c[rows, d * HG_DK:(d + 1) * HG_DK] = jnp.dot(perm_t, kb, preferred_element_type=F32).astype(BF16)

    o_t[...] = jnp.zeros(o_t.shape, F32)
    ones_b = jnp.ones((HG_DK, HG_DV), BF16)

    def pair_body(it, carry):
        d = it // n_half
        sl = pl.ds((it % n_half) * nbh, nbh)
        for t in range(c):
            rt = t + d * (c - 1 - 2 * t)
            ct = x_t[1 + d, sl, rt]
            qt = x_t[0, sl, rt]
            acc = jnp.zeros((nbh * HG_CPB, HG_DV), F32)
            for s in range(t + 1):
                rs = s + d * (c - 1 - 2 * s)
                e = jnp.exp2(jnp.minimum(ct - x_t[1 + d, sl, rs], 0.0))
                a = (qt * e * kk_t[d, sl, rs]).reshape(nbh * HG_CPB, HG_DK).astype(BF16)
                p = jnp.dot(a, ones_b, preferred_element_type=F32)
                acc = acc + p * x_t[3, sl, rs].reshape(nbh * HG_CPB, HG_DV)
            o_t[sl, rt] += acc.reshape(nbh, HG_CPB, HG_DV)
        return carry

    lax.fori_loop(0, 2 * n_half, pair_body, 0)

    def kv_body(n, carry):
        r0 = pl.multiple_of(n * c, c)
        kv = lax.dot_general(i_ref[pl.ds(r0, c), :], ktil_c[pl.ds(r0, c), :], (((0,), (0,)), ((), ())),
                             preferred_element_type=F32)
        kv_s[0, n] = kv[:, :HG_DK]
        kv_s[1, n] = kv[:, HG_DK:]
        return carry

    lax.fori_loop(0, nc, kv_body, 0, unroll=HG_UNROLL)

    def chain_body(idx, carry):
        new = []
        for d in range(2):
            n = idx if d == 0 else nc - 1 - idx
            s = carry[d]
            new.append(dec_c[d, pl.ds(n, 1), :] * s + kv_s[d, n])
            kv_s[d, n] = s
        return tuple(new)

    zero_state = jnp.zeros((HG_DV, HG_DK), F32)
    lax.fori_loop(0, nc, chain_body, (zero_state, zero_state), unroll=2)

    def out_body(n, carry):
        r0 = pl.multiple_of(n * c, c)
        lhs = jnp.concatenate([qtil_c[0, pl.ds(r0, c), :], qtil_c[1, pl.ds(r0, c), :]], axis=0)
        rhs = jnp.concatenate([kv_s[0, n], kv_s[1, n]], axis=0).astype(BF16)
        out = lax.dot_general(lhs, rhs, (((1,), (1,)), ((), ())), preferred_element_type=F32)
        oint_c[0, pl.ds(r0, c), :] = out[:c, :HG_DV]
        oint_c[1, pl.ds(r0, c), :] = out[c:, HG_DV:]
        return carry

    lax.fori_loop(0, nc, out_body, 0, unroll=HG_UNROLL)

    for r in range(nb):
        rows = slice(r * HG_BLK, (r + 1) * HG_BLK)
        ob = o_t[r].reshape(HG_BLK, HG_DV)
        hi = ob.astype(BF16)
        lo = (ob - hi.astype(F32)).astype(BF16)
        oc = jnp.dot(perm_t, hi, preferred_element_type=F32) + jnp.dot(perm_t, lo, preferred_element_type=F32)
        o = oc + oint_c[0, rows, :] + oint_c[1, rows, :]
        o = o * lax.rsqrt(jnp.mean(o * o, axis=-1, keepdims=True) + RMS_EPS) * g_ref[...]
        og = og_ref[rows, :].astype(F32)
        o_ref[rows, :] = (o * (og * jax.nn.sigmoid(og))).astype(o_ref.dtype)


def _hgrn(proj, lb, norm_g, batch, seq):
    t = batch * seq
    nc = seq // HG_CHUNK

    def col(off):
        return pl.BlockSpec((seq, HG_DK), lambda b, h: (b, off // HG_DK + h))

    vec = pl.BlockSpec((None, 1, HG_DK), lambda b, h: (h, 0, 0))
    lbf = lb[:HG_KW].reshape(HG_HEADS, 1, HG_DK)
    lbb = lb[HG_KW:].reshape(HG_HEADS, 1, HG_DK)
    g = norm_g.reshape(HG_HEADS, 1, HG_DV)
    slab = (seq // HG_BLK, HG_CHUNK, HG_CPB, HG_DK)
    perm = _hgrn_perm()
    pspec = pl.BlockSpec((HG_BLK, HG_BLK), lambda b, h: (0, 0))
    return pl.pallas_call(
        functools.partial(_hgrn_kernel, seq=seq),
        grid=(batch, HG_HEADS),
        in_specs=[col(OFF_HQ), col(OFF_HFF), col(OFF_HFB), col(OFF_HI), col(OFF_HOG), vec, vec, vec, pspec, pspec],
        out_specs=pl.BlockSpec((seq, HG_DV), lambda b, h: (b, h)),
        out_shape=jax.ShapeDtypeStruct((t, HG_VW), BF16),
        scratch_shapes=[
            pltpu.VMEM((4,) + slab, F32),
            pltpu.VMEM((2,) + slab, F32),
            pltpu.VMEM(slab, F32),
            pltpu.VMEM((2,) + slab, F32),
            pltpu.VMEM((2, seq, HG_DK), BF16),
            pltpu.VMEM((seq, 2 * HG_DK), BF16),
            pltpu.VMEM((2, nc, HG_DV, HG_DK), F32),
            pltpu.VMEM((2, seq, HG_DV), F32),
            pltpu.VMEM((2, nc, HG_DK), F32),
        ],
        compiler_params=_params(("arbitrary", "arbitrary"), 48),
        name="hgrn2",
    )(proj, proj, proj, proj, proj, lbf, lbb, g, perm, perm.T)


def _attn_kernel(sink_ref, q_ref, kp_ref, ko_ref, kn_ref, vp_ref, vo_ref, vn_ref, bias_ref, o_ref, s_ref, p_ref):
    w = ATT_BLOCK
    group = ATT_HEADS // ATT_KV_HEADS
    kband = jnp.concatenate([kp_ref[...], ko_ref[...], kn_ref[...]], axis=0)
    vband = jnp.concatenate([vp_ref[...], vo_ref[...], vn_ref[...]], axis=0)
    ones = jnp.ones((3 * w, ATT_DH), BF16)
    q = q_ref[...] * jnp.asarray(ATT_DH ** -0.5, BF16)
    kgs = [kband[:, g * ATT_DH:(g + 1) * ATT_DH] for g in range(ATT_KV_HEADS)]
    vg1s = [jnp.concatenate([vband[:, g * ATT_DH:(g + 1) * ATT_DH], ones], axis=1) for g in range(ATT_KV_HEADS)]
    for h in range(ATT_HEADS):
        s_ref[h] = lax.dot_general(q[:, h * ATT_DH:(h + 1) * ATT_DH], kgs[h // group], (((1,), (1,)), ((), ())),
                                   preferred_element_type=F32)
    exps = []
    for h in range(ATT_HEADS):
        s = s_ref[h] * LOG2E + bias_ref[h]
        sk = sink_ref[h] * LOG2E
        m = jnp.maximum(jnp.max(s, axis=-1, keepdims=True), sk)
        p_ref[h] = jnp.exp2(s - m).astype(BF16)
        exps.append(jnp.exp2(sk - m))
    outs = []
    for h in range(ATT_HEADS):
        ov = jnp.dot(p_ref[h], vg1s[h // group], preferred_element_type=F32)
        outs.append(ov[:, :ATT_DH] / (ov[:, ATT_DH:ATT_DH + 1] + exps[h]))
    o_ref[...] = jnp.concatenate(outs, axis=1).astype(o_ref.dtype)


def _t5_relative_bucket(rel):
    half = REL_BUCKETS // 2
    max_exact = half // 2
    bucket = (rel > 0).astype(jnp.int32) * half
    n = jnp.abs(rel)
    n_safe = jnp.maximum(n, 1).astype(F32)
    large = max_exact + (jnp.log(n_safe / max_exact) / math.log(REL_MAX_DIST / max_exact)
                         * (half - max_exact)).astype(jnp.int32)
    large = jnp.clip(large, 0, half - 1)
    return bucket + jnp.where(n < max_exact, n, large)


def _attn_bias_table(rel_bias):
    w = ATT_BLOCK
    kofs = jnp.arange(3 * w, dtype=jnp.int32)[None, :] - w
    rel = kofs - jnp.arange(w, dtype=jnp.int32)[:, None]
    onehot = (_t5_relative_bucket(rel)[:, :, None] == jnp.arange(REL_BUCKETS)[None, None, :]).astype(F32)
    bias = jnp.einsum('qkb,bh->hqk', onehot, rel_bias.astype(F32), precision=lax.Precision.HIGHEST)
    band = jnp.abs(rel) <= WINDOW
    tabs = []
    for first, last in ((False, False), (True, False), (False, True), (True, True)):
        ok = band & ((kofs >= 0) | (not first)) & ((kofs < w) | (not last))
        tabs.append(jnp.where(ok[None], bias * LOG2E, MASK_VALUE))
    return jnp.stack(tabs)


def _attn(proj, proj_kv, sink, bias_tab, batch, seq):
    t = batch * seq
    w = ATT_BLOCK
    nb = seq // w
    kcol, vcol = 0, 1

    def kv(col, delta):
        return pl.BlockSpec((w, ATT_KVW), lambda b, n: (b * nb + jnp.clip(n + delta, 0, nb - 1), col))

    def variant(b, n):
        return ((n == 0).astype(jnp.int32) + 2 * (n == nb - 1).astype(jnp.int32), 0, 0, 0)

    return pl.pallas_call(
        _attn_kernel,
        grid=(batch, nb),
        in_specs=[
            pl.BlockSpec(memory_space=pltpu.SMEM),
            pl.BlockSpec((w, ATT_QW), lambda b, n: (b * nb + n, OFF_AQ // ATT_QW)),
            kv(kcol, -1), kv(kcol, 0), kv(kcol, 1),
            kv(vcol, -1), kv(vcol, 0), kv(vcol, 1),
            pl.BlockSpec((None, ATT_HEADS, w, 3 * w), variant),
        ],
        out_specs=pl.BlockSpec((w, ATT_QW), lambda b, n: (b * nb + n, 0)),
        out_shape=jax.ShapeDtypeStruct((t, ATT_QW), BF16),
        scratch_shapes=[pltpu.VMEM((ATT_HEADS, w, 3 * w), F32), pltpu.VMEM((ATT_HEADS, w, 3 * w), BF16)],
        compiler_params=_params(("arbitrary", "arbitrary"), 32),
        name="win_attn",
    )(sink.astype(F32), proj, proj_kv, proj_kv, proj_kv, proj_kv, proj_kv, proj_kv, bias_tab)


DFT_ROWS = 64


def _load_once(src_hbm, dst_vmem, sem, first):
    @pl.when(first)
    def _():
        cp = pltpu.make_async_copy(src_hbm, dst_vmem, sem)
        cp.start()
        cp.wait()


def _dft_tables(seq):
    n = 2 * seq
    k = jnp.arange(seq, dtype=jnp.int32)[:, None]
    s = jnp.arange(seq, dtype=jnp.int32)[None, :]
    theta = 2.0 * math.pi / n
    ang_a = ((jnp.arange(0, seq, DFT_ROWS, dtype=jnp.int32)[:, None] * s) % n).astype(F32) * theta
    ang_b = ((jnp.arange(DFT_ROWS, dtype=jnp.int32)[:, None] * s) % n).astype(F32) * theta
    ca, sa = jnp.cos(ang_a)[:, None, :], jnp.sin(ang_a)[:, None, :]
    cb, sb = jnp.cos(ang_b)[None], jnp.sin(ang_b)[None]
    cm = (ca * cb - sa * sb).reshape(seq, seq)
    sm = -(sa * cb + ca * sb).reshape(seq, seq)
    nyq = jnp.where(s % 2 == 0, 1.0, -1.0).astype(F32)
    sm = jnp.where(k == 0, nyq, sm)
    f = jnp.concatenate([cm, sm], axis=0).astype(BF16)
    return f, f.T


def _hy_positions(seq):
    t = jnp.linspace(0.0, 1.0, seq, dtype=F32)[:, None]
    w = 2.0 * math.pi * jnp.arange(seq, dtype=F32)[:, None] / seq
    f = jnp.linspace(1e-4, HY_BANDS - 1, HY_BANDS, dtype=F32)[None]
    z = jnp.concatenate([t, jnp.cos(f * w), -jnp.sin(f * w)], axis=-1)
    z = jnp.pad(z, ((0, 0), (0, V7X_LANES - HY_EMB)))
    max_decay = math.log(HY_TARGET) / HY_FAST_DECAY
    min_decay = math.log(HY_TARGET) / HY_SLOW_DECAY
    deltas = jnp.linspace(min_decay, max_decay, HY_W, dtype=F32)
    window = jnp.exp(-t * jnp.abs(deltas))
    return z, window


def _hy_filter_kernel(z_ref, w1_ref, b1_ref, w2_ref, b2_ref, fr_ref, w3_ref, win_ref, h_ref, hid_ref):
    dot = functools.partial(jnp.dot, precision=lax.Precision.HIGHEST, preferred_element_type=F32)

    @pl.when(pl.program_id(0) == 0)
    def _():
        fr = fr_ref[...]
        h = jnp.sin(fr * (dot(z_ref[...], w1_ref[...]) + b1_ref[...]))
        for j in range(HY_INNER):
            h = jnp.sin(fr * (dot(h, w2_ref[j]) + b2_ref[j]))
        hid_ref[...] = h

    h = dot(hid_ref[...], w3_ref[...]) * win_ref[...]
    row = lax.broadcasted_iota(jnp.int32, h.shape, 0)
    backward = pl.program_id(0) >= pl.num_programs(0) // 2
    h_ref[...] = jnp.where((row == 0) & backward, 0.0, h).astype(h_ref.dtype)


def _hy_filters(z, window, w1, b1, w2, b2, freq, w3):
    seq = z.shape[0]
    tn = HY_TC
    per_dir = HY_W // tn
    full = lambda shape: pl.BlockSpec(shape, lambda j: (0,) * len(shape))
    w1p = jnp.pad(w1, ((0, V7X_LANES - HY_EMB), (0, 0)))
    return pl.pallas_call(
        _hy_filter_kernel,
        grid=(2 * per_dir,),
        in_specs=[
            full((seq, V7X_LANES)), full((V7X_LANES, HY_ORDER)), full((1, HY_ORDER)),
            full((HY_INNER, HY_ORDER, HY_ORDER)), full((HY_INNER, 1, HY_ORDER)), full((1, HY_ORDER)),
            pl.BlockSpec((HY_ORDER, tn), lambda j: (0, j)),
            pl.BlockSpec((seq, tn), lambda j: (0, j % per_dir)),
        ],
        out_specs=pl.BlockSpec((seq, tn), lambda j: (0, j)),
        out_shape=jax.ShapeDtypeStruct((seq, 2 * HY_W), BF16),
        scratch_shapes=[pltpu.VMEM((seq, HY_ORDER), F32)],
        compiler_params=_params(("arbitrary",), 32),
        name="hy_filter",
    )(z, w1p, b1.reshape(1, HY_ORDER), w2, b2.reshape(HY_INNER, 1, HY_ORDER), freq.reshape(1, HY_ORDER), w3, window)


def _mm_kernel(a_ref, b_ref, o_ref):
    o_ref[...] = jnp.dot(a_ref[...], b_ref[...], preferred_element_type=F32).astype(o_ref.dtype)


def _mm(a, b, tm, tn, out_dtype):
    m, k = a.shape
    n = b.shape[1]
    return pl.pallas_call(
        _mm_kernel,
        grid=(m // tm, n // tn),
        in_specs=[pl.BlockSpec((tm, k), lambda i, j: (i, 0)), pl.BlockSpec((k, tn), lambda i, j: (0, j))],
        out_specs=pl.BlockSpec((tm, tn), lambda i, j: (i, j)),
        out_shape=jax.ShapeDtypeStruct((m, n), out_dtype),
        compiler_params=_params(("arbitrary", "arbitrary"), 32),
        name="mm",
    )(a, b)


def _hy_spectrum(fmat, hcat, seq):
    spec = _mm(fmat, hcat, min(1024, 2 * seq), 512, F32)
    top, bot = spec[:seq], spec[seq:]
    kr = top[:, :HY_W] + top[:, HY_W:]
    ki = bot[:, :HY_W] - bot[:, HY_W:]
    n = 2 * seq
    first = (jnp.arange(seq) == 0)[:, None]
    pr = jnp.where(first, kr / n, kr * (2.0 / n))
    pi = jnp.where(first, 0.0, ki * (2.0 / n))
    nyq = (bot[:1, :HY_W] + bot[:1, HY_W:]) / n
    return pr, pi, nyq


def _short_conv(x, w_ref, b_ref):
    seq = x.shape[0]
    row = lax.broadcasted_iota(jnp.int32, x.shape, 0)
    prev = jnp.where(row == 0, 0.0, pltpu.roll(x, 1, 0))
    nxt = jnp.where(row == seq - 1, 0.0, pltpu.roll(x, seq - 1, 0))
    return w_ref[0:1, :] * prev + w_ref[1:2, :] * x + w_ref[2:3, :] * nxt + b_ref[...]


def _first_step():
    return (pl.program_id(0) == 0) & (pl.program_id(1) == 0)


def _hy_fwd_kernel(x1_ref, v_ref, w1_ref, b1_ref, wv_ref, bv_ref, f_hbm, pr_ref, pi_ref, nyq_ref, y_ref,
                   f_ref, sem):
    seq = x1_ref.shape[0]
    _load_once(f_hbm, f_ref, sem, _first_step())
    x1 = _short_conv(x1_ref[...].astype(F32), w1_ref, b1_ref)
    v = _short_conv(v_ref[...].astype(F32), wv_ref, bv_ref)
    u = (x1 * v).astype(BF16)
    w = jnp.dot(f_ref[...], u, preferred_element_type=F32)
    a, b = w[:seq], w[seq:]
    pr, pi = pr_ref[...], pi_ref[...]
    row = lax.broadcasted_iota(jnp.int32, pr.shape, 0)
    pd = jnp.where(row == 0, nyq_ref[...], pr)
    y_ref[:seq, :] = (a * pr - b * pi).astype(y_ref.dtype)
    y_ref[seq:, :] = (a * pi + b * pd).astype(y_ref.dtype)


def _hy_inv_kernel(y_ref, x0_ref, x1_ref, v_ref, w0_ref, b0_ref, w1_ref, b1_ref, wv_ref, bv_ref, skip_ref,
                   g_hbm, o_ref, g_ref, sem):
    _load_once(g_hbm, g_ref, sem, _first_step())
    y = jnp.dot(g_ref[...], y_ref[...], preferred_element_type=F32)
    x0 = _short_conv(x0_ref[...].astype(F32), w0_ref, b0_ref)
    x1 = _short_conv(x1_ref[...].astype(F32), w1_ref, b1_ref)
    v = _short_conv(v_ref[...].astype(F32), wv_ref, bv_ref)
    u = x1 * v
    o_ref[...] = (x0 * (y + u * skip_ref[...])).astype(o_ref.dtype)


def _hyena(proj, conv_w, conv_b, skip, fmat, gmat, pr, pi, nyq, batch, seq):
    t = batch * seq
    tc = HY_TC // 2
    nct = HY_W // tc
    conv_b2 = conv_b.reshape(1, 3 * HY_W)

    def xcol(part):
        return pl.BlockSpec((seq, tc), lambda c, b: (b, (OFF_HY + part * HY_W) // tc + c))

    def wcol(part):
        return pl.BlockSpec((HY_SHORT, tc), lambda c, b: (0, part * nct + c))

    def bcol(part):
        return pl.BlockSpec((1, tc), lambda c, b: (0, part * nct + c))

    chan = pl.BlockSpec((seq, tc), lambda c, b: (0, c))
    chan1 = pl.BlockSpec((1, tc), lambda c, b: (0, c))
    whole = pl.BlockSpec(memory_space=pl.ANY)
    dft_scratch = [pltpu.VMEM(fmat.shape, BF16), pltpu.SemaphoreType.DMA(())]
    yspec = pl.BlockSpec((None, 2 * seq, tc), lambda c, b: (b, 0, c))
    yfreq = pl.pallas_call(
        _hy_fwd_kernel,
        grid=(nct, batch),
        in_specs=[xcol(1), xcol(2), wcol(1), bcol(1), wcol(2), bcol(2), whole, chan, chan, chan1],
        out_specs=yspec,
        out_shape=jax.ShapeDtypeStruct((batch, 2 * seq, HY_W), BF16),
        scratch_shapes=dft_scratch,
        compiler_params=_params(("arbitrary", "arbitrary"), 56),
        name="hy_fwd",
    )(proj, proj, conv_w, conv_b2, conv_w, conv_b2, fmat, pr, pi, nyq)
    return pl.pallas_call(
        _hy_inv_kernel,
        grid=(nct, batch),
        in_specs=[yspec, xcol(0), xcol(1), xcol(2), wcol(0), bcol(0), wcol(1), bcol(1), wcol(2), bcol(2), chan1,
                  whole],
        out_specs=pl.BlockSpec((seq, tc), lambda c, b: (b, c)),
        out_shape=jax.ShapeDtypeStruct((t, HY_W), BF16),
        scratch_shapes=[pltpu.VMEM(gmat.shape, BF16), pltpu.SemaphoreType.DMA(())],
        compiler_params=_params(("arbitrary", "arbitrary"), 56),
        name="hy_inv",
    )(yfreq, proj, proj, proj, conv_w, conv_b2, conv_w, conv_b2, conv_w, conv_b2, skip.reshape(1, HY_W), gmat)


def _branch_kernel(ohg_ref, ohy_ref, oat_ref, gates_ref, wb_hbm, m_ref, wb_ref, sem):
    d = D_MODEL
    _load_once(wb_hbm, wb_ref, sem, pl.program_id(0) == 0)
    m = None
    for n, o_ref in enumerate((ohg_ref, ohy_ref, oat_ref)):
        br = jnp.dot(o_ref[...], wb_ref[n], preferred_element_type=F32)
        term = jax.nn.sigmoid(gates_ref[:, n * d:(n + 1) * d].astype(F32)) * br
        m = term if m is None else m + term
    m_ref[...] = m.astype(m_ref.dtype)


def _merge_kernel(m_ref, h_ref, wo_hbm, lng_ref, lnb_ref, rwh_ref, rwl_ref, rb_ref,
                  hmid_ref, hp_ref, idx_ref, gate_ref, wo_ref, sem):
    _load_once(wo_hbm, wo_ref, sem, pl.program_id(0) == 0)
    y = jnp.dot(m_ref[...], wo_ref[...], preferred_element_type=F32)
    hn = _layer_norm_rows(DEEPNORM_ALPHA * h_ref[...] + y, lng_ref[...], lnb_ref[...])
    hmid_ref[...] = hn
    hp_ref[...] = _pack_halves(hn)

    h_hi = hn.astype(BF16)
    h_lo = (hn - h_hi.astype(F32)).astype(BF16)
    logits = (jnp.dot(h_hi, rwh_ref[...], preferred_element_type=F32)
              + jnp.dot(h_lo, rwh_ref[...], preferred_element_type=F32)
              + jnp.dot(h_hi, rwl_ref[...], preferred_element_type=F32)) + rb_ref[...]
    lane = lax.broadcasted_iota(jnp.int32, logits.shape, 1)
    vals, idxs = [], []
    for _ in range(TOP_K):
        mx = jnp.max(logits, axis=-1, keepdims=True)
        ix = jnp.min(jnp.where(logits == mx, lane, ROUTE_LANES), axis=-1, keepdims=True)
        vals.append(mx)
        idxs.append(ix)
        logits = jnp.where(lane == ix, -jnp.inf, logits)
    exps = [jnp.exp(v - vals[0]) for v in vals]
    total = exps[0]
    for e in exps[1:]:
        total = total + e
    gate_out = jnp.zeros(logits.shape, F32)
    idx_out = jnp.zeros(logits.shape, jnp.int32)
    for r in range(TOP_K):
        gate_out = jnp.where(lane == r, exps[r] / total, gate_out)
        idx_out = jnp.where(lane == r, idxs[r], idx_out)
    gate_ref[...] = gate_out
    idx_ref[...] = idx_out


def _merge(o_hg, o_hy, o_at, gates, h, wb, wo, ln_g, ln_b, rw, rb):
    t, d = h.shape
    tm = min(MERGE_TM, t)
    row = lambda width: pl.BlockSpec((tm, width), lambda i: (i, 0))
    const = lambda shape: pl.BlockSpec(shape, lambda i: (0,) * len(shape))
    whole = pl.BlockSpec(memory_space=pl.ANY)
    rwp = jnp.pad(rw, ((0, 0), (0, ROUTE_LANES - N_EXPERTS)))
    rw_hi = rwp.astype(BF16)
    rw_lo = (rwp - rw_hi.astype(F32)).astype(BF16)
    rbp = jnp.pad(rb, (0, ROUTE_LANES - N_EXPERTS), constant_values=MASK_VALUE).reshape(1, ROUTE_LANES)
    m = pl.pallas_call(
        _branch_kernel,
        grid=(t // tm,),
        in_specs=[row(MIX_W), row(MIX_W), row(MIX_W), row(N_BRANCH * d), whole],
        out_specs=row(d),
        out_shape=jax.ShapeDtypeStruct((t, d), BF16),
        scratch_shapes=[pltpu.VMEM((N_BRANCH, MIX_W, d), BF16), pltpu.SemaphoreType.DMA(())],
        compiler_params=_params(("arbitrary",), 56),
        name="branch_merge",
    )(o_hg, o_hy, o_at, gates, wb)
    return pl.pallas_call(
        _merge_kernel,
        grid=(t // tm,),
        in_specs=[row(d), row(d), whole, const((1, d)), const((1, d)),
                  const((d, ROUTE_LANES)), const((d, ROUTE_LANES)), const((1, ROUTE_LANES))],
        out_specs=[row(d), row(d // 2), row(ROUTE_LANES), row(ROUTE_LANES)],
        out_shape=[jax.ShapeDtypeStruct((t, d), F32), jax.ShapeDtypeStruct((t, d // 2), jnp.uint32),
                   jax.ShapeDtypeStruct((t, ROUTE_LANES), jnp.int32), jax.ShapeDtypeStruct((t, ROUTE_LANES), F32)],
        scratch_shapes=[pltpu.VMEM((d, d), BF16), pltpu.SemaphoreType.DMA(())],
        compiler_params=_params(("arbitrary",), 56),
        name="merge",
    )(m, h, wo, ln_g.reshape(1, d), ln_b.reshape(1, d), rw_hi, rw_lo, rbp)


def _route_plan(top_idx, n_tiles):
    e = top_idx.reshape(-1)
    onehot = (e[:, None] == jnp.arange(N_EXPERTS, dtype=jnp.int32)[None, :]).astype(jnp.int32)
    csum = jnp.cumsum(onehot, axis=0)
    rank = jnp.sum(csum * onehot, axis=1) - 1
    counts = csum[-1]
    padded = (counts + MOE_TM - 1) // MOE_TM * MOE_TM
    p_end = jnp.cumsum(padded)
    p_start = p_end - padded
    pos = (p_start[e] + rank).astype(jnp.int32)
    n_used = (p_end[-1] // MOE_TM).astype(jnp.int32)
    tile_start = jnp.arange(n_tiles, dtype=jnp.int32) * MOE_TM
    tile_expert = jnp.minimum(jnp.searchsorted(p_end, tile_start, side='right'), N_EXPERTS - 1).astype(jnp.int32)
    tile_rows = jnp.clip(counts[tile_expert] - (tile_start - p_start[tile_expert]), 0, MOE_TM).astype(jnp.int32)
    return pos, tile_expert, tile_rows, n_used.reshape(1)


def _dispatch_kernel(pos_ref, hp_ref, xs_ref, sem):
    def body(r, carry):
        for k in range(TOP_K):
            dst = pos_ref[0, r * TOP_K + k]
            pltpu.make_async_copy(hp_ref.at[pl.ds(r, 1), :], xs_ref.at[pl.ds(dst, 1), :], sem).start(priority=k % 2)
        return carry

    lax.fori_loop(0, hp_ref.shape[0], body, 0)
    for k in range(TOP_K):
        pltpu.make_async_copy(hp_ref, xs_ref.at[pl.ds(0, hp_ref.shape[0]), :], sem).wait()


def _dispatch(hp, pos, n_slots):
    t, dw = hp.shape
    tq = min(TOK_TILE, t)
    return pl.pallas_call(
        _dispatch_kernel,
        grid=(t // tq,),
        in_specs=[pl.BlockSpec((None, 1, tq * TOP_K), lambda i: (i, 0, 0), memory_space=pltpu.SMEM),
                  pl.BlockSpec((tq, dw), lambda i: (i, 0))],
        out_specs=pl.BlockSpec(memory_space=pl.ANY),
        out_shape=jax.ShapeDtypeStruct((n_slots, dw), jnp.uint32),
        scratch_shapes=[pltpu.SemaphoreType.DMA(())],
        compiler_params=_params(("arbitrary",), 32),
        name="moe_dispatch",
    )(pos.reshape(t // tq, 1, tq * TOP_K), hp)


def _ffn_kernel(te_ref, tr_ref, nu_ref, x_ref, wg_ref, wu_ref, bgu_ref, wdl_ref, wdh_ref, bd_ref,
                y_ref, xb_ref, act_ref):
    i = pl.program_id(0)
    j = pl.program_id(1)
    rows = tr_ref[i]
    active = i < nu_ref[0]
    half = x_ref.shape[1]
    nf = D_FF // MOE_TF
    nsub = (rows + MOE_SUB - 1) // MOE_SUB

    @pl.when(active & (j == 0))
    def _():
        lo, hi = _unpack_halves(x_ref[...])
        keep = lax.broadcasted_iota(jnp.int32, lo.shape, 0) < rows
        xb_ref[:, :half] = jnp.where(keep, lo, 0.0).astype(BF16)
        xb_ref[:, half:] = jnp.where(keep, hi, 0.0).astype(BF16)

    fcol = pl.multiple_of(jnp.minimum(j, nf - 1) * MOE_TF, MOE_TF)
    ncol = pl.multiple_of(jnp.maximum(j - nf, 0) * MOE_TN, MOE_TN)

    for k in range(1, MOE_TM // MOE_SUB + 1):
        m = k * MOE_SUB

        @pl.when(active & (j < nf) & (nsub == k))
        def _():
            xs = xb_ref[:m, :]
            g = jnp.dot(xs, wg_ref[...].astype(BF16), preferred_element_type=F32) + bgu_ref[:, pl.ds(fcol, MOE_TF)]
            u = (jnp.dot(xs, wu_ref[...].astype(BF16), preferred_element_type=F32)
                 + bgu_ref[:, pl.ds(D_FF + fcol, MOE_TF)])
            g = jnp.minimum(g, SWIGLU_LIMIT)
            u = jnp.clip(u, -SWIGLU_LIMIT, SWIGLU_LIMIT)
            act_ref[:m, pl.ds(fcol, MOE_TF)] = ((u + 1.0) * g * jax.nn.sigmoid(SWIGLU_ALPHA * g)).astype(BF16)

        @pl.when(active & (j >= nf) & (nsub == k))
        def _():
            act = act_ref[:m, :]
            lo = (jnp.dot(act, wdl_ref[...].astype(BF16), preferred_element_type=F32)
                  + bd_ref[:, pl.ds(ncol, MOE_TN)])
            hi = (jnp.dot(act, wdh_ref[...].astype(BF16), preferred_element_type=F32)
                  + bd_ref[:, pl.ds(half + ncol, MOE_TN)])
            y_ref[:m, :] = _pack_pair(lo, hi)
            if m < MOE_TM:
                y_ref[m:, :] = jnp.zeros((MOE_TM - m, y_ref.shape[1]), jnp.uint32)


def _ffn(xs, w_gate_up, b_gate_up, w_down, b_down, layer, tile_expert, tile_rows, n_used):
    n_slots, dw = xs.shape
    d = 2 * dw
    n_tiles = n_slots // MOE_TM
    nf = D_FF // MOE_TF
    nn = dw // MOE_TN
    bgu = b_gate_up.reshape(DEPTH, N_EXPERTS, 1, 2 * D_FF)
    bd = b_down.reshape(DEPTH, N_EXPERTS, 1, d)

    def tile(i, nu):
        return jnp.minimum(i, nu[0] - 1)

    def fcol(i, j, nu):
        return jnp.where(i < nu[0], jnp.minimum(j, nf - 1), nf - 1)

    def ncol(i, j, nu):
        return jnp.where(i < nu[0], jnp.maximum(j - nf, 0), nn - 1)

    xmap = lambda i, j, te, tr, nu: (tile(i, nu), 0)
    gmap = lambda i, j, te, tr, nu: (layer, te[tile(i, nu)], 0, fcol(i, j, nu))
    umap = lambda i, j, te, tr, nu: (layer, te[tile(i, nu)], 0, nf + fcol(i, j, nu))
    dlmap = lambda i, j, te, tr, nu: (layer, te[tile(i, nu)], 0, ncol(i, j, nu))
    dhmap = lambda i, j, te, tr, nu: (layer, te[tile(i, nu)], 0, nn + ncol(i, j, nu))
    bmap = lambda i, j, te, tr, nu: (layer, te[tile(i, nu)], 0, 0)
    ymap = lambda i, j, te, tr, nu: (tile(i, nu), ncol(i, j, nu))
    grid_spec = pltpu.PrefetchScalarGridSpec(
        num_scalar_prefetch=3,
        grid=(n_tiles, nf + nn),
        in_specs=[
            pl.BlockSpec((MOE_TM, dw), xmap, pipeline_mode=pl.Buffered(1)),
            pl.BlockSpec((None, None, d, MOE_TF), gmap),
            pl.BlockSpec((None, None, d, MOE_TF), umap),
            pl.BlockSpec((None, None, 1, 2 * D_FF), bmap),
            pl.BlockSpec((None, None, D_FF, MOE_TN), dlmap),
            pl.BlockSpec((None, None, D_FF, MOE_TN), dhmap),
            pl.BlockSpec((None, None, 1, d), bmap),
        ],
        out_specs=pl.BlockSpec((MOE_TM, MOE_TN), ymap),
        scratch_shapes=[pltpu.VMEM((MOE_TM, d), BF16), pltpu.VMEM((MOE_TM, D_FF), BF16)],
    )
    return pl.pallas_call(
        _ffn_kernel,
        grid_spec=grid_spec,
        out_shape=jax.ShapeDtypeStruct((n_slots, dw), jnp.uint32),
        compiler_params=_params(("arbitrary", "arbitrary"), 56),
        name="moe_ffn",
    )(tile_expert, tile_rows, n_used, xs, w_gate_up, w_gate_up, bgu, w_down, w_down, bd)


def _combine_kernel(pos_ref, posn_ref, ys_ref, gate_ref, h_ref, lng_ref, lnb_ref, hout_ref, hb_ref, buf, sem):
    i = pl.program_id(0)
    tq, d = h_ref.shape
    half = d // 2
    slot = i % 2

    def issue(p_ref, s):
        def body(r, carry):
            for k in range(TOP_K):
                src = p_ref[0, r * TOP_K + k]
                pltpu.make_async_copy(ys_ref.at[pl.ds(src, 1), :], buf.at[s, k, pl.ds(r, 1), :],
                                      sem.at[s]).start(priority=k % 2)
            return carry

        lax.fori_loop(0, tq, body, 0, unroll=2)

    @pl.when(i == 0)
    def _():
        issue(pos_ref, 0)

    for s in range(2):
        @pl.when((i + 1 < pl.num_programs(0)) & (slot == 1 - s))
        def _():
            issue(posn_ref, s)

    for k in range(TOP_K):
        pltpu.make_async_copy(ys_ref.at[pl.ds(0, tq), :], buf.at[slot, k], sem.at[slot]).wait()
    gate = gate_ref[...]
    acc_lo = DEEPNORM_ALPHA * h_ref[:, :half]
    acc_hi = DEEPNORM_ALPHA * h_ref[:, half:]
    for k in range(TOP_K):
        lo, hi = _unpack_halves(buf[slot, k])
        acc_lo = acc_lo + gate[:, k:k + 1] * lo
        acc_hi = acc_hi + gate[:, k:k + 1] * hi
    mu = (jnp.sum(acc_lo, axis=-1, keepdims=True) + jnp.sum(acc_hi, axis=-1, keepdims=True)) / d
    acc_lo = acc_lo - mu
    acc_hi = acc_hi - mu
    var = (jnp.sum(acc_lo * acc_lo, axis=-1, keepdims=True) + jnp.sum(acc_hi * acc_hi, axis=-1, keepdims=True)) / d
    inv = lax.rsqrt(var + LN_EPS)
    for sl, acc in ((slice(0, half), acc_lo), (slice(half, d), acc_hi)):
        hn = acc * inv * lng_ref[:, sl] + lnb_ref[:, sl]
        hout_ref[:, sl] = hn
        hb_ref[:, sl] = hn.astype(BF16)


def _combine(ys, pos, gate, h, ln_g, ln_b):
    t, d = h.shape
    tq = min(TOK_TILE, t)
    nt = t // tq
    row = lambda width: pl.BlockSpec((tq, width), lambda i: (i, 0))
    vec = pl.BlockSpec((1, d), lambda i: (0, 0))
    pos3 = pos.reshape(nt, 1, tq * TOP_K)
    return pl.pallas_call(
        _combine_kernel,
        grid=(nt,),
        in_specs=[pl.BlockSpec((None, 1, tq * TOP_K), lambda i: (i, 0, 0), memory_space=pltpu.SMEM),
                  pl.BlockSpec((None, 1, tq * TOP_K), lambda i: (jnp.minimum(i + 1, nt - 1), 0, 0),
                               memory_space=pltpu.SMEM),
                  pl.BlockSpec(memory_space=pl.ANY), row(ROUTE_LANES), row(d), vec, vec],
        out_specs=[row(d), row(d)],
        out_shape=[jax.ShapeDtypeStruct((t, d), F32), jax.ShapeDtypeStruct((t, d), BF16)],
        scratch_shapes=[pltpu.VMEM((2, TOP_K, tq, d // 2), jnp.uint32), pltpu.SemaphoreType.DMA((2,))],
        compiler_params=_params(("arbitrary",), 40),
        name="moe_combine",
    )(pos3, pos3, ys, gate, h, ln_g.reshape(1, d), ln_b.reshape(1, d))


def _hgrn_lower_bound(lb_table, layer):
    p = jax.nn.softmax(lb_table.astype(F32), axis=0)
    return jnp.cumsum(p, axis=0)[layer] - p[0]


def kernel(x, ln_in_g, ln_in_b, w_in, hg_lower_bound, hg_norm_g, hy_conv_w, hy_conv_b, hy_filt_w1, hy_filt_b1,
           hy_filt_w2, hy_filt_b2, hy_filt_freq, hy_filt_w3, hy_skip, att_sink, rel_bias, w_branch, w_out,
           ln_mix_g, ln_mix_b, router_w, router_b, w_gate_up, b_gate_up, w_down, b_down, ln_moe_g, ln_moe_b):
    batch, seq, d = x.shape
    t = batch * seq
    n_tiles = t * TOP_K // MOE_TM + N_EXPERTS
    n_slots = n_tiles * MOE_TM

    fmat, gmat = _dft_tables(seq)
    z_pos, window = _hy_positions(seq)
    bias_tab = _attn_bias_table(rel_bias)

    h, hb = _ln_in(x.reshape(t, d), ln_in_g, ln_in_b)
    for layer in range(DEPTH):
        proj = _inproj(hb, w_in, layer, 0, OFF_AK, PROJ_TN)
        proj_kv = _inproj(hb, w_in, layer, OFF_AK, PROJ_KV_COLS, PROJ_KV_COLS)
        w_gates = lax.slice(w_in, (layer, 0, OFF_GATES), (layer + 1, d, IN_COLS))
        gates = _inproj(hb, w_gates, 0, 0, N_BRANCH * d, PROJ_TN)

        lb = _hgrn_lower_bound(hg_lower_bound, layer)
        o_hg = _hgrn(proj, lb, hg_norm_g[layer], batch, seq)

        hcat = _hy_filters(z_pos, window, hy_filt_w1[layer], hy_filt_b1[layer], hy_filt_w2[layer],
                           hy_filt_b2[layer], hy_filt_freq[layer], hy_filt_w3[layer])
        pr, pi, nyq = _hy_spectrum(fmat, hcat, seq)
        o_hy = _hyena(proj, hy_conv_w[layer], hy_conv_b[layer], hy_skip[layer], fmat, gmat, pr, pi, nyq,
                      batch, seq)

        o_at = _attn(proj, proj_kv, att_sink[layer], bias_tab, batch, seq)

        h_mid, hp, top_idx, gate = _merge(
            o_hg, o_hy, o_at, gates, h, w_branch[layer].astype(BF16), w_out[layer].astype(BF16),
            ln_mix_g[layer], ln_mix_b[layer], router_w[layer], router_b[layer])

        pos, tile_expert, tile_rows, n_used = _route_plan(top_idx[:, :TOP_K], n_tiles)
        xs = _dispatch(hp, pos, n_slots)
        ys = _ffn(xs, w_gate_up, b_gate_up, w_down, b_down, layer, tile_expert, tile_rows, n_used)
        h, hb = _combine(ys, pos, gate, h_mid, ln_moe_g[layer], ln_moe_b[layer])
    return h.reshape(batch, seq, d)
```

```python
import functools
import math

import jax
import jax.numpy as jnp
from jax import lax
from jax.experimental import pallas as pl
from jax.experimental.pallas import tpu as pltpu

F32 = jnp.float32
BF16 = jnp.bfloat16

D_MODEL = 2048
DEPTH = 2
MIX_W = 1024
N_BRANCH = 3
HG_HEADS = 8
HG_DK = 128
HG_DV = 128
HG_CHUNK = 16
F_FLOOR = 1e-30
HY_W = 1024
HY_SHORT = 3
HY_EMB = 33
HY_BANDS = (HY_EMB - 1) // 2
HY_ORDER = 64
HY_INNER = 2
HY_FAST_DECAY = 0.3
HY_SLOW_DECAY = 1.5
HY_TARGET = 1e-2
ATT_HEADS = 16
ATT_KV_HEADS = 2
ATT_DH = 64
WINDOW = 128
ATT_BLOCK = 128
REL_BUCKETS = 32
REL_MAX_DIST = 128
MASK_VALUE = -1e30
N_EXPERTS = 32
TOP_K = 4
D_FF = 2048
SWIGLU_ALPHA = 1.702
SWIGLU_LIMIT = 7.0
LN_EPS = 1e-5
RMS_EPS = 1e-6
DEEPNORM_ALPHA = (2 * DEPTH) ** 0.25

HG_KW = HG_HEADS * HG_DK
HG_VW = HG_HEADS * HG_DV
ATT_QW = ATT_HEADS * ATT_DH
ATT_KVW = ATT_KV_HEADS * ATT_DH
IN_SIZES = (HG_KW, HG_KW, HG_KW, HG_VW, HG_VW, 3 * HY_W, ATT_QW, ATT_KVW, ATT_KVW, N_BRANCH * D_MODEL)
IN_COLS = sum(IN_SIZES)
OFF_HQ, OFF_HFF, OFF_HFB, OFF_HI, OFF_HOG = 0, HG_KW, 2 * HG_KW, 3 * HG_KW, 3 * HG_KW + HG_VW
OFF_HY = OFF_HOG + HG_VW
OFF_AQ = OFF_HY + 3 * HY_W
OFF_AK = OFF_AQ + ATT_QW
OFF_AV = OFF_AK + ATT_KVW
OFF_GATES = OFF_AV + ATT_KVW
MAIN_COLS = OFF_GATES

V7X_LANES = 128
V7X_VMEM_BYTES = 64 * 1024 * 1024

LN_ROWS = 512
PROJ_TM = 1024
PROJ_TN = 1024
PROJ_KV_COLS = 2 * ATT_KVW
HY_TC = 512
MERGE_TM = 512
MOE_TM = 1024
MOE_SUB = 256
MOE_TF = 512
MOE_TN = 512
TOK_TILE = 256
ROUTE_LANES = 128


def _params(semantics, vmem_mb):
    return pltpu.CompilerParams(dimension_semantics=semantics, vmem_limit_bytes=vmem_mb * 1024 * 1024)


def _layer_norm_rows(x, g, b):
    mu = jnp.mean(x, axis=-1, keepdims=True)
    xc = x - mu
    var = jnp.mean(xc * xc, axis=-1, keepdims=True)
    return xc * lax.rsqrt(var + LN_EPS) * g + b


def _pack_halves(x):
    c = x.shape[-1] // 2
    return _pack_pair(x[:, :c], x[:, c:])


def _pack_pair(lo, hi):
    lo = pltpu.bitcast(lo.astype(BF16).astype(F32), jnp.uint32)
    hi = pltpu.bitcast(hi.astype(BF16).astype(F32), jnp.uint32)
    return (lo >> 16) | (hi & jnp.uint32(0xFFFF0000))


def _unpack_halves(w):
    lo = pltpu.bitcast(w << 16, F32)
    hi = pltpu.bitcast(w & jnp.uint32(0xFFFF0000), F32)
    return lo, hi


def _ln_in_kernel(x_ref, g_ref, b_ref, h_ref, hb_ref):
    y = _layer_norm_rows(x_ref[...], g_ref[...], b_ref[...])
    h_ref[...] = y
    hb_ref[...] = y.astype(BF16)


def _ln_in(x2, g, b):
    t, d = x2.shape
    tm = min(LN_ROWS, t)
    row = pl.BlockSpec((tm, d), lambda i: (i, 0))
    vec = pl.BlockSpec((1, d), lambda i: (0, 0))
    return pl.pallas_call(
        _ln_in_kernel,
        grid=(t // tm,),
        in_specs=[row, vec, vec],
        out_specs=[row, row],
        out_shape=[jax.ShapeDtypeStruct((t, d), F32), jax.ShapeDtypeStruct((t, d), BF16)],
        compiler_params=_params(("arbitrary",), 32),
        name="ln_in",
    )(x2, g.reshape(1, d), b.reshape(1, d))


def _inproj_kernel(a_ref, w_ref, o_ref, wb_ref):
    @pl.when(pl.program_id(1) == 0)
    def _():
        wb_ref[...] = w_ref[...].astype(BF16)

    o_ref[...] = jnp.dot(a_ref[...], wb_ref[...], preferred_element_type=F32).astype(o_ref.dtype)


def _inproj(hb, w_in, layer, col0, ncols, tn):
    t, d = hb.shape
    tm = min(PROJ_TM, t)
    cb0 = col0 // tn
    return pl.pallas_call(
        _inproj_kernel,
        grid=(ncols // tn, t // tm),
        in_specs=[
            pl.BlockSpec((tm, d), lambda j, i: (i, 0)),
            pl.BlockSpec((None, d, tn), lambda j, i: (layer, 0, cb0 + j)),
        ],
        out_specs=pl.BlockSpec((tm, tn), lambda j, i: (i, j)),
        out_shape=jax.ShapeDtypeStruct((t, ncols), BF16),
        scratch_shapes=[pltpu.VMEM((d, tn), BF16)],
        compiler_params=_params(("arbitrary", "arbitrary"), 48),
        name="inproj",
    )(hb, w_in)


HG_BLK = 128
HG_CPB = HG_BLK // HG_CHUNK
HG_UNROLL = 16
LOG2E = 1.4426950408889634


def _hgrn_perm():
    r = jnp.arange(HG_BLK)
    src = (r % HG_CPB) * HG_CHUNK + r // HG_CPB
    return (src[:, None] == jnp.arange(HG_BLK)[None, :]).astype(BF16)


def _hgrn_kernel(q_ref, ff_ref, fb_ref, i_ref, og_ref, lbf_ref, lbb_ref, g_ref, p_ref, pt_ref, o_ref,
                 x_t, kk_t, o_t, tmp_t, qtil_c, ktil_c, kv_s, st_s, oint_c, dec_c, *, seq):
    c = HG_CHUNK
    nc = seq // c
    nb = seq // HG_BLK
    nbh = max(nb // 2, 1)
    n_half = nb // nbh
    perm = p_ref[...]
    perm_t = pt_ref[...]

    for a, ref in enumerate((q_ref, ff_ref, fb_ref, i_ref)):
        for r in range(nb):
            xp = jnp.dot(perm, ref[r * HG_BLK:(r + 1) * HG_BLK, :], preferred_element_type=F32)
            x_t[a, r] = xp.reshape(c, HG_CPB, HG_DK)

    lbs = (lbf_ref[...], lbb_ref[...])
    for j in range(c):
        qj = x_t[0, :, j]
        x_t[0, :, j] = qj * jax.nn.sigmoid(qj) * (HG_DK ** -0.5)
        for d in range(2):
            z = x_t[1 + d, :, j]
            lb = lbs[d]
            e = jnp.exp(-jnp.abs(z))
            r_ = 1.0 / (1.0 + e)
            er = e * r_
            sig = jnp.where(z >= 0, r_, er)
            nsig = jnp.where(z >= 0, er, r_)
            f = lb + (1.0 - lb) * sig
            x_t[1 + d, :, j] = jnp.log(jnp.maximum(f, F_FLOOR)) * LOG2E
            kk_t[d, :, j] = (1.0 - lb) * nsig

    for d in range(2):
        acc = jnp.zeros((nb, HG_CPB, HG_DK), F32)
        for j in (range(c) if d == 0 else reversed(range(c))):
            acc = acc + x_t[1 + d, :, j]
            x_t[1 + d, :, j] = acc
        last = acc
        dec_c[d] = jnp.exp2(last).reshape(nc, HG_DK)
        for j in range(c):
            cj = x_t[1 + d, :, j]
            tmp_t[0, :, j] = x_t[0, :, j] * jnp.exp2(cj)
            tmp_t[1, :, j] = kk_t[d, :, j] * jnp.exp2(last - cj)
        for r in range(nb):
            rows = slice(r * HG_BLK, (r + 1) * HG_BLK)
            qb = tmp_t[0, r].reshape(HG_BLK, HG_DK).astype(BF16)
            kb = tmp_t[1, r].reshape(HG_BLK, HG_DK).astype(BF16)
            qtil_c[d, rows, :] = jnp.dot(perm_t, qb, preferred_element_type=F32).astype(BF16)
            ktil_c[rows, d * HG_DK:(d + 1) * HG_DK] = jnp.dot(perm_t, kb, preferred_element_type=F32).astype(BF16)

    o_t[...] = jnp.zeros(o_t.shape, F32)
    ones_b = jnp.ones((HG_DK, HG_DV), BF16)

    def pair_body(it, carry):
        d = it // n_half
        sl = pl.ds((it % n_half) * nbh, nbh)
        for t in range(c):
            rt = t + d * (c - 1 - 2 * t)
            ct = x_t[1 + d, sl, rt]
            qt = x_t[0, sl, rt]
            acc = jnp.zeros((nbh * HG_CPB, HG_DV), F32)
            for s in range(t + 1):
                rs = s + d * (c - 1 - 2 * s)
                e = jnp.exp2(jnp.minimum(ct - x_t[1 + d, sl, rs], 0.0))
                a = (qt * e * kk_t[d, sl, rs]).reshape(nbh * HG_CPB, HG_DK).astype(BF16)
                p = jnp.dot(a, ones_b, preferred_element_type=F32)
                acc = acc + p * x_t[3, sl, rs].reshape(nbh * HG_CPB, HG_DV)
            o_t[sl, rt] += acc.reshape(nbh, HG_CPB, HG_DV)
        return carry

    lax.fori_loop(0, 2 * n_half, pair_body, 0)

    def kv_body(n, carry):
        r0 = pl.multiple_of(n * c, c)
        kv = lax.dot_general(i_ref[pl.ds(r0, c), :], ktil_c[pl.ds(r0, c), :], (((0,), (0,)), ((), ())),
                             preferred_element_type=F32)
        kv_s[0, n] = kv[:, :HG_DK]
        kv_s[1, n] = kv[:, HG_DK:]
        return carry

    lax.fori_loop(0, nc, kv_body, 0, unroll=HG_UNROLL)

    def chain_body(idx, carry):
        new = []
        for d in range(2):
            n = idx if d == 0 else nc - 1 - idx
            s = carry[d]
            new.append(dec_c[d, pl.ds(n, 1), :] * s + kv_s[d, n])
            st_s[n, d * HG_DV:(d + 1) * HG_DV, :] = s.astype(BF16)
        return tuple(new)

    zero_state = jnp.zeros((HG_DV, HG_DK), F32)
    lax.fori_loop(0, nc, chain_body, (zero_state, zero_state), unroll=2)

    def out_body(n, carry):
        r0 = pl.multiple_of(n * c, c)
        lhs = jnp.concatenate([qtil_c[0, pl.ds(r0, c), :], qtil_c[1, pl.ds(r0, c), :]], axis=0)
        out = lax.dot_general(lhs, st_s[n], (((1,), (1,)), ((), ())), preferred_element_type=F32)
        oint_c[0, pl.ds(r0, c), :] = out[:c, :HG_DV]
        oint_c[1, pl.ds(r0, c), :] = out[c:, HG_DV:]
        return carry

    lax.fori_loop(0, nc, out_body, 0, unroll=HG_UNROLL)

    for r in range(nb):
        rows = slice(r * HG_BLK, (r + 1) * HG_BLK)
        ob = o_t[r].reshape(HG_BLK, HG_DV)
        hi = ob.astype(BF16)
        lo = (ob - hi.astype(F32)).astype(BF16)
        oc = jnp.dot(perm_t, hi, preferred_element_type=F32) + jnp.dot(perm_t, lo, preferred_element_type=F32)
        o = oc + oint_c[0, rows, :] + oint_c[1, rows, :]
        o = o * lax.rsqrt(jnp.mean(o * o, axis=-1, keepdims=True) + RMS_EPS) * g_ref[...]
        og = og_ref[rows, :].astype(F32)
        o_ref[rows, :] = (o * (og * jax.nn.sigmoid(og))).astype(o_ref.dtype)


def _hgrn(proj, lb, norm_g, batch, seq):
    t = batch * seq
    nc = seq // HG_CHUNK

    def col(off):
        return pl.BlockSpec((seq, HG_DK), lambda b, h: (b, off // HG_DK + h))

    vec = pl.BlockSpec((None, 1, HG_DK), lambda b, h: (h, 0, 0))
    lbf = lb[:HG_KW].reshape(HG_HEADS, 1, HG_DK)
    lbb = lb[HG_KW:].reshape(HG_HEADS, 1, HG_DK)
    g = norm_g.reshape(HG_HEADS, 1, HG_DV)
    slab = (seq // HG_BLK, HG_CHUNK, HG_CPB, HG_DK)
    perm = _hgrn_perm()
    pspec = pl.BlockSpec((HG_BLK, HG_BLK), lambda b, h: (0, 0))
    return pl.pallas_call(
        functools.partial(_hgrn_kernel, seq=seq),
        grid=(batch, HG_HEADS),
        in_specs=[col(OFF_HQ), col(OFF_HFF), col(OFF_HFB), col(OFF_HI), col(OFF_HOG), vec, vec, vec, pspec, pspec],
        out_specs=pl.BlockSpec((seq, HG_DV), lambda b, h: (b, h)),
        out_shape=jax.ShapeDtypeStruct((t, HG_VW), BF16),
        scratch_shapes=[
            pltpu.VMEM((4,) + slab, F32),
            pltpu.VMEM((2,) + slab, F32),
            pltpu.VMEM(slab, F32),
            pltpu.VMEM((2,) + slab, F32),
            pltpu.VMEM((2, seq, HG_DK), BF16),
            pltpu.VMEM((seq, 2 * HG_DK), BF16),
            pltpu.VMEM((2, nc, HG_DV, HG_DK), F32),
            pltpu.VMEM((nc, 2 * HG_DV, HG_DK), BF16),
            pltpu.VMEM((2, seq, HG_DV), F32),
            pltpu.VMEM((2, nc, HG_DK), F32),
        ],
        compiler_params=_params(("arbitrary", "arbitrary"), 48),
        name="hgrn2",
    )(proj, proj, proj, proj, proj, lbf, lbb, g, perm, perm.T)


def _attn_kernel(sink_ref, q_ref, kp_ref, ko_ref, kn_ref, vp_ref, vo_ref, vn_ref, bias_ref, o_ref, s_ref, p_ref):
    w = ATT_BLOCK
    group = ATT_HEADS // ATT_KV_HEADS
    kband = jnp.concatenate([kp_ref[...], ko_ref[...], kn_ref[...]], axis=0)
    vband = jnp.concatenate([vp_ref[...], vo_ref[...], vn_ref[...]], axis=0)
    ones = jnp.ones((3 * w, ATT_DH), BF16)
    q = q_ref[...] * jnp.asarray(ATT_DH ** -0.5, BF16)
    kgs = [kband[:, g * ATT_DH:(g + 1) * ATT_DH] for g in range(ATT_KV_HEADS)]
    vg1s = [jnp.concatenate([vband[:, g * ATT_DH:(g + 1) * ATT_DH], ones], axis=1) for g in range(ATT_KV_HEADS)]
    for h in range(ATT_HEADS):
        s_ref[h] = lax.dot_general(q[:, h * ATT_DH:(h + 1) * ATT_DH], kgs[h // group], (((1,), (1,)), ((), ())),
                                   preferred_element_type=F32)
    exps = []
    for h in range(ATT_HEADS):
        s = s_ref[h] * LOG2E + bias_ref[h]
        sk = sink_ref[h] * LOG2E
        m = jnp.maximum(jnp.max(s, axis=-1, keepdims=True), sk)
        p_ref[h] = jnp.exp2(s - m).astype(BF16)
        exps.append(jnp.exp2(sk - m))
    outs = []
    for h in range(ATT_HEADS):
        ov = jnp.dot(p_ref[h], vg1s[h // group], preferred_element_type=F32)
        outs.append(ov[:, :ATT_DH] / (ov[:, ATT_DH:ATT_DH + 1] + exps[h]))
    o_ref[...] = jnp.concatenate(outs, axis=1).astype(o_ref.dtype)


def _t5_relative_bucket(rel):
    half = REL_BUCKETS // 2
    max_exact = half // 2
    bucket = (rel > 0).astype(jnp.int32) * half
    n = jnp.abs(rel)
    n_safe = jnp.maximum(n, 1).astype(F32)
    large = max_exact + (jnp.log(n_safe / max_exact) / math.log(REL_MAX_DIST / max_exact)
                         * (half - max_exact)).astype(jnp.int32)
    large = jnp.clip(large, 0, half - 1)
    return bucket + jnp.where(n < max_exact, n, large)


def _attn_bias_table(rel_bias):
    w = ATT_BLOCK
    kofs = jnp.arange(3 * w, dtype=jnp.int32)[None, :] - w
    rel = kofs - jnp.arange(w, dtype=jnp.int32)[:, None]
    onehot = (_t5_relative_bucket(rel)[:, :, None] == jnp.arange(REL_BUCKETS)[None, None, :]).astype(F32)
    bias = jnp.einsum('qkb,bh->hqk', onehot, rel_bias.astype(F32), precision=lax.Precision.HIGHEST)
    band = jnp.abs(rel) <= WINDOW
    tabs = []
    for first, last in ((False, False), (True, False), (False, True), (True, True)):
        ok = band & ((kofs >= 0) | (not first)) & ((kofs < w) | (not last))
        tabs.append(jnp.where(ok[None], bias * LOG2E, MASK_VALUE))
    return jnp.stack(tabs)


def _attn(proj, proj_kv, sink, bias_tab, batch, seq):
    t = batch * seq
    w = ATT_BLOCK
    nb = seq // w
    kcol, vcol = 0, 1

    def kv(col, delta):
        return pl.BlockSpec((w, ATT_KVW), lambda b, n: (b * nb + jnp.clip(n + delta, 0, nb - 1), col))

    def variant(b, n):
        return ((n == 0).astype(jnp.int32) + 2 * (n == nb - 1).astype(jnp.int32), 0, 0, 0)

    return pl.pallas_call(
        _attn_kernel,
        grid=(batch, nb),
        in_specs=[
            pl.BlockSpec(memory_space=pltpu.SMEM),
            pl.BlockSpec((w, ATT_QW), lambda b, n: (b * nb + n, OFF_AQ // ATT_QW)),
            kv(kcol, -1), kv(kcol, 0), kv(kcol, 1),
            kv(vcol, -1), kv(vcol, 0), kv(vcol, 1),
            pl.BlockSpec((None, ATT_HEADS, w, 3 * w), variant),
        ],
        out_specs=pl.BlockSpec((w, ATT_QW), lambda b, n: (b * nb + n, 0)),
        out_shape=jax.ShapeDtypeStruct((t, ATT_QW), BF16),
        scratch_shapes=[pltpu.VMEM((ATT_HEADS, w, 3 * w), F32), pltpu.VMEM((ATT_HEADS, w, 3 * w), BF16)],
        compiler_params=_params(("arbitrary", "arbitrary"), 32),
        name="win_attn",
    )(sink.astype(F32), proj, proj_kv, proj_kv, proj_kv, proj_kv, proj_kv, proj_kv, bias_tab)


DFT_ROWS = 64


def _load_once(src_hbm, dst_vmem, sem, first):
    @pl.when(first)
    def _():
        cp = pltpu.make_async_copy(src_hbm, dst_vmem, sem)
        cp.start()
        cp.wait()


def _dft_tables(seq):
    n = 2 * seq
    k = jnp.arange(seq, dtype=jnp.int32)[:, None]
    s = jnp.arange(seq, dtype=jnp.int32)[None, :]
    theta = 2.0 * math.pi / n
    ang_a = ((jnp.arange(0, seq, DFT_ROWS, dtype=jnp.int32)[:, None] * s) % n).astype(F32) * theta
    ang_b = ((jnp.arange(DFT_ROWS, dtype=jnp.int32)[:, None] * s) % n).astype(F32) * theta
    ca, sa = jnp.cos(ang_a)[:, None, :], jnp.sin(ang_a)[:, None, :]
    cb, sb = jnp.cos(ang_b)[None], jnp.sin(ang_b)[None]
    cm = (ca * cb - sa * sb).reshape(seq, seq)
    sm = -(sa * cb + ca * sb).reshape(seq, seq)
    nyq = jnp.where(s % 2 == 0, 1.0, -1.0).astype(F32)
    sm = jnp.where(k == 0, nyq, sm)
    f = jnp.concatenate([cm, sm], axis=0).astype(BF16)
    return f, f.T


def _hy_positions(seq):
    t = jnp.linspace(0.0, 1.0, seq, dtype=F32)[:, None]
    w = 2.0 * math.pi * jnp.arange(seq, dtype=F32)[:, None] / seq
    f = jnp.linspace(1e-4, HY_BANDS - 1, HY_BANDS, dtype=F32)[None]
    z = jnp.concatenate([t, jnp.cos(f * w), -jnp.sin(f * w)], axis=-1)
    z = jnp.pad(z, ((0, 0), (0, V7X_LANES - HY_EMB)))
    max_decay = math.log(HY_TARGET) / HY_FAST_DECAY
    min_decay = math.log(HY_TARGET) / HY_SLOW_DECAY
    deltas = jnp.linspace(min_decay, max_decay, HY_W, dtype=F32)
    window = jnp.exp(-t * jnp.abs(deltas))
    return z, window


def _hy_filter_kernel(z_ref, w1_ref, b1_ref, w2_ref, b2_ref, fr_ref, w3_ref, win_ref, h_ref, hid_ref):
    dot = functools.partial(jnp.dot, precision=lax.Precision.HIGHEST, preferred_element_type=F32)

    @pl.when(pl.program_id(0) == 0)
    def _():
        fr = fr_ref[...]
        h = jnp.sin(fr * (dot(z_ref[...], w1_ref[...]) + b1_ref[...]))
        for j in range(HY_INNER):
            h = jnp.sin(fr * (dot(h, w2_ref[j]) + b2_ref[j]))
        hid_ref[...] = h

    h = dot(hid_ref[...], w3_ref[...]) * win_ref[...]
    row = lax.broadcasted_iota(jnp.int32, h.shape, 0)
    backward = pl.program_id(0) >= pl.num_programs(0) // 2
    h_ref[...] = jnp.where((row == 0) & backward, 0.0, h).astype(h_ref.dtype)


def _hy_filters(z, window, w1, b1, w2, b2, freq, w3):
    seq = z.shape[0]
    tn = HY_TC
    per_dir = HY_W // tn
    full = lambda shape: pl.BlockSpec(shape, lambda j: (0,) * len(shape))
    w1p = jnp.pad(w1, ((0, V7X_LANES - HY_EMB), (0, 0)))
    return pl.pallas_call(
        _hy_filter_kernel,
        grid=(2 * per_dir,),
        in_specs=[
            full((seq, V7X_LANES)), full((V7X_LANES, HY_ORDER)), full((1, HY_ORDER)),
            full((HY_INNER, HY_ORDER, HY_ORDER)), full((HY_INNER, 1, HY_ORDER)), full((1, HY_ORDER)),
            pl.BlockSpec((HY_ORDER, tn), lambda j: (0, j)),
            pl.BlockSpec((seq, tn), lambda j: (0, j % per_dir)),
        ],
        out_specs=pl.BlockSpec((seq, tn), lambda j: (0, j)),
        out_shape=jax.ShapeDtypeStruct((seq, 2 * HY_W), BF16),
        scratch_shapes=[pltpu.VMEM((seq, HY_ORDER), F32)],
        compiler_params=_params(("arbitrary",), 32),
        name="hy_filter",
    )(z, w1p, b1.reshape(1, HY_ORDER), w2, b2.reshape(HY_INNER, 1, HY_ORDER), freq.reshape(1, HY_ORDER), w3, window)


def _mm_kernel(a_ref, b_ref, o_ref):
    o_ref[...] = jnp.dot(a_ref[...], b_ref[...], preferred_element_type=F32).astype(o_ref.dtype)


def _mm(a, b, tm, tn, out_dtype):
    m, k = a.shape
    n = b.shape[1]
    return pl.pallas_call(
        _mm_kernel,
        grid=(m // tm, n // tn),
        in_specs=[pl.BlockSpec((tm, k), lambda i, j: (i, 0)), pl.BlockSpec((k, tn), lambda i, j: (0, j))],
        out_specs=pl.BlockSpec((tm, tn), lambda i, j: (i, j)),
        out_shape=jax.ShapeDtypeStruct((m, n), out_dtype),
        compiler_params=_params(("arbitrary", "arbitrary"), 32),
        name="mm",
    )(a, b)


def _hy_spectrum(fmat, hcat, seq):
    spec = _mm(fmat, hcat, min(1024, 2 * seq), 512, F32)
    top, bot = spec[:seq], spec[seq:]
    kr = top[:, :HY_W] + top[:, HY_W:]
    ki = bot[:, :HY_W] - bot[:, HY_W:]
    n = 2 * seq
    first = (jnp.arange(seq) == 0)[:, None]
    pr = jnp.where(first, kr / n, kr * (2.0 / n))
    pi = jnp.where(first, 0.0, ki * (2.0 / n))
    nyq = (bot[:1, :HY_W] + bot[:1, HY_W:]) / n
    return pr, pi, nyq


def _short_conv(x, w_ref, b_ref):
    seq = x.shape[0]
    row = lax.broadcasted_iota(jnp.int32, x.shape, 0)
    prev = jnp.where(row == 0, 0.0, pltpu.roll(x, 1, 0))
    nxt = jnp.where(row == seq - 1, 0.0, pltpu.roll(x, seq - 1, 0))
    return w_ref[0:1, :] * prev + w_ref[1:2, :] * x + w_ref[2:3, :] * nxt + b_ref[...]


def _first_step():
    return (pl.program_id(0) == 0) & (pl.program_id(1) == 0)


def _hy_fwd_kernel(x1_ref, v_ref, w1_ref, b1_ref, wv_ref, bv_ref, f_hbm, pr_ref, pi_ref, nyq_ref, y_ref,
                   f_ref, sem):
    seq = x1_ref.shape[0]
    _load_once(f_hbm, f_ref, sem, _first_step())
    x1 = _short_conv(x1_ref[...].astype(F32), w1_ref, b1_ref)
    v = _short_conv(v_ref[...].astype(F32), wv_ref, bv_ref)
    u = (x1 * v).astype(BF16)
    w = jnp.dot(f_ref[...], u, preferred_element_type=F32)
    a, b = w[:seq], w[seq:]
    pr, pi = pr_ref[...], pi_ref[...]
    row = lax.broadcasted_iota(jnp.int32, pr.shape, 0)
    pd = jnp.where(row == 0, nyq_ref[...], pr)
    y_ref[:seq, :] = (a * pr - b * pi).astype(y_ref.dtype)
    y_ref[seq:, :] = (a * pi + b * pd).astype(y_ref.dtype)


def _hy_inv_kernel(y_ref, x0_ref, x1_ref, v_ref, w0_ref, b0_ref, w1_ref, b1_ref, wv_ref, bv_ref, skip_ref,
                   g_hbm, o_ref, g_ref, sem):
    _load_once(g_hbm, g_ref, sem, _first_step())
    y = jnp.dot(g_ref[...], y_ref[...], preferred_element_type=F32)
    x0 = _short_conv(x0_ref[...].astype(F32), w0_ref, b0_ref)
    x1 = _short_conv(x1_ref[...].astype(F32), w1_ref, b1_ref)
    v = _short_conv(v_ref[...].astype(F32), wv_ref, bv_ref)
    u = x1 * v
    o_ref[...] = (x0 * (y + u * skip_ref[...])).astype(o_ref.dtype)


def _hyena(proj, conv_w, conv_b, skip, fmat, gmat, pr, pi, nyq, batch, seq):
    t = batch * seq
    tc = HY_TC // 2
    nct = HY_W // tc
    conv_b2 = conv_b.reshape(1, 3 * HY_W)

    def xcol(part):
        return pl.BlockSpec((seq, tc), lambda c, b: (b, (OFF_HY + part * HY_W) // tc + c))

    def wcol(part):
        return pl.BlockSpec((HY_SHORT, tc), lambda c, b: (0, part * nct + c))

    def bcol(part):
        return pl.BlockSpec((1, tc), lambda c, b: (0, part * nct + c))

    chan = pl.BlockSpec((seq, tc), lambda c, b: (0, c))
    chan1 = pl.BlockSpec((1, tc), lambda c, b: (0, c))
    whole = pl.BlockSpec(memory_space=pl.ANY)
    dft_scratch = [pltpu.VMEM(fmat.shape, BF16), pltpu.SemaphoreType.DMA(())]
    yspec = pl.BlockSpec((None, 2 * seq, tc), lambda c, b: (b, 0, c))
    yfreq = pl.pallas_call(
        _hy_fwd_kernel,
        grid=(nct, batch),
        in_specs=[xcol(1), xcol(2), wcol(1), bcol(1), wcol(2), bcol(2), whole, chan, chan, chan1],
        out_specs=yspec,
        out_shape=jax.ShapeDtypeStruct((batch, 2 * seq, HY_W), BF16),
        scratch_shapes=dft_scratch,
        compiler_params=_params(("arbitrary", "arbitrary"), 56),
        name="hy_fwd",
    )(proj, proj, conv_w, conv_b2, conv_w, conv_b2, fmat, pr, pi, nyq)
    return pl.pallas_call(
        _hy_inv_kernel,
        grid=(nct, batch),
        in_specs=[yspec, xcol(0), xcol(1), xcol(2), wcol(0), bcol(0), wcol(1), bcol(1), wcol(2), bcol(2), chan1,
                  whole],
        out_specs=pl.BlockSpec((seq, tc), lambda c, b: (b, c)),
        out_shape=jax.ShapeDtypeStruct((t, HY_W), BF16),
        scratch_shapes=[pltpu.VMEM(gmat.shape, BF16), pltpu.SemaphoreType.DMA(())],
        compiler_params=_params(("arbitrary", "arbitrary"), 56),
        name="hy_inv",
    )(yfreq, proj, proj, proj, conv_w, conv_b2, conv_w, conv_b2, conv_w, conv_b2, skip.reshape(1, HY_W), gmat)


def _branch_kernel(ohg_ref, ohy_ref, oat_ref, gates_ref, wb_hbm, m_ref, wb_ref, sem):
    d = D_MODEL
    _load_once(wb_hbm, wb_ref, sem, pl.program_id(0) == 0)
    m = None
    for n, o_ref in enumerate((ohg_ref, ohy_ref, oat_ref)):
        br = jnp.dot(o_ref[...], wb_ref[n], preferred_element_type=F32)
        term = jax.nn.sigmoid(gates_ref[:, n * d:(n + 1) * d].astype(F32)) * br
        m = term if m is None else m + term
    m_ref[...] = m.astype(m_ref.dtype)


def _merge_kernel(m_ref, h_ref, wo_hbm, lng_ref, lnb_ref, rwh_ref, rwl_ref, rb_ref,
                  hmid_ref, hp_ref, idx_ref, gate_ref, wo_ref, sem):
    _load_once(wo_hbm, wo_ref, sem, pl.program_id(0) == 0)
    y = jnp.dot(m_ref[...], wo_ref[...], preferred_element_type=F32)
    hn = _layer_norm_rows(DEEPNORM_ALPHA * h_ref[...] + y, lng_ref[...], lnb_ref[...])
    hmid_ref[...] = hn
    hp_ref[...] = _pack_halves(hn)

    h_hi = hn.astype(BF16)
    h_lo = (hn - h_hi.astype(F32)).astype(BF16)
    logits = (jnp.dot(h_hi, rwh_ref[...], preferred_element_type=F32)
              + jnp.dot(h_lo, rwh_ref[...], preferred_element_type=F32)
              + jnp.dot(h_hi, rwl_ref[...], preferred_element_type=F32)) + rb_ref[...]
    lane = lax.broadcasted_iota(jnp.int32, logits.shape, 1)
    vals, idxs = [], []
    for _ in range(TOP_K):
        mx = jnp.max(logits, axis=-1, keepdims=True)
        ix = jnp.min(jnp.where(logits == mx, lane, ROUTE_LANES), axis=-1, keepdims=True)
        vals.append(mx)
        idxs.append(ix)
        logits = jnp.where(lane == ix, -jnp.inf, logits)
    exps = [jnp.exp(v - vals[0]) for v in vals]
    total = exps[0]
    for e in exps[1:]:
        total = total + e
    gate_out = jnp.zeros(logits.shape, F32)
    idx_out = jnp.zeros(logits.shape, jnp.int32)
    for r in range(TOP_K):
        gate_out = jnp.where(lane == r, exps[r] / total, gate_out)
        idx_out = jnp.where(lane == r, idxs[r], idx_out)
    gate_ref[...] = gate_out
    idx_ref[...] = idx_out


def _merge(o_hg, o_hy, o_at, gates, h, wb, wo, ln_g, ln_b, rw, rb):
    t, d = h.shape
    tm = min(MERGE_TM, t)
    row = lambda width: pl.BlockSpec((tm, width), lambda i: (i, 0))
    const = lambda shape: pl.BlockSpec(shape, lambda i: (0,) * len(shape))
    whole = pl.BlockSpec(memory_space=pl.ANY)
    rwp = jnp.pad(rw, ((0, 0), (0, ROUTE_LANES - N_EXPERTS)))
    rw_hi = rwp.astype(BF16)
    rw_lo = (rwp - rw_hi.astype(F32)).astype(BF16)
    rbp = jnp.pad(rb, (0, ROUTE_LANES - N_EXPERTS), constant_values=MASK_VALUE).reshape(1, ROUTE_LANES)
    m = pl.pallas_call(
        _branch_kernel,
        grid=(t // tm,),
        in_specs=[row(MIX_W), row(MIX_W), row(MIX_W), row(N_BRANCH * d), whole],
        out_specs=row(d),
        out_shape=jax.ShapeDtypeStruct((t, d), BF16),
        scratch_shapes=[pltpu.VMEM((N_BRANCH, MIX_W, d), BF16), pltpu.SemaphoreType.DMA(())],
        compiler_params=_params(("arbitrary",), 56),
        name="branch_merge",
    )(o_hg, o_hy, o_at, gates, wb)
    return pl.pallas_call(
        _merge_kernel,
        grid=(t // tm,),
        in_specs=[row(d), row(d), whole, const((1, d)), const((1, d)),
                  const((d, ROUTE_LANES)), const((d, ROUTE_LANES)), const((1, ROUTE_LANES))],
        out_specs=[row(d), row(d // 2), row(ROUTE_LANES), row(ROUTE_LANES)],
        out_shape=[jax.ShapeDtypeStruct((t, d), F32), jax.ShapeDtypeStruct((t, d // 2), jnp.uint32),
                   jax.ShapeDtypeStruct((t, ROUTE_LANES), jnp.int32), jax.ShapeDtypeStruct((t, ROUTE_LANES), F32)],
        scratch_shapes=[pltpu.VMEM((d, d), BF16), pltpu.SemaphoreType.DMA(())],
        compiler_params=_params(("arbitrary",), 56),
        name="merge",
    )(m, h, wo, ln_g.reshape(1, d), ln_b.reshape(1, d), rw_hi, rw_lo, rbp)


def _route_plan(top_idx, n_tiles):
    e = top_idx.reshape(-1)
    onehot = (e[:, None] == jnp.arange(N_EXPERTS, dtype=jnp.int32)[None, :]).astype(jnp.int32)
    csum = jnp.cumsum(onehot, axis=0)
    rank = jnp.sum(csum * onehot, axis=1) - 1
    counts = csum[-1]
    padded = (counts + MOE_TM - 1) // MOE_TM * MOE_TM
    p_end = jnp.cumsum(padded)
    p_start = p_end - padded
    pos = (p_start[e] + rank).astype(jnp.int32)
    n_used = (p_end[-1] // MOE_TM).astype(jnp.int32)
    tile_start = jnp.arange(n_tiles, dtype=jnp.int32) * MOE_TM
    tile_expert = jnp.minimum(jnp.searchsorted(p_end, tile_start, side='right'), N_EXPERTS - 1).astype(jnp.int32)
    tile_rows = jnp.clip(counts[tile_expert] - (tile_start - p_start[tile_expert]), 0, MOE_TM).astype(jnp.int32)
    return pos, tile_expert, tile_rows, n_used.reshape(1)


def _dispatch_kernel(pos_ref, hp_ref, xs_ref, sem):
    def body(r, carry):
        for k in range(TOP_K):
            dst = pos_ref[0, r * TOP_K + k]
            pltpu.make_async_copy(hp_ref.at[pl.ds(r, 1), :], xs_ref.at[pl.ds(dst, 1), :], sem).start(priority=k % 2)
        return carry

    lax.fori_loop(0, hp_ref.shape[0], body, 0)
    for k in range(TOP_K):
        pltpu.make_async_copy(hp_ref, xs_ref.at[pl.ds(0, hp_ref.shape[0]), :], sem).wait()


def _dispatch(hp, pos, n_slots):
    t, dw = hp.shape
    tq = min(TOK_TILE, t)
    return pl.pallas_call(
        _dispatch_kernel,
        grid=(t // tq,),
        in_specs=[pl.BlockSpec((None, 1, tq * TOP_K), lambda i: (i, 0, 0), memory_space=pltpu.SMEM),
                  pl.BlockSpec((tq, dw), lambda i: (i, 0))],
        out_specs=pl.BlockSpec(memory_space=pl.ANY),
        out_shape=jax.ShapeDtypeStruct((n_slots, dw), jnp.uint32),
        scratch_shapes=[pltpu.SemaphoreType.DMA(())],
        compiler_params=_params(("arbitrary",), 32),
        name="moe_dispatch",
    )(pos.reshape(t // tq, 1, tq * TOP_K), hp)


def _ffn_kernel(te_ref, tr_ref, nu_ref, x_ref, wg_ref, wu_ref, bgu_ref, wdl_ref, wdh_ref, bd_ref,
                y_ref, xb_ref, act_ref):
    i = pl.program_id(0)
    j = pl.program_id(1)
    rows = tr_ref[i]
    active = i < nu_ref[0]
    half = x_ref.shape[1]
    nf = D_FF // MOE_TF
    nsub = (rows + MOE_SUB - 1) // MOE_SUB

    @pl.when(active & (j == 0))
    def _():
        lo, hi = _unpack_halves(x_ref[...])
        keep = lax.broadcasted_iota(jnp.int32, lo.shape, 0) < rows
        xb_ref[:, :half] = jnp.where(keep, lo, 0.0).astype(BF16)
        xb_ref[:, half:] = jnp.where(keep, hi, 0.0).astype(BF16)

    fcol = pl.multiple_of(jnp.minimum(j, nf - 1) * MOE_TF, MOE_TF)
    ncol = pl.multiple_of(jnp.maximum(j - nf, 0) * MOE_TN, MOE_TN)

    for k in range(1, MOE_TM // MOE_SUB + 1):
        m = k * MOE_SUB

        @pl.when(active & (j < nf) & (nsub == k))
        def _():
            xs = xb_ref[:m, :]
            g = jnp.dot(xs, wg_ref[...].astype(BF16), preferred_element_type=F32) + bgu_ref[:, pl.ds(fcol, MOE_TF)]
            u = (jnp.dot(xs, wu_ref[...].astype(BF16), preferred_element_type=F32)
                 + bgu_ref[:, pl.ds(D_FF + fcol, MOE_TF)])
            g = jnp.minimum(g, SWIGLU_LIMIT)
            u = jnp.clip(u, -SWIGLU_LIMIT, SWIGLU_LIMIT)
            act_ref[:m, pl.ds(fcol, MOE_TF)] = ((u + 1.0) * g * jax.nn.sigmoid(SWIGLU_ALPHA * g)).astype(BF16)

        @pl.when(active & (j >= nf) & (nsub == k))
        def _():
            act = act_ref[:m, :]
            lo = (jnp.dot(act, wdl_ref[...].astype(BF16), preferred_element_type=F32)
                  + bd_ref[:, pl.ds(ncol, MOE_TN)])
            hi = (jnp.dot(act, wdh_ref[...].astype(BF16), preferred_element_type=F32)
                  + bd_ref[:, pl.ds(half + ncol, MOE_TN)])
            y_ref[:m, :] = _pack_pair(lo, hi)
            if m < MOE_TM:
                y_ref[m:, :] = jnp.zeros((MOE_TM - m, y_ref.shape[1]), jnp.uint32)


def _ffn(xs, w_gate_up, b_gate_up, w_down, b_down, layer, tile_expert, tile_rows, n_used):
    n_slots, dw = xs.shape
    d = 2 * dw
    n_tiles = n_slots // MOE_TM
    nf = D_FF // MOE_TF
    nn = dw // MOE_TN
    bgu = b_gate_up.reshape(DEPTH, N_EXPERTS, 1, 2 * D_FF)
    bd = b_down.reshape(DEPTH, N_EXPERTS, 1, d)

    def tile(i, nu):
        return jnp.minimum(i, nu[0] - 1)

    def fcol(i, j, nu):
        return jnp.where(i < nu[0], jnp.minimum(j, nf - 1), nf - 1)

    def ncol(i, j, nu):
        return jnp.where(i < nu[0], jnp.maximum(j - nf, 0), nn - 1)

    xmap = lambda i, j, te, tr, nu: (tile(i, nu), 0)
    gmap = lambda i, j, te, tr, nu: (layer, te[tile(i, nu)], 0, fcol(i, j, nu))
    umap = lambda i, j, te, tr, nu: (layer, te[tile(i, nu)], 0, nf + fcol(i, j, nu))
    dlmap = lambda i, j, te, tr, nu: (layer, te[tile(i, nu)], 0, ncol(i, j, nu))
    dhmap = lambda i, j, te, tr, nu: (layer, te[tile(i, nu)], 0, nn + ncol(i, j, nu))
    bmap = lambda i, j, te, tr, nu: (layer, te[tile(i, nu)], 0, 0)
    ymap = lambda i, j, te, tr, nu: (tile(i, nu), ncol(i, j, nu))
    grid_spec = pltpu.PrefetchScalarGridSpec(
        num_scalar_prefetch=3,
        grid=(n_tiles, nf + nn),
        in_specs=[
            pl.BlockSpec((MOE_TM, dw), xmap, pipeline_mode=pl.Buffered(1)),
            pl.BlockSpec((None, None, d, MOE_TF), gmap),
            pl.BlockSpec((None, None, d, MOE_TF), umap),
            pl.BlockSpec((None, None, 1, 2 * D_FF), bmap),
            pl.BlockSpec((None, None, D_FF, MOE_TN), dlmap),
            pl.BlockSpec((None, None, D_FF, MOE_TN), dhmap),
            pl.BlockSpec((None, None, 1, d), bmap),
        ],
        out_specs=pl.BlockSpec((MOE_TM, MOE_TN), ymap),
        scratch_shapes=[pltpu.VMEM((MOE_TM, d), BF16), pltpu.VMEM((MOE_TM, D_FF), BF16)],
    )
    return pl.pallas_call(
        _ffn_kernel,
        grid_spec=grid_spec,
        out_shape=jax.ShapeDtypeStruct((n_slots, dw), jnp.uint32),
        compiler_params=_params(("arbitrary", "arbitrary"), 56),
        name="moe_ffn",
    )(tile_expert, tile_rows, n_used, xs, w_gate_up, w_gate_up, bgu, w_down, w_down, bd)


def _combine_kernel(pos_ref, posn_ref, ys_ref, gate_ref, h_ref, lng_ref, lnb_ref, hout_ref, hb_ref, buf, sem):
    i = pl.program_id(0)
    tq, d = h_ref.shape
    half = d // 2
    slot = i % 2

    def issue(p_ref, s):
        def body(r, carry):
            for k in range(TOP_K):
                src = p_ref[0, r * TOP_K + k]
                pltpu.make_async_copy(ys_ref.at[pl.ds(src, 1), :], buf.at[s, k, pl.ds(r, 1), :],
                                      sem.at[s]).start(priority=k % 2)
            return carry

        lax.fori_loop(0, tq, body, 0, unroll=4)

    @pl.when(i == 0)
    def _():
        issue(pos_ref, 0)

    for s in range(2):
        @pl.when((i + 1 < pl.num_programs(0)) & (slot == 1 - s))
        def _():
            issue(posn_ref, s)

    for k in range(TOP_K):
        pltpu.make_async_copy(ys_ref.at[pl.ds(0, tq), :], buf.at[slot, k], sem.at[slot]).wait()
    gate = gate_ref[...]
    acc_lo = DEEPNORM_ALPHA * h_ref[:, :half]
    acc_hi = DEEPNORM_ALPHA * h_ref[:, half:]
    for k in range(TOP_K):
        lo, hi = _unpack_halves(buf[slot, k])
        acc_lo = acc_lo + gate[:, k:k + 1] * lo
        acc_hi = acc_hi + gate[:, k:k + 1] * hi
    mu = (jnp.sum(acc_lo, axis=-1, keepdims=True) + jnp.sum(acc_hi, axis=-1, keepdims=True)) / d
    acc_lo = acc_lo - mu
    acc_hi = acc_hi - mu
    var = (jnp.sum(acc_lo * acc_lo, axis=-1, keepdims=True) + jnp.sum(acc_hi * acc_hi, axis=-1, keepdims=True)) / d
    inv = lax.rsqrt(var + LN_EPS)
    for sl, acc in ((slice(0, half), acc_lo), (slice(half, d), acc_hi)):
        hn = acc * inv * lng_ref[:, sl] + lnb_ref[:, sl]
        hout_ref[:, sl] = hn
        hb_ref[:, sl] = hn.astype(BF16)


def _combine(ys, pos, gate, h, ln_g, ln_b):
    t, d = h.shape
    tq = min(TOK_TILE, t)
    nt = t // tq
    row = lambda width: pl.BlockSpec((tq, width), lambda i: (i, 0))
    vec = pl.BlockSpec((1, d), lambda i: (0, 0))
    pos3 = pos.reshape(nt, 1, tq * TOP_K)
    return pl.pallas_call(
        _combine_kernel,
        grid=(nt,),
        in_specs=[pl.BlockSpec((None, 1, tq * TOP_K), lambda i: (i, 0, 0), memory_space=pltpu.SMEM),
                  pl.BlockSpec((None, 1, tq * TOP_K), lambda i: (jnp.minimum(i + 1, nt - 1), 0, 0),
                               memory_space=pltpu.SMEM),
                  pl.BlockSpec(memory_space=pl.ANY), row(ROUTE_LANES), row(d), vec, vec],
        out_specs=[row(d), row(d)],
        out_shape=[jax.ShapeDtypeStruct((t, d), F32), jax.ShapeDtypeStruct((t, d), BF16)],
        scratch_shapes=[pltpu.VMEM((2, TOP_K, tq, d // 2), jnp.uint32), pltpu.SemaphoreType.DMA((2,))],
        compiler_params=_params(("arbitrary",), 40),
        name="moe_combine",
    )(pos3, pos3, ys, gate, h, ln_g.reshape(1, d), ln_b.reshape(1, d))


def _hgrn_lower_bound(lb_table, layer):
    p = jax.nn.softmax(lb_table.astype(F32), axis=0)
    return jnp.cumsum(p, axis=0)[layer] - p[0]


def kernel(x, ln_in_g, ln_in_b, w_in, hg_lower_bound, hg_norm_g, hy_conv_w, hy_conv_b, hy_filt_w1, hy_filt_b1,
           hy_filt_w2, hy_filt_b2, hy_filt_freq, hy_filt_w3, hy_skip, att_sink, rel_bias, w_branch, w_out,
           ln_mix_g, ln_mix_b, router_w, router_b, w_gate_up, b_gate_up, w_down, b_down, ln_moe_g, ln_moe_b):
    batch, seq, d = x.shape
    t = batch * seq
    n_tiles = t * TOP_K // MOE_TM + N_EXPERTS
    n_slots = n_tiles * MOE_TM

    fmat, gmat = _dft_tables(seq)
    z_pos, window = _hy_positions(seq)
    bias_tab = _attn_bias_table(rel_bias)

    h, hb = _ln_in(x.reshape(t, d), ln_in_g, ln_in_b)
    for layer in range(DEPTH):
        proj = _inproj(hb, w_in, layer, 0, OFF_AK, PROJ_TN)
        proj_kv = _inproj(hb, w_in, layer, OFF_AK, PROJ_KV_COLS, PROJ_KV_COLS)
        w_gates = lax.slice(w_in, (layer, 0, OFF_GATES), (layer + 1, d, IN_COLS))
        gates = _inproj(hb, w_gates, 0, 0, N_BRANCH * d, PROJ_TN)

        lb = _hgrn_lower_bound(hg_lower_bound, layer)
        o_hg = _hgrn(proj, lb, hg_norm_g[layer], batch, seq)

        hcat = _hy_filters(z_pos, window, hy_filt_w1[layer], hy_filt_b1[layer], hy_filt_w2[layer],
                           hy_filt_b2[layer], hy_filt_freq[layer], hy_filt_w3[layer])
        pr, pi, nyq = _hy_spectrum(fmat, hcat, seq)
        o_hy = _hyena(proj, hy_conv_w[layer], hy_conv_b[layer], hy_skip[layer], fmat, gmat, pr, pi, nyq,
                      batch, seq)

        o_at = _attn(proj, proj_kv, att_sink[layer], bias_tab, batch, seq)

        h_mid, hp, top_idx, gate = _merge(
            o_hg, o_hy, o_at, gates, h, w_branch[layer].astype(BF16), w_out[layer].astype(BF16),
            ln_mix_g[layer], ln_mix_b[layer], router_w[layer], router_b[layer])

        pos, tile_expert, tile_rows, n_used = _route_plan(top_idx[:, :TOP_K], n_tiles)
        xs = _dispatch(hp, pos, n_slots)
        ys = _ffn(xs, w_gate_up, b_gate_up, w_down, b_down, layer, tile_expert, tile_rows, n_used)
        h, hb = _combine(ys, pos, gate, h_mid, ln_moe_g[layer], ln_moe_b[layer])
    return h.reshape(batch, seq, d)
```

```python
import functools
import math

import jax
import jax.numpy as jnp
from jax import lax
from jax.experimental import pallas as pl
from jax.experimental.pallas import tpu as pltpu

F32 = jnp.float32
BF16 = jnp.bfloat16

D_MODEL = 2048
DEPTH = 2
MIX_W = 1024
N_BRANCH = 3
HG_HEADS = 8
HG_DK = 128
HG_DV = 128
HG_CHUNK = 16
F_FLOOR = 1e-30
HY_W = 1024
HY_SHORT = 3
HY_EMB = 33
HY_BANDS = (HY_EMB - 1) // 2
HY_ORDER = 64
HY_INNER = 2
HY_FAST_DECAY = 0.3
HY_SLOW_DECAY = 1.5
HY_TARGET = 1e-2
ATT_HEADS = 16
ATT_KV_HEADS = 2
ATT_DH = 64
WINDOW = 128
ATT_BLOCK = 128
REL_BUCKETS = 32
REL_MAX_DIST = 128
MASK_VALUE = -1e30
N_EXPERTS = 32
TOP_K = 4
D_FF = 2048
SWIGLU_ALPHA = 1.702
SWIGLU_LIMIT = 7.0
LN_EPS = 1e-5
RMS_EPS = 1e-6
DEEPNORM_ALPHA = (2 * DEPTH) ** 0.25

HG_KW = HG_HEADS * HG_DK
HG_VW = HG_HEADS * HG_DV
ATT_QW = ATT_HEADS * ATT_DH
ATT_KVW = ATT_KV_HEADS * ATT_DH
IN_SIZES = (HG_KW, HG_KW, HG_KW, HG_VW, HG_VW, 3 * HY_W, ATT_QW, ATT_KVW, ATT_KVW, N_BRANCH * D_MODEL)
IN_COLS = sum(IN_SIZES)
OFF_HQ, OFF_HFF, OFF_HFB, OFF_HI, OFF_HOG = 0, HG_KW, 2 * HG_KW, 3 * HG_KW, 3 * HG_KW + HG_VW
OFF_HY = OFF_HOG + HG_VW
OFF_AQ = OFF_HY + 3 * HY_W
OFF_AK = OFF_AQ + ATT_QW
OFF_AV = OFF_AK + ATT_KVW
OFF_GATES = OFF_AV + ATT_KVW
MAIN_COLS = OFF_GATES

V7X_LANES = 128
V7X_VMEM_BYTES = 64 * 1024 * 1024

LN_ROWS = 512
PROJ_TM = 1024
PROJ_TN = 1024
PROJ_KV_COLS = 2 * ATT_KVW
HY_TC = 512
MERGE_TM = 512
MOE_TM = 1024
MOE_SUB = 256
MOE_TF = 512
MOE_TN = 512
TOK_TILE = 256
ROUTE_LANES = 128


def _params(semantics, vmem_mb):
    return pltpu.CompilerParams(dimension_semantics=semantics, vmem_limit_bytes=vmem_mb * 1024 * 1024)


def _layer_norm_rows(x, g, b):
    mu = jnp.mean(x, axis=-1, keepdims=True)
    xc = x - mu
    var = jnp.mean(xc * xc, axis=-1, keepdims=True)
    return xc * lax.rsqrt(var + LN_EPS) * g + b


def _pack_halves(x):
    c = x.shape[-1] // 2
    return _pack_pair(x[:, :c], x[:, c:])


def _pack_pair(lo, hi):
    lo = pltpu.bitcast(lo.astype(BF16).astype(F32), jnp.uint32)
    hi = pltpu.bitcast(hi.astype(BF16).astype(F32), jnp.uint32)
    return (lo >> 16) | (hi & jnp.uint32(0xFFFF0000))


def _unpack_halves(w):
    lo = pltpu.bitcast(w << 16, F32)
    hi = pltpu.bitcast(w & jnp.uint32(0xFFFF0000), F32)
    return lo, hi


def _ln_in_kernel(x_ref, g_ref, b_ref, h_ref, hb_ref):
    y = _layer_norm_rows(x_ref[...], g_ref[...], b_ref[...])
    h_ref[...] = y
    hb_ref[...] = y.astype(BF16)


def _ln_in(x2, g, b):
    t, d = x2.shape
    tm = min(LN_ROWS, t)
    row = pl.BlockSpec((tm, d), lambda i: (i, 0))
    vec = pl.BlockSpec((1, d), lambda i: (0, 0))
    return pl.pallas_call(
        _ln_in_kernel,
        grid=(t // tm,),
        in_specs=[row, vec, vec],
        out_specs=[row, row],
        out_shape=[jax.ShapeDtypeStruct((t, d), F32), jax.ShapeDtypeStruct((t, d), BF16)],
        compiler_params=_params(("arbitrary",), 32),
        name="ln_in",
    )(x2, g.reshape(1, d), b.reshape(1, d))


def _inproj_kernel(a_ref, w_ref, o_ref, wb_ref):
    @pl.when(pl.program_id(1) == 0)
    def _():
        wb_ref[...] = w_ref[...].astype(BF16)

    o_ref[...] = jnp.dot(a_ref[...], wb_ref[...], preferred_element_type=F32).astype(o_ref.dtype)


def _inproj(hb, w_in, layer, col0, ncols, tn):
    t, d = hb.shape
    tm = min(PROJ_TM, t)
    cb0 = col0 // tn
    return pl.pallas_call(
        _inproj_kernel,
        grid=(ncols // tn, t // tm),
        in_specs=[
            pl.BlockSpec((tm, d), lambda j, i: (i, 0)),
            pl.BlockSpec((None, d, tn), lambda j, i: (layer, 0, cb0 + j)),
        ],
        out_specs=pl.BlockSpec((tm, tn), lambda j, i: (i, j)),
        out_shape=jax.ShapeDtypeStruct((t, ncols), BF16),
        scratch_shapes=[pltpu.VMEM((d, tn), BF16)],
        compiler_params=_params(("arbitrary", "arbitrary"), 48),
        name="inproj",
    )(hb, w_in)


HG_BLK = 128
HG_CPB = HG_BLK // HG_CHUNK
HG_UNROLL = 128
LOG2E = 1.4426950408889634


def _hgrn_perm():
    r = jnp.arange(HG_BLK)
    src = (r % HG_CPB) * HG_CHUNK + r // HG_CPB
    return (src[:, None] == jnp.arange(HG_BLK)[None, :]).astype(BF16)


def _hgrn_kernel(q_ref, ff_ref, fb_ref, i_ref, og_ref, lbf_ref, lbb_ref, g_ref, p_ref, pt_ref, o_ref,
                 x_t, kk_t, o_t, tmp_t, qtil_c, ktil_c, kv_s, st_s, oint_c, dec_c, *, seq):
    c = HG_CHUNK
    nc = seq // c
    nb = seq // HG_BLK
    nbh = max(nb // 2, 1)
    n_half = nb // nbh
    perm = p_ref[...]
    perm_t = pt_ref[...]

    for a, ref in enumerate((q_ref, ff_ref, fb_ref, i_ref)):
        for r in range(nb):
            xp = jnp.dot(perm, ref[r * HG_BLK:(r + 1) * HG_BLK, :], preferred_element_type=F32)
            x_t[a, r] = xp.reshape(c, HG_CPB, HG_DK)

    lbs = (lbf_ref[...], lbb_ref[...])
    for j in range(c):
        qj = x_t[0, :, j]
        x_t[0, :, j] = qj * jax.nn.sigmoid(qj) * (HG_DK ** -0.5)
        for d in range(2):
            z = x_t[1 + d, :, j]
            lb = lbs[d]
            e = jnp.exp(-jnp.abs(z))
            r_ = 1.0 / (1.0 + e)
            er = e * r_
            sig = jnp.where(z >= 0, r_, er)
            nsig = jnp.where(z >= 0, er, r_)
            f = lb + (1.0 - lb) * sig
            x_t[1 + d, :, j] = jnp.log(jnp.maximum(f, F_FLOOR)) * LOG2E
            kk_t[d, :, j] = (1.0 - lb) * nsig

    for d in range(2):
        acc = jnp.zeros((nb, HG_CPB, HG_DK), F32)
        for j in (range(c) if d == 0 else reversed(range(c))):
            acc = acc + x_t[1 + d, :, j]
            x_t[1 + d, :, j] = acc
        last = acc
        dec_c[d] = jnp.exp2(last).reshape(nc, HG_DK)
        for j in range(c):
            cj = x_t[1 + d, :, j]
            tmp_t[0, :, j] = x_t[0, :, j] * jnp.exp2(cj)
            tmp_t[1, :, j] = kk_t[d, :, j] * jnp.exp2(last - cj)
        for r in range(nb):
            rows = slice(r * HG_BLK, (r + 1) * HG_BLK)
            qb = tmp_t[0, r].reshape(HG_BLK, HG_DK).astype(BF16)
            kb = tmp_t[1, r].reshape(HG_BLK, HG_DK).astype(BF16)
            qtil_c[d, rows, :] = jnp.dot(perm_t, qb, preferred_element_type=F32).astype(BF16)
            ktil_c[rows, d * HG_DK:(d + 1) * HG_DK] = jnp.dot(perm_t, kb, preferred_element_type=F32).astype(BF16)

    o_t[...] = jnp.zeros(o_t.shape, F32)
    ones_b = jnp.ones((HG_DK, HG_DV), BF16)

    def pair_body(it, carry):
        d = it // n_half
        sl = pl.ds((it % n_half) * nbh, nbh)
        for t in range(c):
            rt = t + d * (c - 1 - 2 * t)
            ct = x_t[1 + d, sl, rt]
            qt = x_t[0, sl, rt]
            acc = jnp.zeros((nbh * HG_CPB, HG_DV), F32)
            for s in range(t + 1):
                rs = s + d * (c - 1 - 2 * s)
                e = jnp.exp2(jnp.minimum(ct - x_t[1 + d, sl, rs], 0.0))
                a = (qt * e * kk_t[d, sl, rs]).reshape(nbh * HG_CPB, HG_DK).astype(BF16)
                p = jnp.dot(a, ones_b, preferred_element_type=F32)
                acc = acc + p * x_t[3, sl, rs].reshape(nbh * HG_CPB, HG_DV)
            o_t[sl, rt] += acc.reshape(nbh, HG_CPB, HG_DV)
        return carry

    lax.fori_loop(0, 2 * n_half, pair_body, 0)

    def kv_body(n, carry):
        r0 = pl.multiple_of(n * c, c)
        kv = lax.dot_general(i_ref[pl.ds(r0, c), :], ktil_c[pl.ds(r0, c), :], (((0,), (0,)), ((), ())),
                             preferred_element_type=F32)
        kv_s[0, n] = kv[:, :HG_DK]
        kv_s[1, n] = kv[:, HG_DK:]
        return carry

    lax.fori_loop(0, nc, kv_body, 0, unroll=min(HG_UNROLL, nc))

    def chain_body(idx, carry):
        new = []
        for d in range(2):
            n = idx if d == 0 else nc - 1 - idx
            s = carry[d]
            new.append(dec_c[d, pl.ds(n, 1), :] * s + kv_s[d, n])
            st_s[n, d * HG_DV:(d + 1) * HG_DV, :] = s.astype(BF16)
        return tuple(new)

    zero_state = jnp.zeros((HG_DV, HG_DK), F32)
    lax.fori_loop(0, nc, chain_body, (zero_state, zero_state), unroll=8)

    def out_body(n, carry):
        r0 = pl.multiple_of(n * c, c)
        lhs = jnp.concatenate([qtil_c[0, pl.ds(r0, c), :], qtil_c[1, pl.ds(r0, c), :]], axis=0)
        out = lax.dot_general(lhs, st_s[n], (((1,), (1,)), ((), ())), preferred_element_type=F32)
        oint_c[0, pl.ds(r0, c), :] = out[:c, :HG_DV]
        oint_c[1, pl.ds(r0, c), :] = out[c:, HG_DV:]
        return carry

    lax.fori_loop(0, nc, out_body, 0, unroll=min(HG_UNROLL, nc))

    for r in range(nb):
        rows = slice(r * HG_BLK, (r + 1) * HG_BLK)
        ob = o_t[r].reshape(HG_BLK, HG_DV)
        hi = ob.astype(BF16)
        lo = (ob - hi.astype(F32)).astype(BF16)
        oc = jnp.dot(perm_t, hi, preferred_element_type=F32) + jnp.dot(perm_t, lo, preferred_element_type=F32)
        o = oc + oint_c[0, rows, :] + oint_c[1, rows, :]
        o = o * lax.rsqrt(jnp.mean(o * o, axis=-1, keepdims=True) + RMS_EPS) * g_ref[...]
        og = og_ref[rows, :].astype(F32)
        o_ref[rows, :] = (o * (og * jax.nn.sigmoid(og))).astype(o_ref.dtype)


def _hgrn(proj, lb, norm_g, batch, seq):
    t = batch * seq
    nc = seq // HG_CHUNK

    def col(off):
        return pl.BlockSpec((seq, HG_DK), lambda b, h: (b, off // HG_DK + h))

    vec = pl.BlockSpec((None, 1, HG_DK), lambda b, h: (h, 0, 0))
    lbf = lb[:HG_KW].reshape(HG_HEADS, 1, HG_DK)
    lbb = lb[HG_KW:].reshape(HG_HEADS, 1, HG_DK)
    g = norm_g.reshape(HG_HEADS, 1, HG_DV)
    slab = (seq // HG_BLK, HG_CHUNK, HG_CPB, HG_DK)
    perm = _hgrn_perm()
    pspec = pl.BlockSpec((HG_BLK, HG_BLK), lambda b, h: (0, 0))
    return pl.pallas_call(
        functools.partial(_hgrn_kernel, seq=seq),
        grid=(batch, HG_HEADS),
        in_specs=[col(OFF_HQ), col(OFF_HFF), col(OFF_HFB), col(OFF_HI), col(OFF_HOG), vec, vec, vec, pspec, pspec],
        out_specs=pl.BlockSpec((seq, HG_DV), lambda b, h: (b, h)),
        out_shape=jax.ShapeDtypeStruct((t, HG_VW), BF16),
        scratch_shapes=[
            pltpu.VMEM((4,) + slab, F32),
            pltpu.VMEM((2,) + slab, F32),
            pltpu.VMEM(slab, F32),
            pltpu.VMEM((2,) + slab, F32),
            pltpu.VMEM((2, seq, HG_DK), BF16),
            pltpu.VMEM((seq, 2 * HG_DK), BF16),
            pltpu.VMEM((2, nc, HG_DV, HG_DK), F32),
            pltpu.VMEM((nc, 2 * HG_DV, HG_DK), BF16),
            pltpu.VMEM((2, seq, HG_DV), F32),
            pltpu.VMEM((2, nc, HG_DK), F32),
        ],
        compiler_params=_params(("arbitrary", "arbitrary"), 48),
        name="hgrn2",
    )(proj, proj, proj, proj, proj, lbf, lbb, g, perm, perm.T)


def _attn_kernel(sink_ref, q_ref, kp_ref, ko_ref, kn_ref, vp_ref, vo_ref, vn_ref, bias_ref, o_ref, s_ref, p_ref):
    w = ATT_BLOCK
    group = ATT_HEADS // ATT_KV_HEADS
    kband = jnp.concatenate([kp_ref[...], ko_ref[...], kn_ref[...]], axis=0)
    vband = jnp.concatenate([vp_ref[...], vo_ref[...], vn_ref[...]], axis=0)
    ones = jnp.ones((3 * w, ATT_DH), BF16)
    q = q_ref[...] * jnp.asarray(ATT_DH ** -0.5, BF16)
    kgs = [kband[:, g * ATT_DH:(g + 1) * ATT_DH] for g in range(ATT_KV_HEADS)]
    vg1s = [jnp.concatenate([vband[:, g * ATT_DH:(g + 1) * ATT_DH], ones], axis=1) for g in range(ATT_KV_HEADS)]
    for h in range(ATT_HEADS):
        s_ref[h] = lax.dot_general(q[:, h * ATT_DH:(h + 1) * ATT_DH], kgs[h // group], (((1,), (1,)), ((), ())),
                                   preferred_element_type=F32)
    exps = []
    for h in range(ATT_HEADS):
        s = s_ref[h] * LOG2E + bias_ref[h]
        sk = sink_ref[h] * LOG2E
        m = jnp.maximum(jnp.max(s, axis=-1, keepdims=True), sk)
        p_ref[h] = jnp.exp2(s - m).astype(BF16)
        exps.append(jnp.exp2(sk - m))
    outs = []
    for h in range(ATT_HEADS):
        ov = jnp.dot(p_ref[h], vg1s[h // group], preferred_element_type=F32)
        outs.append(ov[:, :ATT_DH] / (ov[:, ATT_DH:ATT_DH + 1] + exps[h]))
    o_ref[...] = jnp.concatenate(outs, axis=1).astype(o_ref.dtype)


def _t5_relative_bucket(rel):
    half = REL_BUCKETS // 2
    max_exact = half // 2
    bucket = (rel > 0).astype(jnp.int32) * half
    n = jnp.abs(rel)
    n_safe = jnp.maximum(n, 1).astype(F32)
    large = max_exact + (jnp.log(n_safe / max_exact) / math.log(REL_MAX_DIST / max_exact)
                         * (half - max_exact)).astype(jnp.int32)
    large = jnp.clip(large, 0, half - 1)
    return bucket + jnp.where(n < max_exact, n, large)


def _attn_bias_table(rel_bias):
    w = ATT_BLOCK
    kofs = jnp.arange(3 * w, dtype=jnp.int32)[None, :] - w
    rel = kofs - jnp.arange(w, dtype=jnp.int32)[:, None]
    onehot = (_t5_relative_bucket(rel)[:, :, None] == jnp.arange(REL_BUCKETS)[None, None, :]).astype(F32)
    bias = jnp.einsum('qkb,bh->hqk', onehot, rel_bias.astype(F32), precision=lax.Precision.HIGHEST)
    band = jnp.abs(rel) <= WINDOW
    tabs = []
    for first, last in ((False, False), (True, False), (False, True), (True, True)):
        ok = band & ((kofs >= 0) | (not first)) & ((kofs < w) | (not last))
        tabs.append(jnp.where(ok[None], bias * LOG2E, MASK_VALUE))
    return jnp.stack(tabs)


def _attn(proj, proj_kv, sink, bias_tab, batch, seq):
    t = batch * seq
    w = ATT_BLOCK
    nb = seq // w
    kcol, vcol = 0, 1

    def kv(col, delta):
        return pl.BlockSpec((w, ATT_KVW), lambda b, n: (b * nb + jnp.clip(n + delta, 0, nb - 1), col))

    def variant(b, n):
        return ((n == 0).astype(jnp.int32) + 2 * (n == nb - 1).astype(jnp.int32), 0, 0, 0)

    return pl.pallas_call(
        _attn_kernel,
        grid=(batch, nb),
        in_specs=[
            pl.BlockSpec(memory_space=pltpu.SMEM),
            pl.BlockSpec((w, ATT_QW), lambda b, n: (b * nb + n, OFF_AQ // ATT_QW)),
            kv(kcol, -1), kv(kcol, 0), kv(kcol, 1),
            kv(vcol, -1), kv(vcol, 0), kv(vcol, 1),
            pl.BlockSpec((None, ATT_HEADS, w, 3 * w), variant),
        ],
        out_specs=pl.BlockSpec((w, ATT_QW), lambda b, n: (b * nb + n, 0)),
        out_shape=jax.ShapeDtypeStruct((t, ATT_QW), BF16),
        scratch_shapes=[pltpu.VMEM((ATT_HEADS, w, 3 * w), F32), pltpu.VMEM((ATT_HEADS, w, 3 * w), BF16)],
        compiler_params=_params(("arbitrary", "arbitrary"), 32),
        name="win_attn",
    )(sink.astype(F32), proj, proj_kv, proj_kv, proj_kv, proj_kv, proj_kv, proj_kv, bias_tab)


DFT_ROWS = 64


def _load_once(src_hbm, dst_vmem, sem, first):
    @pl.when(first)
    def _():
        cp = pltpu.make_async_copy(src_hbm, dst_vmem, sem)
        cp.start()
        cp.wait()


def _dft_tables(seq):
    n = 2 * seq
    k = jnp.arange(seq, dtype=jnp.int32)[:, None]
    s = jnp.arange(seq, dtype=jnp.int32)[None, :]
    theta = 2.0 * math.pi / n
    ang_a = ((jnp.arange(0, seq, DFT_ROWS, dtype=jnp.int32)[:, None] * s) % n).astype(F32) * theta
    ang_b = ((jnp.arange(DFT_ROWS, dtype=jnp.int32)[:, None] * s) % n).astype(F32) * theta
    ca, sa = jnp.cos(ang_a)[:, None, :], jnp.sin(ang_a)[:, None, :]
    cb, sb = jnp.cos(ang_b)[None], jnp.sin(ang_b)[None]
    cm = (ca * cb - sa * sb).reshape(seq, seq)
    sm = -(sa * cb + ca * sb).reshape(seq, seq)
    nyq = jnp.where(s % 2 == 0, 1.0, -1.0).astype(F32)
    sm = jnp.where(k == 0, nyq, sm)
    f = jnp.concatenate([cm, sm], axis=0).astype(BF16)
    return f, f.T


def _hy_positions(seq):
    t = jnp.linspace(0.0, 1.0, seq, dtype=F32)[:, None]
    w = 2.0 * math.pi * jnp.arange(seq, dtype=F32)[:, None] / seq
    f = jnp.linspace(1e-4, HY_BANDS - 1, HY_BANDS, dtype=F32)[None]
    z = jnp.concatenate([t, jnp.cos(f * w), -jnp.sin(f * w)], axis=-1)
    z = jnp.pad(z, ((0, 0), (0, V7X_LANES - HY_EMB)))
    max_decay = math.log(HY_TARGET) / HY_FAST_DECAY
    min_decay = math.log(HY_TARGET) / HY_SLOW_DECAY
    deltas = jnp.linspace(min_decay, max_decay, HY_W, dtype=F32)
    window = jnp.exp(-t * jnp.abs(deltas))
    return z, window


def _hy_filter_kernel(z_ref, w1_ref, b1_ref, w2_ref, b2_ref, fr_ref, w3_ref, win_ref, h_ref, hid_ref):
    dot = functools.partial(jnp.dot, precision=lax.Precision.HIGHEST, preferred_element_type=F32)

    @pl.when(pl.program_id(0) == 0)
    def _():
        fr = fr_ref[...]
        h = jnp.sin(fr * (dot(z_ref[...], w1_ref[...]) + b1_ref[...]))
        for j in range(HY_INNER):
            h = jnp.sin(fr * (dot(h, w2_ref[j]) + b2_ref[j]))
        hid_ref[...] = h

    h = dot(hid_ref[...], w3_ref[...]) * win_ref[...]
    row = lax.broadcasted_iota(jnp.int32, h.shape, 0)
    backward = pl.program_id(0) >= pl.num_programs(0) // 2
    h_ref[...] = jnp.where((row == 0) & backward, 0.0, h).astype(h_ref.dtype)


def _hy_filters(z, window, w1, b1, w2, b2, freq, w3):
    seq = z.shape[0]
    tn = HY_TC
    per_dir = HY_W // tn
    full = lambda shape: pl.BlockSpec(shape, lambda j: (0,) * len(shape))
    w1p = jnp.pad(w1, ((0, V7X_LANES - HY_EMB), (0, 0)))
    return pl.pallas_call(
        _hy_filter_kernel,
        grid=(2 * per_dir,),
        in_specs=[
            full((seq, V7X_LANES)), full((V7X_LANES, HY_ORDER)), full((1, HY_ORDER)),
            full((HY_INNER, HY_ORDER, HY_ORDER)), full((HY_INNER, 1, HY_ORDER)), full((1, HY_ORDER)),
            pl.BlockSpec((HY_ORDER, tn), lambda j: (0, j)),
            pl.BlockSpec((seq, tn), lambda j: (0, j % per_dir)),
        ],
        out_specs=pl.BlockSpec((seq, tn), lambda j: (0, j)),
        out_shape=jax.ShapeDtypeStruct((seq, 2 * HY_W), BF16),
        scratch_shapes=[pltpu.VMEM((seq, HY_ORDER), F32)],
        compiler_params=_params(("arbitrary",), 32),
        name="hy_filter",
    )(z, w1p, b1.reshape(1, HY_ORDER), w2, b2.reshape(HY_INNER, 1, HY_ORDER), freq.reshape(1, HY_ORDER), w3, window)


def _mm_kernel(a_ref, b_ref, o_ref):
    o_ref[...] = jnp.dot(a_ref[...], b_ref[...], preferred_element_type=F32).astype(o_ref.dtype)


def _mm(a, b, tm, tn, out_dtype):
    m, k = a.shape
    n = b.shape[1]
    return pl.pallas_call(
        _mm_kernel,
        grid=(m // tm, n // tn),
        in_specs=[pl.BlockSpec((tm, k), lambda i, j: (i, 0)), pl.BlockSpec((k, tn), lambda i, j: (0, j))],
        out_specs=pl.BlockSpec((tm, tn), lambda i, j: (i, j)),
        out_shape=jax.ShapeDtypeStruct((m, n), out_dtype),
        compiler_params=_params(("arbitrary", "arbitrary"), 32),
        name="mm",
    )(a, b)


def _hy_spectrum(fmat, hcat, seq):
    spec = _mm(fmat, hcat, min(1024, 2 * seq), 512, F32)
    top, bot = spec[:seq], spec[seq:]
    kr = top[:, :HY_W] + top[:, HY_W:]
    ki = bot[:, :HY_W] - bot[:, HY_W:]
    n = 2 * seq
    first = (jnp.arange(seq) == 0)[:, None]
    pr = jnp.where(first, kr / n, kr * (2.0 / n))
    pi = jnp.where(first, 0.0, ki * (2.0 / n))
    nyq = (bot[:1, :HY_W] + bot[:1, HY_W:]) / n
    return pr, pi, nyq


def _short_conv(x, w_ref, b_ref):
    seq = x.shape[0]
    row = lax.broadcasted_iota(jnp.int32, x.shape, 0)
    prev = jnp.where(row == 0, 0.0, pltpu.roll(x, 1, 0))
    nxt = jnp.where(row == seq - 1, 0.0, pltpu.roll(x, seq - 1, 0))
    return w_ref[0:1, :] * prev + w_ref[1:2, :] * x + w_ref[2:3, :] * nxt + b_ref[...]


def _first_step():
    return (pl.program_id(0) == 0) & (pl.program_id(1) == 0)


def _hy_fwd_kernel(x1_ref, v_ref, w1_ref, b1_ref, wv_ref, bv_ref, f_hbm, pr_ref, pi_ref, nyq_ref, y_ref,
                   f_ref, sem):
    seq = x1_ref.shape[0]
    _load_once(f_hbm, f_ref, sem, _first_step())
    x1 = _short_conv(x1_ref[...].astype(F32), w1_ref, b1_ref)
    v = _short_conv(v_ref[...].astype(F32), wv_ref, bv_ref)
    u = (x1 * v).astype(BF16)
    w = jnp.dot(f_ref[...], u, preferred_element_type=F32)
    a, b = w[:seq], w[seq:]
    pr, pi = pr_ref[...], pi_ref[...]
    row = lax.broadcasted_iota(jnp.int32, pr.shape, 0)
    pd = jnp.where(row == 0, nyq_ref[...], pr)
    y_ref[:seq, :] = (a * pr - b * pi).astype(y_ref.dtype)
    y_ref[seq:, :] = (a * pi + b * pd).astype(y_ref.dtype)


def _hy_inv_kernel(y_ref, x0_ref, x1_ref, v_ref, w0_ref, b0_ref, w1_ref, b1_ref, wv_ref, bv_ref, skip_ref,
                   g_hbm, o_ref, g_ref, sem):
    _load_once(g_hbm, g_ref, sem, _first_step())
    y = jnp.dot(g_ref[...], y_ref[...], preferred_element_type=F32)
    x0 = _short_conv(x0_ref[...].astype(F32), w0_ref, b0_ref)
    x1 = _short_conv(x1_ref[...].astype(F32), w1_ref, b1_ref)
    v = _short_conv(v_ref[...].astype(F32), wv_ref, bv_ref)
    u = x1 * v
    o_ref[...] = (x0 * (y + u * skip_ref[...])).astype(o_ref.dtype)


def _hyena(proj, conv_w, conv_b, skip, fmat, gmat, pr, pi, nyq, batch, seq):
    t = batch * seq
    tc = HY_TC // 2
    nct = HY_W // tc
    conv_b2 = conv_b.reshape(1, 3 * HY_W)

    def xcol(part):
        return pl.BlockSpec((seq, tc), lambda c, b: (b, (OFF_HY + part * HY_W) // tc + c))

    def wcol(part):
        return pl.BlockSpec((HY_SHORT, tc), lambda c, b: (0, part * nct + c))

    def bcol(part):
        return pl.BlockSpec((1, tc), lambda c, b: (0, part * nct + c))

    chan = pl.BlockSpec((seq, tc), lambda c, b: (0, c))
    chan1 = pl.BlockSpec((1, tc), lambda c, b: (0, c))
    whole = pl.BlockSpec(memory_space=pl.ANY)
    dft_scratch = [pltpu.VMEM(fmat.shape, BF16), pltpu.SemaphoreType.DMA(())]
    yspec = pl.BlockSpec((None, 2 * seq, tc), lambda c, b: (b, 0, c))
    yfreq = pl.pallas_call(
        _hy_fwd_kernel,
        grid=(nct, batch),
        in_specs=[xcol(1), xcol(2), wcol(1), bcol(1), wcol(2), bcol(2), whole, chan, chan, chan1],
        out_specs=yspec,
        out_shape=jax.ShapeDtypeStruct((batch, 2 * seq, HY_W), BF16),
        scratch_shapes=dft_scratch,
        compiler_params=_params(("arbitrary", "arbitrary"), 56),
        name="hy_fwd",
    )(proj, proj, conv_w, conv_b2, conv_w, conv_b2, fmat, pr, pi, nyq)
    return pl.pallas_call(
        _hy_inv_kernel,
        grid=(nct, batch),
        in_specs=[yspec, xcol(0), xcol(1), xcol(2), wcol(0), bcol(0), wcol(1), bcol(1), wcol(2), bcol(2), chan1,
                  whole],
        out_specs=pl.BlockSpec((seq, tc), lambda c, b: (b, c)),
        out_shape=jax.ShapeDtypeStruct((t, HY_W), BF16),
        scratch_shapes=[pltpu.VMEM(gmat.shape, BF16), pltpu.SemaphoreType.DMA(())],
        compiler_params=_params(("arbitrary", "arbitrary"), 56),
        name="hy_inv",
    )(yfreq, proj, proj, proj, conv_w, conv_b2, conv_w, conv_b2, conv_w, conv_b2, skip.reshape(1, HY_W), gmat)


def _branch_kernel(ohg_ref, ohy_ref, oat_ref, gates_ref, wb_hbm, m_ref, wb_ref, sem):
    d = D_MODEL
    _load_once(wb_hbm, wb_ref, sem, pl.program_id(0) == 0)
    m = None
    for n, o_ref in enumerate((ohg_ref, ohy_ref, oat_ref)):
        br = jnp.dot(o_ref[...], wb_ref[n], preferred_element_type=F32)
        term = jax.nn.sigmoid(gates_ref[:, n * d:(n + 1) * d].astype(F32)) * br
        m = term if m is None else m + term
    m_ref[...] = m.astype(m_ref.dtype)


def _merge_kernel(m_ref, h_ref, wo_hbm, lng_ref, lnb_ref, rwh_ref, rwl_ref, rb_ref,
                  hmid_ref, hp_ref, idx_ref, gate_ref, wo_ref, sem):
    _load_once(wo_hbm, wo_ref, sem, pl.program_id(0) == 0)
    y = jnp.dot(m_ref[...], wo_ref[...], preferred_element_type=F32)
    hn = _layer_norm_rows(DEEPNORM_ALPHA * h_ref[...] + y, lng_ref[...], lnb_ref[...])
    hmid_ref[...] = hn
    hp_ref[...] = _pack_halves(hn)

    h_hi = hn.astype(BF16)
    h_lo = (hn - h_hi.astype(F32)).astype(BF16)
    logits = (jnp.dot(h_hi, rwh_ref[...], preferred_element_type=F32)
              + jnp.dot(h_lo, rwh_ref[...], preferred_element_type=F32)
              + jnp.dot(h_hi, rwl_ref[...], preferred_element_type=F32)) + rb_ref[...]
    lane = lax.broadcasted_iota(jnp.int32, logits.shape, 1)
    vals, idxs = [], []
    for _ in range(TOP_K):
        mx = jnp.max(logits, axis=-1, keepdims=True)
        ix = jnp.min(jnp.where(logits == mx, lane, ROUTE_LANES), axis=-1, keepdims=True)
        vals.append(mx)
        idxs.append(ix)
        logits = jnp.where(lane == ix, -jnp.inf, logits)
    exps = [jnp.exp(v - vals[0]) for v in vals]
    total = exps[0]
    for e in exps[1:]:
        total = total + e
    gate_out = jnp.zeros(logits.shape, F32)
    idx_out = jnp.zeros(logits.shape, jnp.int32)
    for r in range(TOP_K):
        gate_out = jnp.where(lane == r, exps[r] / total, gate_out)
        idx_out = jnp.where(lane == r, idxs[r], idx_out)
    gate_ref[...] = gate_out
    idx_ref[...] = idx_out


def _merge(o_hg, o_hy, o_at, gates, h, wb, wo, ln_g, ln_b, rw, rb):
    t, d = h.shape
    tm = min(MERGE_TM, t)
    row = lambda width: pl.BlockSpec((tm, width), lambda i: (i, 0))
    const = lambda shape: pl.BlockSpec(shape, lambda i: (0,) * len(shape))
    whole = pl.BlockSpec(memory_space=pl.ANY)
    rwp = jnp.pad(rw, ((0, 0), (0, ROUTE_LANES - N_EXPERTS)))
    rw_hi = rwp.astype(BF16)
    rw_lo = (rwp - rw_hi.astype(F32)).astype(BF16)
    rbp = jnp.pad(rb, (0, ROUTE_LANES - N_EXPERTS), constant_values=MASK_VALUE).reshape(1, ROUTE_LANES)
    m = pl.pallas_call(
        _branch_kernel,
        grid=(t // tm,),
        in_specs=[row(MIX_W), row(MIX_W), row(MIX_W), row(N_BRANCH * d), whole],
        out_specs=row(d),
        out_shape=jax.ShapeDtypeStruct((t, d), BF16),
        scratch_shapes=[pltpu.VMEM((N_BRANCH, MIX_W, d), BF16), pltpu.SemaphoreType.DMA(())],
        compiler_params=_params(("arbitrary",), 56),
        name="branch_merge",
    )(o_hg, o_hy, o_at, gates, wb)
    return pl.pallas_call(
        _merge_kernel,
        grid=(t // tm,),
        in_specs=[row(d), row(d), whole, const((1, d)), const((1, d)),
                  const((d, ROUTE_LANES)), const((d, ROUTE_LANES)), const((1, ROUTE_LANES))],
        out_specs=[row(d), row(d // 2), row(ROUTE_LANES), row(ROUTE_LANES)],
        out_shape=[jax.ShapeDtypeStruct((t, d), F32), jax.ShapeDtypeStruct((t, d // 2), jnp.uint32),
                   jax.ShapeDtypeStruct((t, ROUTE_LANES), jnp.int32), jax.ShapeDtypeStruct((t, ROUTE_LANES), F32)],
        scratch_shapes=[pltpu.VMEM((d, d), BF16), pltpu.SemaphoreType.DMA(())],
        compiler_params=_params(("arbitrary",), 56),
        name="merge",
    )(m, h, wo, ln_g.reshape(1, d), ln_b.reshape(1, d), rw_hi, rw_lo, rbp)


def _route_plan(top_idx, n_tiles):
    e = top_idx.reshape(-1)
    onehot = (e[:, None] == jnp.arange(N_EXPERTS, dtype=jnp.int32)[None, :]).astype(jnp.int32)
    csum = jnp.cumsum(onehot, axis=0)
    rank = jnp.sum(csum * onehot, axis=1) - 1
    counts = csum[-1]
    padded = (counts + MOE_TM - 1) // MOE_TM * MOE_TM
    p_end = jnp.cumsum(padded)
    p_start = p_end - padded
    pos = (p_start[e] + rank).astype(jnp.int32)
    n_used = (p_end[-1] // MOE_TM).astype(jnp.int32)
    tile_start = jnp.arange(n_tiles, dtype=jnp.int32) * MOE_TM
    tile_expert = jnp.minimum(jnp.searchsorted(p_end, tile_start, side='right'), N_EXPERTS - 1).astype(jnp.int32)
    tile_rows = jnp.clip(counts[tile_expert] - (tile_start - p_start[tile_expert]), 0, MOE_TM).astype(jnp.int32)
    return pos, tile_expert, tile_rows, n_used.reshape(1)


def _dispatch_kernel(pos_ref, hp_ref, xs_ref, sem):
    def body(r, carry):
        for k in range(TOP_K):
            dst = pos_ref[0, r * TOP_K + k]
            pltpu.make_async_copy(hp_ref.at[pl.ds(r, 1), :], xs_ref.at[pl.ds(dst, 1), :], sem).start(priority=k % 2)
        return carry

    lax.fori_loop(0, hp_ref.shape[0], body, 0)
    for k in range(TOP_K):
        pltpu.make_async_copy(hp_ref, xs_ref.at[pl.ds(0, hp_ref.shape[0]), :], sem).wait()


def _dispatch(hp, pos, n_slots):
    t, dw = hp.shape
    tq = min(TOK_TILE, t)
    return pl.pallas_call(
        _dispatch_kernel,
        grid=(t // tq,),
        in_specs=[pl.BlockSpec((None, 1, tq * TOP_K), lambda i: (i, 0, 0), memory_space=pltpu.SMEM),
                  pl.BlockSpec((tq, dw), lambda i: (i, 0))],
        out_specs=pl.BlockSpec(memory_space=pl.ANY),
        out_shape=jax.ShapeDtypeStruct((n_slots, dw), jnp.uint32),
        scratch_shapes=[pltpu.SemaphoreType.DMA(())],
        compiler_params=_params(("arbitrary",), 32),
        name="moe_dispatch",
    )(pos.reshape(t // tq, 1, tq * TOP_K), hp)


def _ffn_kernel(te_ref, tr_ref, nu_ref, x_ref, wg_ref, wu_ref, bgu_ref, wdl_ref, wdh_ref, bd_ref,
                y_ref, xb_ref, act_ref):
    i = pl.program_id(0)
    j = pl.program_id(1)
    rows = tr_ref[i]
    active = i < nu_ref[0]
    half = x_ref.shape[1]
    nf = D_FF // MOE_TF
    nsub = (rows + MOE_SUB - 1) // MOE_SUB

    @pl.when(active & (j == 0))
    def _():
        lo, hi = _unpack_halves(x_ref[...])
        keep = lax.broadcasted_iota(jnp.int32, lo.shape, 0) < rows
        xb_ref[:, :half] = jnp.where(keep, lo, 0.0).astype(BF16)
        xb_ref[:, half:] = jnp.where(keep, hi, 0.0).astype(BF16)

    fcol = pl.multiple_of(jnp.minimum(j, nf - 1) * MOE_TF, MOE_TF)
    ncol = pl.multiple_of(jnp.maximum(j - nf, 0) * MOE_TN, MOE_TN)

    for k in range(1, MOE_TM // MOE_SUB + 1):
        m = k * MOE_SUB

        @pl.when(active & (j < nf) & (nsub == k))
        def _():
            xs = xb_ref[:m, :]
            g = jnp.dot(xs, wg_ref[...].astype(BF16), preferred_element_type=F32) + bgu_ref[:, pl.ds(fcol, MOE_TF)]
            u = (jnp.dot(xs, wu_ref[...].astype(BF16), preferred_element_type=F32)
                 + bgu_ref[:, pl.ds(D_FF + fcol, MOE_TF)])
            g = jnp.minimum(g, SWIGLU_LIMIT)
            u = jnp.clip(u, -SWIGLU_LIMIT, SWIGLU_LIMIT)
            act_ref[:m, pl.ds(fcol, MOE_TF)] = ((u + 1.0) * g * jax.nn.sigmoid(SWIGLU_ALPHA * g)).astype(BF16)

        @pl.when(active & (j >= nf) & (nsub == k))
        def _():
            act = act_ref[:m, :]
            lo = (jnp.dot(act, wdl_ref[...].astype(BF16), preferred_element_type=F32)
                  + bd_ref[:, pl.ds(ncol, MOE_TN)])
            hi = (jnp.dot(act, wdh_ref[...].astype(BF16), preferred_element_type=F32)
                  + bd_ref[:, pl.ds(half + ncol, MOE_TN)])
            y_ref[:m, :] = _pack_pair(lo, hi)
            if m < MOE_TM:
                y_ref[m:, :] = jnp.zeros((MOE_TM - m, y_ref.shape[1]), jnp.uint32)


def _ffn(xs, w_gate_up, b_gate_up, w_down, b_down, layer, tile_expert, tile_rows, n_used):
    n_slots, dw = xs.shape
    d = 2 * dw
    n_tiles = n_slots // MOE_TM
    nf = D_FF // MOE_TF
    nn = dw // MOE_TN
    bgu = b_gate_up.reshape(DEPTH, N_EXPERTS, 1, 2 * D_FF)
    bd = b_down.reshape(DEPTH, N_EXPERTS, 1, d)

    def tile(i, nu):
        return jnp.minimum(i, nu[0] - 1)

    def fcol(i, j, nu):
        return jnp.where(i < nu[0], jnp.minimum(j, nf - 1), nf - 1)

    def ncol(i, j, nu):
        return jnp.where(i < nu[0], jnp.maximum(j - nf, 0), nn - 1)

    xmap = lambda i, j, te, tr, nu: (tile(i, nu), 0)
    gmap = lambda i, j, te, tr, nu: (layer, te[tile(i, nu)], 0, fcol(i, j, nu))
    umap = lambda i, j, te, tr, nu: (layer, te[tile(i, nu)], 0, nf + fcol(i, j, nu))
    dlmap = lambda i, j, te, tr, nu: (layer, te[tile(i, nu)], 0, ncol(i, j, nu))
    dhmap = lambda i, j, te, tr, nu: (layer, te[tile(i, nu)], 0, nn + ncol(i, j, nu))
    bmap = lambda i, j, te, tr, nu: (layer, te[tile(i, nu)], 0, 0)
    ymap = lambda i, j, te, tr, nu: (tile(i, nu), ncol(i, j, nu))
    grid_spec = pltpu.PrefetchScalarGridSpec(
        num_scalar_prefetch=3,
        grid=(n_tiles, nf + nn),
        in_specs=[
            pl.BlockSpec((MOE_TM, dw), xmap, pipeline_mode=pl.Buffered(1)),
            pl.BlockSpec((None, None, d, MOE_TF), gmap),
            pl.BlockSpec((None, None, d, MOE_TF), umap),
            pl.BlockSpec((None, None, 1, 2 * D_FF), bmap),
            pl.BlockSpec((None, None, D_FF, MOE_TN), dlmap),
            pl.BlockSpec((None, None, D_FF, MOE_TN), dhmap),
            pl.BlockSpec((None, None, 1, d), bmap),
        ],
        out_specs=pl.BlockSpec((MOE_TM, MOE_TN), ymap),
        scratch_shapes=[pltpu.VMEM((MOE_TM, d), BF16), pltpu.VMEM((MOE_TM, D_FF), BF16)],
    )
    return pl.pallas_call(
        _ffn_kernel,
        grid_spec=grid_spec,
        out_shape=jax.ShapeDtypeStruct((n_slots, dw), jnp.uint32),
        compiler_params=_params(("arbitrary", "arbitrary"), 56),
        name="moe_ffn",
    )(tile_expert, tile_rows, n_used, xs, w_gate_up, w_gate_up, bgu, w_down, w_down, bd)


def _combine_kernel(pos_ref, posn_ref, ys_ref, gate_ref, h_ref, lng_ref, lnb_ref, hout_ref, hb_ref, buf, sem):
    i = pl.program_id(0)
    tq, d = h_ref.shape
    half = d // 2
    slot = i % 2

    def issue(p_ref, s):
        def body(r, carry):
            for k in range(TOP_K):
                src = p_ref[0, r * TOP_K + k]
                pltpu.make_async_copy(ys_ref.at[pl.ds(src, 1), :], buf.at[s, k, pl.ds(r, 1), :],
                                      sem.at[s]).start(priority=k % 2)
            return carry

        lax.fori_loop(0, tq, body, 0, unroll=4)

    @pl.when(i == 0)
    def _():
        issue(pos_ref, 0)

    for s in range(2):
        @pl.when((i + 1 < pl.num_programs(0)) & (slot == 1 - s))
        def _():
            issue(posn_ref, s)

    for k in range(TOP_K):
        pltpu.make_async_copy(ys_ref.at[pl.ds(0, tq), :], buf.at[slot, k], sem.at[slot]).wait()
    gate = gate_ref[...]
    acc_lo = DEEPNORM_ALPHA * h_ref[:, :half]
    acc_hi = DEEPNORM_ALPHA * h_ref[:, half:]
    for k in range(TOP_K):
        lo, hi = _unpack_halves(buf[slot, k])
        acc_lo = acc_lo + gate[:, k:k + 1] * lo
        acc_hi = acc_hi + gate[:, k:k + 1] * hi
    mu = (jnp.sum(acc_lo, axis=-1, keepdims=True) + jnp.sum(acc_hi, axis=-1, keepdims=True)) / d
    acc_lo = acc_lo - mu
    acc_hi = acc_hi - mu
    var = (jnp.sum(acc_lo * acc_lo, axis=-1, keepdims=True) + jnp.sum(acc_hi * acc_hi, axis=-1, keepdims=True)) / d
    inv = lax.rsqrt(var + LN_EPS)
    for sl, acc in ((slice(0, half), acc_lo), (slice(half, d), acc_hi)):
        hn = acc * inv * lng_ref[:, sl] + lnb_ref[:, sl]
        hout_ref[:, sl] = hn
        hb_ref[:, sl] = hn.astype(BF16)


def _combine(ys, pos, gate, h, ln_g, ln_b):
    t, d = h.shape
    tq = min(TOK_TILE, t)
    nt = t // tq
    row = lambda width: pl.BlockSpec((tq, width), lambda i: (i, 0))
    vec = pl.BlockSpec((1, d), lambda i: (0, 0))
    pos3 = pos.reshape(nt, 1, tq * TOP_K)
    return pl.pallas_call(
        _combine_kernel,
        grid=(nt,),
        in_specs=[pl.BlockSpec((None, 1, tq * TOP_K), lambda i: (i, 0, 0), memory_space=pltpu.SMEM),
                  pl.BlockSpec((None, 1, tq * TOP_K), lambda i: (jnp.minimum(i + 1, nt - 1), 0, 0),
                               memory_space=pltpu.SMEM),
                  pl.BlockSpec(memory_space=pl.ANY), row(ROUTE_LANES), row(d), vec, vec],
        out_specs=[row(d), row(d)],
        out_shape=[jax.ShapeDtypeStruct((t, d), F32), jax.ShapeDtypeStruct((t, d), BF16)],
        scratch_shapes=[pltpu.VMEM((2, TOP_K, tq, d // 2), jnp.uint32), pltpu.SemaphoreType.DMA((2,))],
        compiler_params=_params(("arbitrary",), 40),
        name="moe_combine",
    )(pos3, pos3, ys, gate, h, ln_g.reshape(1, d), ln_b.reshape(1, d))


def _hgrn_lower_bound(lb_table, layer):
    p = jax.nn.softmax(lb_table.astype(F32), axis=0)
    return jnp.cumsum(p, axis=0)[layer] - p[0]


def kernel(x, ln_in_g, ln_in_b, w_in, hg_lower_bound, hg_norm_g, hy_conv_w, hy_conv_b, hy_filt_w1, hy_filt_b1,
           hy_filt_w2, hy_filt_b2, hy_filt_freq, hy_filt_w3, hy_skip, att_sink, rel_bias, w_branch, w_out,
           ln_mix_g, ln_mix_b, router_w, router_b, w_gate_up, b_gate_up, w_down, b_down, ln_moe_g, ln_moe_b):
    batch, seq, d = x.shape
    t = batch * seq
    n_tiles = t * TOP_K // MOE_TM + N_EXPERTS
    n_slots = n_tiles * MOE_TM

    fmat, gmat = _dft_tables(seq)
    z_pos, window = _hy_positions(seq)
    bias_tab = _attn_bias_table(rel_bias)

    h, hb = _ln_in(x.reshape(t, d), ln_in_g, ln_in_b)
    for layer in range(DEPTH):
        proj = _inproj(hb, w_in, layer, 0, OFF_AK, PROJ_TN)
        proj_kv = _inproj(hb, w_in, layer, OFF_AK, PROJ_KV_COLS, PROJ_KV_COLS)
        w_gates = lax.slice(w_in, (layer, 0, OFF_GATES), (layer + 1, d, IN_COLS))
        gates = _inproj(hb, w_gates, 0, 0, N_BRANCH * d, PROJ_TN)

        lb = _hgrn_lower_bound(hg_lower_bound, layer)
        o_hg = _hgrn(proj, lb, hg_norm_g[layer], batch, seq)

        hcat = _hy_filters(z_pos, window, hy_filt_w1[layer], hy_filt_b1[layer], hy_filt_w2[layer],
                           hy_filt_b2[layer], hy_filt_freq[layer], hy_filt_w3[layer])
        pr, pi, nyq = _hy_spectrum(fmat, hcat, seq)
        o_hy = _hyena(proj, hy_conv_w[layer], hy_conv_b[layer], hy_skip[layer], fmat, gmat, pr, pi, nyq,
                      batch, seq)

        o_at = _attn(proj, proj_kv, att_sink[layer], bias_tab, batch, seq)

        h_mid, hp, top_idx, gate = _merge(
            o_hg, o_hy, o_at, gates, h, w_branch[layer].astype(BF16), w_out[layer].astype(BF16),
            ln_mix_g[layer], ln_mix_b[layer], router_w[layer], router_b[layer])

        pos, tile_expert, tile_rows, n_used = _route_plan(top_idx[:, :TOP_K], n_tiles)
        xs = _dispatch(hp, pos, n_slots)
        ys = _ffn(xs, w_gate_up, b_gate_up, w_down, b_down, layer, tile_expert, tile_rows, n_used)
        h, hb = _combine(ys, pos, gate, h_mid, ln_moe_g[layer], ln_moe_b[layer])
    return h.reshape(batch, seq, d)
```

```python
import functools
import math

import jax
import jax.numpy as jnp
from jax import lax
from jax.experimental import pallas as pl
from jax.experimental.pallas import tpu as pltpu

F32 = jnp.float32
BF16 = jnp.bfloat16

D_MODEL = 2048
DEPTH = 2
MIX_W = 1024
N_BRANCH = 3
HG_HEADS = 8
HG_DK = 128
HG_DV = 128
HG_CHUNK = 16
F_FLOOR = 1e-30
HY_W = 1024
HY_SHORT = 3
HY_EMB = 33
HY_BANDS = (HY_EMB - 1) // 2
HY_ORDER = 64
HY_INNER = 2
HY_FAST_DECAY = 0.3
HY_SLOW_DECAY = 1.5
HY_TARGET = 1e-2
ATT_HEADS = 16
ATT_KV_HEADS = 2
ATT_DH = 64
WINDOW = 128
ATT_BLOCK = 128
REL_BUCKETS = 32
REL_MAX_DIST = 128
MASK_VALUE = -1e30
N_EXPERTS = 32
TOP_K = 4
D_FF = 2048
SWIGLU_ALPHA = 1.702
SWIGLU_LIMIT = 7.0
LN_EPS = 1e-5
RMS_EPS = 1e-6
DEEPNORM_ALPHA = (2 * DEPTH) ** 0.25

HG_KW = HG_HEADS * HG_DK
HG_VW = HG_HEADS * HG_DV
ATT_QW = ATT_HEADS * ATT_DH
ATT_KVW = ATT_KV_HEADS * ATT_DH
IN_SIZES = (HG_KW, HG_KW, HG_KW, HG_VW, HG_VW, 3 * HY_W, ATT_QW, ATT_KVW, ATT_KVW, N_BRANCH * D_MODEL)
IN_COLS = sum(IN_SIZES)
OFF_HQ, OFF_HFF, OFF_HFB, OFF_HI, OFF_HOG = 0, HG_KW, 2 * HG_KW, 3 * HG_KW, 3 * HG_KW + HG_VW
OFF_HY = OFF_HOG + HG_VW
OFF_AQ = OFF_HY + 3 * HY_W
OFF_AK = OFF_AQ + ATT_QW
OFF_AV = OFF_AK + ATT_KVW
OFF_GATES = OFF_AV + ATT_KVW
MAIN_COLS = OFF_GATES

V7X_LANES = 128
V7X_VMEM_BYTES = 64 * 1024 * 1024

LN_ROWS = 512
PROJ_TM = 1024
PROJ_TN = 1024
PROJ_KV_COLS = 2 * ATT_KVW
HY_TC = 512
MERGE_TM = 512
MOE_TM = 1024
MOE_SUB = 256
MOE_TF = 512
MOE_TN = 512
TOK_TILE = 256
ROUTE_LANES = 128


def _params(semantics, vmem_mb):
    return pltpu.CompilerParams(dimension_semantics=semantics, vmem_limit_bytes=vmem_mb * 1024 * 1024)


def _layer_norm_rows(x, g, b):
    mu = jnp.mean(x, axis=-1, keepdims=True)
    xc = x - mu
    var = jnp.mean(xc * xc, axis=-1, keepdims=True)
    return xc * lax.rsqrt(var + LN_EPS) * g + b


def _pack_halves(x):
    c = x.shape[-1] // 2
    return _pack_pair(x[:, :c], x[:, c:])


def _pack_pair(lo, hi):
    lo = pltpu.bitcast(lo.astype(BF16).astype(F32), jnp.uint32)
    hi = pltpu.bitcast(hi.astype(BF16).astype(F32), jnp.uint32)
    return (lo >> 16) | (hi & jnp.uint32(0xFFFF0000))


def _unpack_halves(w):
    lo = pltpu.bitcast(w << 16, F32)
    hi = pltpu.bitcast(w & jnp.uint32(0xFFFF0000), F32)
    return lo, hi


def _ln_in_kernel(x_ref, g_ref, b_ref, h_ref, hb_ref):
    y = _layer_norm_rows(x_ref[...], g_ref[...], b_ref[...])
    h_ref[...] = y
    hb_ref[...] = y.astype(BF16)


def _ln_in(x2, g, b):
    t, d = x2.shape
    tm = min(LN_ROWS, t)
    row = pl.BlockSpec((tm, d), lambda i: (i, 0))
    vec = pl.BlockSpec((1, d), lambda i: (0, 0))
    return pl.pallas_call(
        _ln_in_kernel,
        grid=(t // tm,),
        in_specs=[row, vec, vec],
        out_specs=[row, row],
        out_shape=[jax.ShapeDtypeStruct((t, d), F32), jax.ShapeDtypeStruct((t, d), BF16)],
        compiler_params=_params(("arbitrary",), 32),
        name="ln_in",
    )(x2, g.reshape(1, d), b.reshape(1, d))


def _inproj_kernel(a_ref, w_ref, o_ref, wb_ref):
    @pl.when(pl.program_id(1) == 0)
    def _():
        wb_ref[...] = w_ref[...].astype(BF16)

    o_ref[...] = jnp.dot(a_ref[...], wb_ref[...], preferred_element_type=F32).astype(o_ref.dtype)


def _inproj(hb, w_in, layer, col0, ncols, tn):
    t, d = hb.shape
    tm = min(PROJ_TM, t)
    cb0 = col0 // tn
    return pl.pallas_call(
        _inproj_kernel,
        grid=(ncols // tn, t // tm),
        in_specs=[
            pl.BlockSpec((tm, d), lambda j, i: (i, 0)),
            pl.BlockSpec((None, d, tn), lambda j, i: (layer, 0, cb0 + j)),
        ],
        out_specs=pl.BlockSpec((tm, tn), lambda j, i: (i, j)),
        out_shape=jax.ShapeDtypeStruct((t, ncols), BF16),
        scratch_shapes=[pltpu.VMEM((d, tn), BF16)],
        compiler_params=_params(("arbitrary", "arbitrary"), 48),
        name="inproj",
    )(hb, w_in)


HG_BLK = 128
HG_CPB = HG_BLK // HG_CHUNK
HG_UNROLL = 128
LOG2E = 1.4426950408889634


def _hgrn_perm():
    r = jnp.arange(HG_BLK)
    src = (r % HG_CPB) * HG_CHUNK + r // HG_CPB
    return (src[:, None] == jnp.arange(HG_BLK)[None, :]).astype(BF16)


def _hgrn_kernel(q_ref, ff_ref, fb_ref, i_ref, og_ref, lbf_ref, lbb_ref, g_ref, p_ref, pt_ref, o_ref,
                 x_t, kk_t, o_t, tmp_t, qtil_c, ktil_c, kv_s, st_s, oint_c, dec_c, *, seq):
    c = HG_CHUNK
    nc = seq // c
    nb = seq // HG_BLK
    nbh = max(nb // 2, 1)
    n_half = nb // nbh
    perm = p_ref[...]
    perm_t = pt_ref[...]

    for a, ref in enumerate((q_ref, ff_ref, fb_ref, i_ref)):
        for r in range(nb):
            xp = jnp.dot(perm, ref[r * HG_BLK:(r + 1) * HG_BLK, :], preferred_element_type=F32)
            x_t[a, r] = xp.reshape(c, HG_CPB, HG_DK)

    lbs = (lbf_ref[...], lbb_ref[...])
    for j in range(c):
        qj = x_t[0, :, j]
        x_t[0, :, j] = qj * jax.nn.sigmoid(qj) * (HG_DK ** -0.5)
        for d in range(2):
            z = x_t[1 + d, :, j]
            lb = lbs[d]
            e = jnp.exp(-jnp.abs(z))
            r_ = 1.0 / (1.0 + e)
            er = e * r_
            sig = jnp.where(z >= 0, r_, er)
            nsig = jnp.where(z >= 0, er, r_)
            f = lb + (1.0 - lb) * sig
            x_t[1 + d, :, j] = jnp.log(jnp.maximum(f, F_FLOOR)) * LOG2E
            kk_t[d, :, j] = (1.0 - lb) * nsig

    for d in range(2):
        acc = jnp.zeros((nb, HG_CPB, HG_DK), F32)
        for j in (range(c) if d == 0 else reversed(range(c))):
            acc = acc + x_t[1 + d, :, j]
            x_t[1 + d, :, j] = acc
        last = acc
        dec_c[d] = jnp.exp2(last).reshape(nc, HG_DK)
        for j in range(c):
            cj = x_t[1 + d, :, j]
            tmp_t[0, :, j] = x_t[0, :, j] * jnp.exp2(cj)
            tmp_t[1, :, j] = kk_t[d, :, j] * jnp.exp2(last - cj)
        for r in range(nb):
            rows = slice(r * HG_BLK, (r + 1) * HG_BLK)
            qb = tmp_t[0, r].reshape(HG_BLK, HG_DK).astype(BF16)
            kb = tmp_t[1, r].reshape(HG_BLK, HG_DK).astype(BF16)
            qtil_c[d, rows, :] = jnp.dot(perm_t, qb, preferred_element_type=F32).astype(BF16)
            ktil_c[rows, d * HG_DK:(d + 1) * HG_DK] = jnp.dot(perm_t, kb, preferred_element_type=F32).astype(BF16)

    o_t[...] = jnp.zeros(o_t.shape, F32)
    ones_b = jnp.ones((HG_DK, HG_DV), BF16)

    def pair_body(it, carry):
        d = it // n_half
        sl = pl.ds((it % n_half) * nbh, nbh)
        for t in range(c):
            rt = t + d * (c - 1 - 2 * t)
            ct = x_t[1 + d, sl, rt]
            qt = x_t[0, sl, rt]
            acc = jnp.zeros((nbh * HG_CPB, HG_DV), F32)
            for s in range(t + 1):
                rs = s + d * (c - 1 - 2 * s)
                e = jnp.exp2(jnp.minimum(ct - x_t[1 + d, sl, rs], 0.0))
                a = (qt * e * kk_t[d, sl, rs]).reshape(nbh * HG_CPB, HG_DK).astype(BF16)
                p = jnp.dot(a, ones_b, preferred_element_type=F32)
                acc = acc + p * x_t[3, sl, rs].reshape(nbh * HG_CPB, HG_DV)
            o_t[sl, rt] += acc.reshape(nbh, HG_CPB, HG_DV)
        return carry

    for it in range(2 * n_half):
        pair_body(it, 0)

    def kv_body(n, carry):
        r0 = pl.multiple_of(n * c, c)
        kv = lax.dot_general(i_ref[pl.ds(r0, c), :], ktil_c[pl.ds(r0, c), :], (((0,), (0,)), ((), ())),
                             preferred_element_type=F32)
        kv_s[0, n] = kv[:, :HG_DK]
        kv_s[1, n] = kv[:, HG_DK:]
        return carry

    lax.fori_loop(0, nc, kv_body, 0, unroll=min(HG_UNROLL, nc))

    def chain_body(idx, carry):
        new = []
        for d in range(2):
            n = idx if d == 0 else nc - 1 - idx
            s = carry[d]
            new.append(dec_c[d, pl.ds(n, 1), :] * s + kv_s[d, n])
            st_s[n, d * HG_DV:(d + 1) * HG_DV, :] = s.astype(BF16)
        return tuple(new)

    zero_state = jnp.zeros((HG_DV, HG_DK), F32)
    lax.fori_loop(0, nc, chain_body, (zero_state, zero_state), unroll=8)

    def out_body(n, carry):
        r0 = pl.multiple_of(n * c, c)
        lhs = jnp.concatenate([qtil_c[0, pl.ds(r0, c), :], qtil_c[1, pl.ds(r0, c), :]], axis=0)
        out = lax.dot_general(lhs, st_s[n], (((1,), (1,)), ((), ())), preferred_element_type=F32)
        oint_c[0, pl.ds(r0, c), :] = out[:c, :HG_DV]
        oint_c[1, pl.ds(r0, c), :] = out[c:, HG_DV:]
        return carry

    lax.fori_loop(0, nc, out_body, 0, unroll=min(HG_UNROLL, nc))

    for r in range(nb):
        rows = slice(r * HG_BLK, (r + 1) * HG_BLK)
        ob = o_t[r].reshape(HG_BLK, HG_DV)
        hi = ob.astype(BF16)
        lo = (ob - hi.astype(F32)).astype(BF16)
        oc = jnp.dot(perm_t, hi, preferred_element_type=F32) + jnp.dot(perm_t, lo, preferred_element_type=F32)
        o = oc + oint_c[0, rows, :] + oint_c[1, rows, :]
        o = o * lax.rsqrt(jnp.mean(o * o, axis=-1, keepdims=True) + RMS_EPS) * g_ref[...]
        og = og_ref[rows, :].astype(F32)
        o_ref[rows, :] = (o * (og * jax.nn.sigmoid(og))).astype(o_ref.dtype)


def _hgrn(proj, lb, norm_g, batch, seq):
    t = batch * seq
    nc = seq // HG_CHUNK

    def col(off):
        return pl.BlockSpec((seq, HG_DK), lambda b, h: (b, off // HG_DK + h))

    vec = pl.BlockSpec((None, 1, HG_DK), lambda b, h: (h, 0, 0))
    lbf = lb[:HG_KW].reshape(HG_HEADS, 1, HG_DK)
    lbb = lb[HG_KW:].reshape(HG_HEADS, 1, HG_DK)
    g = norm_g.reshape(HG_HEADS, 1, HG_DV)
    slab = (seq // HG_BLK, HG_CHUNK, HG_CPB, HG_DK)
    perm = _hgrn_perm()
    pspec = pl.BlockSpec((HG_BLK, HG_BLK), lambda b, h: (0, 0))
    return pl.pallas_call(
        functools.partial(_hgrn_kernel, seq=seq),
        grid=(batch, HG_HEADS),
        in_specs=[col(OFF_HQ), col(OFF_HFF), col(OFF_HFB), col(OFF_HI), col(OFF_HOG), vec, vec, vec, pspec, pspec],
        out_specs=pl.BlockSpec((seq, HG_DV), lambda b, h: (b, h)),
        out_shape=jax.ShapeDtypeStruct((t, HG_VW), BF16),
        scratch_shapes=[
            pltpu.VMEM((4,) + slab, F32),
            pltpu.VMEM((2,) + slab, F32),
            pltpu.VMEM(slab, F32),
            pltpu.VMEM((2,) + slab, F32),
            pltpu.VMEM((2, seq, HG_DK), BF16),
            pltpu.VMEM((seq, 2 * HG_DK), BF16),
            pltpu.VMEM((2, nc, HG_DV, HG_DK), F32),
            pltpu.VMEM((nc, 2 * HG_DV, HG_DK), BF16),
            pltpu.VMEM((2, seq, HG_DV), F32),
            pltpu.VMEM((2, nc, HG_DK), F32),
        ],
        compiler_params=_params(("arbitrary", "arbitrary"), 48),
        name="hgrn2",
    )(proj, proj, proj, proj, proj, lbf, lbb, g, perm, perm.T)


def _attn_kernel(sink_ref, q_ref, kp_ref, ko_ref, kn_ref, vp_ref, vo_ref, vn_ref, bias_ref, o_ref, s_ref, p_ref):
    w = ATT_BLOCK
    group = ATT_HEADS // ATT_KV_HEADS
    kband = jnp.concatenate([kp_ref[...], ko_ref[...], kn_ref[...]], axis=0)
    vband = jnp.concatenate([vp_ref[...], vo_ref[...], vn_ref[...]], axis=0)
    ones = jnp.ones((3 * w, ATT_DH), BF16)
    q = q_ref[...] * jnp.asarray(ATT_DH ** -0.5, BF16)
    kgs = [kband[:, g * ATT_DH:(g + 1) * ATT_DH] for g in range(ATT_KV_HEADS)]
    vg1s = [jnp.concatenate([vband[:, g * ATT_DH:(g + 1) * ATT_DH], ones], axis=1) for g in range(ATT_KV_HEADS)]
    for h in range(ATT_HEADS):
        s_ref[h] = lax.dot_general(q[:, h * ATT_DH:(h + 1) * ATT_DH], kgs[h // group], (((1,), (1,)), ((), ())),
                                   preferred_element_type=F32)
    exps = []
    for h in range(ATT_HEADS):
        s = s_ref[h] * LOG2E + bias_ref[h]
        sk = sink_ref[h] * LOG2E
        m = jnp.maximum(jnp.max(s, axis=-1, keepdims=True), sk)
        p_ref[h] = jnp.exp2(s - m).astype(BF16)
        exps.append(jnp.exp2(sk - m))
    outs = []
    for h in range(ATT_HEADS):
        ov = jnp.dot(p_ref[h], vg1s[h // group], preferred_element_type=F32)
        outs.append(ov[:, :ATT_DH] / (ov[:, ATT_DH:ATT_DH + 1] + exps[h]))
    o_ref[...] = jnp.concatenate(outs, axis=1).astype(o_ref.dtype)


def _t5_relative_bucket(rel):
    half = REL_BUCKETS // 2
    max_exact = half // 2
    bucket = (rel > 0).astype(jnp.int32) * half
    n = jnp.abs(rel)
    n_safe = jnp.maximum(n, 1).astype(F32)
    large = max_exact + (jnp.log(n_safe / max_exact) / math.log(REL_MAX_DIST / max_exact)
                         * (half - max_exact)).astype(jnp.int32)
    large = jnp.clip(large, 0, half - 1)
    return bucket + jnp.where(n < max_exact, n, large)


def _attn_bias_table(rel_bias):
    w = ATT_BLOCK
    kofs = jnp.arange(3 * w, dtype=jnp.int32)[None, :] - w
    rel = kofs - jnp.arange(w, dtype=jnp.int32)[:, None]
    onehot = (_t5_relative_bucket(rel)[:, :, None] == jnp.arange(REL_BUCKETS)[None, None, :]).astype(F32)
    bias = jnp.einsum('qkb,bh->hqk', onehot, rel_bias.astype(F32), precision=lax.Precision.HIGHEST)
    band = jnp.abs(rel) <= WINDOW
    tabs = []
    for first, last in ((False, False), (True, False), (False, True), (True, True)):
        ok = band & ((kofs >= 0) | (not first)) & ((kofs < w) | (not last))
        tabs.append(jnp.where(ok[None], bias * LOG2E, MASK_VALUE))
    return jnp.stack(tabs)


def _attn(proj, proj_kv, sink, bias_tab, batch, seq):
    t = batch * seq
    w = ATT_BLOCK
    nb = seq // w
    kcol, vcol = 0, 1

    def kv(col, delta):
        return pl.BlockSpec((w, ATT_KVW), lambda b, n: (b * nb + jnp.clip(n + delta, 0, nb - 1), col))

    def variant(b, n):
        return ((n == 0).astype(jnp.int32) + 2 * (n == nb - 1).astype(jnp.int32), 0, 0, 0)

    return pl.pallas_call(
        _attn_kernel,
        grid=(batch, nb),
        in_specs=[
            pl.BlockSpec(memory_space=pltpu.SMEM),
            pl.BlockSpec((w, ATT_QW), lambda b, n: (b * nb + n, OFF_AQ // ATT_QW)),
            kv(kcol, -1), kv(kcol, 0), kv(kcol, 1),
            kv(vcol, -1), kv(vcol, 0), kv(vcol, 1),
            pl.BlockSpec((None, ATT_HEADS, w, 3 * w), variant),
        ],
        out_specs=pl.BlockSpec((w, ATT_QW), lambda b, n: (b * nb + n, 0)),
        out_shape=jax.ShapeDtypeStruct((t, ATT_QW), BF16),
        scratch_shapes=[pltpu.VMEM((ATT_HEADS, w, 3 * w), F32), pltpu.VMEM((ATT_HEADS, w, 3 * w), BF16)],
        compiler_params=_params(("arbitrary", "arbitrary"), 32),
        name="win_attn",
    )(sink.astype(F32), proj, proj_kv, proj_kv, proj_kv, proj_kv, proj_kv, proj_kv, bias_tab)


DFT_ROWS = 64


def _load_once(src_hbm, dst_vmem, sem, first):
    @pl.when(first)
    def _():
        cp = pltpu.make_async_copy(src_hbm, dst_vmem, sem)
        cp.start()
        cp.wait()


def _dft_tables(seq):
    n = 2 * seq
    k = jnp.arange(seq, dtype=jnp.int32)[:, None]
    s = jnp.arange(seq, dtype=jnp.int32)[None, :]
    theta = 2.0 * math.pi / n
    ang_a = ((jnp.arange(0, seq, DFT_ROWS, dtype=jnp.int32)[:, None] * s) % n).astype(F32) * theta
    ang_b = ((jnp.arange(DFT_ROWS, dtype=jnp.int32)[:, None] * s) % n).astype(F32) * theta
    ca, sa = jnp.cos(ang_a)[:, None, :], jnp.sin(ang_a)[:, None, :]
    cb, sb = jnp.cos(ang_b)[None], jnp.sin(ang_b)[None]
    cm = (ca * cb - sa * sb).reshape(seq, seq)
    sm = -(sa * cb + ca * sb).reshape(seq, seq)
    nyq = jnp.where(s % 2 == 0, 1.0, -1.0).astype(F32)
    sm = jnp.where(k == 0, nyq, sm)
    f = jnp.concatenate([cm, sm], axis=0).astype(BF16)
    return f, f.T


def _hy_positions(seq):
    t = jnp.linspace(0.0, 1.0, seq, dtype=F32)[:, None]
    w = 2.0 * math.pi * jnp.arange(seq, dtype=F32)[:, None] / seq
    f = jnp.linspace(1e-4, HY_BANDS - 1, HY_BANDS, dtype=F32)[None]
    z = jnp.concatenate([t, jnp.cos(f * w), -jnp.sin(f * w)], axis=-1)
    z = jnp.pad(z, ((0, 0), (0, V7X_LANES - HY_EMB)))
    max_decay = math.log(HY_TARGET) / HY_FAST_DECAY
    min_decay = math.log(HY_TARGET) / HY_SLOW_DECAY
    deltas = jnp.linspace(min_decay, max_decay, HY_W, dtype=F32)
    window = jnp.exp(-t * jnp.abs(deltas))
    return z, window


def _hy_filter_kernel(z_ref, w1_ref, b1_ref, w2_ref, b2_ref, fr_ref, w3_ref, win_ref, h_ref, hid_ref):
    dot = functools.partial(jnp.dot, precision=lax.Precision.HIGHEST, preferred_element_type=F32)

    @pl.when(pl.program_id(0) == 0)
    def _():
        fr = fr_ref[...]
        h = jnp.sin(fr * (dot(z_ref[...], w1_ref[...]) + b1_ref[...]))
        for j in range(HY_INNER):
            h = jnp.sin(fr * (dot(h, w2_ref[j]) + b2_ref[j]))
        hid_ref[...] = h

    h = dot(hid_ref[...], w3_ref[...]) * win_ref[...]
    row = lax.broadcasted_iota(jnp.int32, h.shape, 0)
    backward = pl.program_id(0) >= pl.num_programs(0) // 2
    h_ref[...] = jnp.where((row == 0) & backward, 0.0, h).astype(h_ref.dtype)


def _hy_filters(z, window, w1, b1, w2, b2, freq, w3):
    seq = z.shape[0]
    tn = HY_TC
    per_dir = HY_W // tn
    full = lambda shape: pl.BlockSpec(shape, lambda j: (0,) * len(shape))
    w1p = jnp.pad(w1, ((0, V7X_LANES - HY_EMB), (0, 0)))
    return pl.pallas_call(
        _hy_filter_kernel,
        grid=(2 * per_dir,),
        in_specs=[
            full((seq, V7X_LANES)), full((V7X_LANES, HY_ORDER)), full((1, HY_ORDER)),
            full((HY_INNER, HY_ORDER, HY_ORDER)), full((HY_INNER, 1, HY_ORDER)), full((1, HY_ORDER)),
            pl.BlockSpec((HY_ORDER, tn), lambda j: (0, j)),
            pl.BlockSpec((seq, tn), lambda j: (0, j % per_dir)),
        ],
        out_specs=pl.BlockSpec((seq, tn), lambda j: (0, j)),
        out_shape=jax.ShapeDtypeStruct((seq, 2 * HY_W), BF16),
        scratch_shapes=[pltpu.VMEM((seq, HY_ORDER), F32)],
        compiler_params=_params(("arbitrary",), 32),
        name="hy_filter",
    )(z, w1p, b1.reshape(1, HY_ORDER), w2, b2.reshape(HY_INNER, 1, HY_ORDER), freq.reshape(1, HY_ORDER), w3, window)


def _mm_kernel(a_ref, b_ref, o_ref):
    o_ref[...] = jnp.dot(a_ref[...], b_ref[...], preferred_element_type=F32).astype(o_ref.dtype)


def _mm(a, b, tm, tn, out_dtype):
    m, k = a.shape
    n = b.shape[1]
    return pl.pallas_call(
        _mm_kernel,
        grid=(m // tm, n // tn),
        in_specs=[pl.BlockSpec((tm, k), lambda i, j: (i, 0)), pl.BlockSpec((k, tn), lambda i, j: (0, j))],
        out_specs=pl.BlockSpec((tm, tn), lambda i, j: (i, j)),
        out_shape=jax.ShapeDtypeStruct((m, n), out_dtype),
        compiler_params=_params(("arbitrary", "arbitrary"), 32),
        name="mm",
    )(a, b)


def _hy_spectrum(fmat, hcat, seq):
    spec = _mm(fmat, hcat, min(1024, 2 * seq), 512, F32)
    top, bot = spec[:seq], spec[seq:]
    kr = top[:, :HY_W] + top[:, HY_W:]
    ki = bot[:, :HY_W] - bot[:, HY_W:]
    n = 2 * seq
    first = (jnp.arange(seq) == 0)[:, None]
    pr = jnp.where(first, kr / n, kr * (2.0 / n))
    pi = jnp.where(first, 0.0, ki * (2.0 / n))
    nyq = (bot[:1, :HY_W] + bot[:1, HY_W:]) / n
    return pr, pi, nyq


def _short_conv(x, w_ref, b_ref):
    seq = x.shape[0]
    row = lax.broadcasted_iota(jnp.int32, x.shape, 0)
    prev = jnp.where(row == 0, 0.0, pltpu.roll(x, 1, 0))
    nxt = jnp.where(row == seq - 1, 0.0, pltpu.roll(x, seq - 1, 0))
    return w_ref[0:1, :] * prev + w_ref[1:2, :] * x + w_ref[2:3, :] * nxt + b_ref[...]


def _first_step():
    return (pl.program_id(0) == 0) & (pl.program_id(1) == 0)


def _hy_fwd_kernel(x1_ref, v_ref, w1_ref, b1_ref, wv_ref, bv_ref, f_hbm, pr_ref, pi_ref, nyq_ref, y_ref,
                   f_ref, sem):
    seq = x1_ref.shape[0]
    _load_once(f_hbm, f_ref, sem, _first_step())
    x1 = _short_conv(x1_ref[...].astype(F32), w1_ref, b1_ref)
    v = _short_conv(v_ref[...].astype(F32), wv_ref, bv_ref)
    u = (x1 * v).astype(BF16)
    w = jnp.dot(f_ref[...], u, preferred_element_type=F32)
    a, b = w[:seq], w[seq:]
    pr, pi = pr_ref[...], pi_ref[...]
    row = lax.broadcasted_iota(jnp.int32, pr.shape, 0)
    pd = jnp.where(row == 0, nyq_ref[...], pr)
    y_ref[:seq, :] = (a * pr - b * pi).astype(y_ref.dtype)
    y_ref[seq:, :] = (a * pi + b * pd).astype(y_ref.dtype)


def _hy_inv_kernel(y_ref, x0_ref, x1_ref, v_ref, w0_ref, b0_ref, w1_ref, b1_ref, wv_ref, bv_ref, skip_ref,
                   g_hbm, o_ref, g_ref, sem):
    _load_once(g_hbm, g_ref, sem, _first_step())
    y = jnp.dot(g_ref[...], y_ref[...], preferred_element_type=F32)
    x0 = _short_conv(x0_ref[...].astype(F32), w0_ref, b0_ref)
    x1 = _short_conv(x1_ref[...].astype(F32), w1_ref, b1_ref)
    v = _short_conv(v_ref[...].astype(F32), wv_ref, bv_ref)
    u = x1 * v
    o_ref[...] = (x0 * (y + u * skip_ref[...])).astype(o_ref.dtype)


def _hyena(proj, conv_w, conv_b, skip, fmat, gmat, pr, pi, nyq, batch, seq):
    t = batch * seq
    tc = HY_TC // 2
    nct = HY_W // tc
    conv_b2 = conv_b.reshape(1, 3 * HY_W)

    def xcol(part):
        return pl.BlockSpec((seq, tc), lambda c, b: (b, (OFF_HY + part * HY_W) // tc + c))

    def wcol(part):
        return pl.BlockSpec((HY_SHORT, tc), lambda c, b: (0, part * nct + c))

    def bcol(part):
        return pl.BlockSpec((1, tc), lambda c, b: (0, part * nct + c))

    chan = pl.BlockSpec((seq, tc), lambda c, b: (0, c))
    chan1 = pl.BlockSpec((1, tc), lambda c, b: (0, c))
    whole = pl.BlockSpec(memory_space=pl.ANY)
    dft_scratch = [pltpu.VMEM(fmat.shape, BF16), pltpu.SemaphoreType.DMA(())]
    yspec = pl.BlockSpec((None, 2 * seq, tc), lambda c, b: (b, 0, c))
    yfreq = pl.pallas_call(
        _hy_fwd_kernel,
        grid=(nct, batch),
        in_specs=[xcol(1), xcol(2), wcol(1), bcol(1), wcol(2), bcol(2), whole, chan, chan, chan1],
        out_specs=yspec,
        out_shape=jax.ShapeDtypeStruct((batch, 2 * seq, HY_W), BF16),
        scratch_shapes=dft_scratch,
        compiler_params=_params(("arbitrary", "arbitrary"), 56),
        name="hy_fwd",
    )(proj, proj, conv_w, conv_b2, conv_w, conv_b2, fmat, pr, pi, nyq)
    return pl.pallas_call(
        _hy_inv_kernel,
        grid=(nct, batch),
        in_specs=[yspec, xcol(0), xcol(1), xcol(2), wcol(0), bcol(0), wcol(1), bcol(1), wcol(2), bcol(2), chan1,
                  whole],
        out_specs=pl.BlockSpec((seq, tc), lambda c, b: (b, c)),
        out_shape=jax.ShapeDtypeStruct((t, HY_W), BF16),
        scratch_shapes=[pltpu.VMEM(gmat.shape, BF16), pltpu.SemaphoreType.DMA(())],
        compiler_params=_params(("arbitrary", "arbitrary"), 56),
        name="hy_inv",
    )(yfreq, proj, proj, proj, conv_w, conv_b2, conv_w, conv_b2, conv_w, conv_b2, skip.reshape(1, HY_W), gmat)


def _branch_kernel(ohg_ref, ohy_ref, oat_ref, gates_ref, wb_hbm, m_ref, wb_ref, sem):
    d = D_MODEL
    _load_once(wb_hbm, wb_ref, sem, pl.program_id(0) == 0)
    m = None
    for n, o_ref in enumerate((ohg_ref, ohy_ref, oat_ref)):
        br = jnp.dot(o_ref[...], wb_ref[n], preferred_element_type=F32)
        term = jax.nn.sigmoid(gates_ref[:, n * d:(n + 1) * d].astype(F32)) * br
        m = term if m is None else m + term
    m_ref[...] = m.astype(m_ref.dtype)


def _merge_kernel(m_ref, h_ref, wo_hbm, lng_ref, lnb_ref, rwh_ref, rwl_ref, rb_ref,
                  hmid_ref, hp_ref, idx_ref, gate_ref, wo_ref, sem):
    _load_once(wo_hbm, wo_ref, sem, pl.program_id(0) == 0)
    y = jnp.dot(m_ref[...], wo_ref[...], preferred_element_type=F32)
    hn = _layer_norm_rows(DEEPNORM_ALPHA * h_ref[...] + y, lng_ref[...], lnb_ref[...])
    hmid_ref[...] = hn
    hp_ref[...] = _pack_halves(hn)

    h_hi = hn.astype(BF16)
    h_lo = (hn - h_hi.astype(F32)).astype(BF16)
    logits = (jnp.dot(h_hi, rwh_ref[...], preferred_element_type=F32)
              + jnp.dot(h_lo, rwh_ref[...], preferred_element_type=F32)
              + jnp.dot(h_hi, rwl_ref[...], preferred_element_type=F32)) + rb_ref[...]
    lane = lax.broadcasted_iota(jnp.int32, logits.shape, 1)
    vals, idxs = [], []
    for _ in range(TOP_K):
        mx = jnp.max(logits, axis=-1, keepdims=True)
        ix = jnp.min(jnp.where(logits == mx, lane, ROUTE_LANES), axis=-1, keepdims=True)
        vals.append(mx)
        idxs.append(ix)
        logits = jnp.where(lane == ix, -jnp.inf, logits)
    exps = [jnp.exp(v - vals[0]) for v in vals]
    total = exps[0]
    for e in exps[1:]:
        total = total + e
    gate_out = jnp.zeros(logits.shape, F32)
    idx_out = jnp.zeros(logits.shape, jnp.int32)
    for r in range(TOP_K):
        gate_out = jnp.where(lane == r, exps[r] / total, gate_out)
        idx_out = jnp.where(lane == r, idxs[r], idx_out)
    gate_ref[...] = gate_out
    idx_ref[...] = idx_out


def _merge(o_hg, o_hy, o_at, gates, h, wb, wo, ln_g, ln_b, rw, rb):
    t, d = h.shape
    tm = min(MERGE_TM, t)
    row = lambda width: pl.BlockSpec((tm, width), lambda i: (i, 0))
    const = lambda shape: pl.BlockSpec(shape, lambda i: (0,) * len(shape))
    whole = pl.BlockSpec(memory_space=pl.ANY)
    rwp = jnp.pad(rw, ((0, 0), (0, ROUTE_LANES - N_EXPERTS)))
    rw_hi = rwp.astype(BF16)
    rw_lo = (rwp - rw_hi.astype(F32)).astype(BF16)
    rbp = jnp.pad(rb, (0, ROUTE_LANES - N_EXPERTS), constant_values=MASK_VALUE).reshape(1, ROUTE_LANES)
    m = pl.pallas_call(
        _branch_kernel,
        grid=(t // tm,),
        in_specs=[row(MIX_W), row(MIX_W), row(MIX_W), row(N_BRANCH * d), whole],
        out_specs=row(d),
        out_shape=jax.ShapeDtypeStruct((t, d), BF16),
        scratch_shapes=[pltpu.VMEM((N_BRANCH, MIX_W, d), BF16), pltpu.SemaphoreType.DMA(())],
        compiler_params=_params(("arbitrary",), 56),
        name="branch_merge",
    )(o_hg, o_hy, o_at, gates, wb)
    return pl.pallas_call(
        _merge_kernel,
        grid=(t // tm,),
        in_specs=[row(d), row(d), whole, const((1, d)), const((1, d)),
                  const((d, ROUTE_LANES)), const((d, ROUTE_LANES)), const((1, ROUTE_LANES))],
        out_specs=[row(d), row(d // 2), row(ROUTE_LANES), row(ROUTE_LANES)],
        out_shape=[jax.ShapeDtypeStruct((t, d), F32), jax.ShapeDtypeStruct((t, d // 2), jnp.uint32),
                   jax.ShapeDtypeStruct((t, ROUTE_LANES), jnp.int32), jax.ShapeDtypeStruct((t, ROUTE_LANES), F32)],
        scratch_shapes=[pltpu.VMEM((d, d), BF16), pltpu.SemaphoreType.DMA(())],
        compiler_params=_params(("arbitrary",), 56),
        name="merge",
    )(m, h, wo, ln_g.reshape(1, d), ln_b.reshape(1, d), rw_hi, rw_lo, rbp)


def _route_plan(top_idx, n_tiles):
    e = top_idx.reshape(-1)
    onehot = (e[:, None] == jnp.arange(N_EXPERTS, dtype=jnp.int32)[None, :]).astype(jnp.int32)
    csum = jnp.cumsum(onehot, axis=0)
    rank = jnp.sum(csum * onehot, axis=1) - 1
    counts = csum[-1]
    padded = (counts + MOE_TM - 1) // MOE_TM * MOE_TM
    p_end = jnp.cumsum(padded)
    p_start = p_end - padded
    pos = (p_start[e] + rank).astype(jnp.int32)
    n_used = (p_end[-1] // MOE_TM).astype(jnp.int32)
    tile_start = jnp.arange(n_tiles, dtype=jnp.int32) * MOE_TM
    tile_expert = jnp.minimum(jnp.searchsorted(p_end, tile_start, side='right'), N_EXPERTS - 1).astype(jnp.int32)
    tile_rows = jnp.clip(counts[tile_expert] - (tile_start - p_start[tile_expert]), 0, MOE_TM).astype(jnp.int32)
    return pos, tile_expert, tile_rows, n_used.reshape(1)


def _dispatch_kernel(pos_ref, hp_ref, xs_ref, sem):
    def body(r, carry):
        for k in range(TOP_K):
            dst = pos_ref[0, r * TOP_K + k]
            pltpu.make_async_copy(hp_ref.at[pl.ds(r, 1), :], xs_ref.at[pl.ds(dst, 1), :], sem).start(priority=k % 2)
        return carry

    lax.fori_loop(0, hp_ref.shape[0], body, 0)
    for k in range(TOP_K):
        pltpu.make_async_copy(hp_ref, xs_ref.at[pl.ds(0, hp_ref.shape[0]), :], sem).wait()


def _dispatch(hp, pos, n_slots):
    t, dw = hp.shape
    tq = min(TOK_TILE, t)
    return pl.pallas_call(
        _dispatch_kernel,
        grid=(t // tq,),
        in_specs=[pl.BlockSpec((None, 1, tq * TOP_K), lambda i: (i, 0, 0), memory_space=pltpu.SMEM),
                  pl.BlockSpec((tq, dw), lambda i: (i, 0))],
        out_specs=pl.BlockSpec(memory_space=pl.ANY),
        out_shape=jax.ShapeDtypeStruct((n_slots, dw), jnp.uint32),
        scratch_shapes=[pltpu.SemaphoreType.DMA(())],
        compiler_params=_params(("arbitrary",), 32),
        name="moe_dispatch",
    )(pos.reshape(t // tq, 1, tq * TOP_K), hp)


def _ffn_kernel(te_ref, tr_ref, nu_ref, x_ref, wg_ref, wu_ref, bgu_ref, wdl_ref, wdh_ref, bd_ref,
                y_ref, xb_ref, act_ref):
    i = pl.program_id(0)
    j = pl.program_id(1)
    rows = tr_ref[i]
    active = i < nu_ref[0]
    half = x_ref.shape[1]
    nf = D_FF // MOE_TF
    nsub = (rows + MOE_SUB - 1) // MOE_SUB

    @pl.when(active & (j == 0))
    def _():
        lo, hi = _unpack_halves(x_ref[...])
        keep = lax.broadcasted_iota(jnp.int32, lo.shape, 0) < rows
        xb_ref[:, :half] = jnp.where(keep, lo, 0.0).astype(BF16)
        xb_ref[:, half:] = jnp.where(keep, hi, 0.0).astype(BF16)

    fcol = pl.multiple_of(jnp.minimum(j, nf - 1) * MOE_TF, MOE_TF)
    ncol = pl.multiple_of(jnp.maximum(j - nf, 0) * MOE_TN, MOE_TN)

    for k in range(1, MOE_TM // MOE_SUB + 1):
        m = k * MOE_SUB

        @pl.when(active & (j < nf) & (nsub == k))
        def _():
            xs = xb_ref[:m, :]
            g = jnp.dot(xs, wg_ref[...].astype(BF16), preferred_element_type=F32) + bgu_ref[:, pl.ds(fcol, MOE_TF)]
            u = (jnp.dot(xs, wu_ref[...].astype(BF16), preferred_element_type=F32)
                 + bgu_ref[:, pl.ds(D_FF + fcol, MOE_TF)])
            g = jnp.minimum(g, SWIGLU_LIMIT)
            u = jnp.clip(u, -SWIGLU_LIMIT, SWIGLU_LIMIT)
            act_ref[:m, pl.ds(fcol, MOE_TF)] = ((u + 1.0) * g * jax.nn.sigmoid(SWIGLU_ALPHA * g)).astype(BF16)

        @pl.when(active & (j >= nf) & (nsub == k))
        def _():
            act = act_ref[:m, :]
            lo = (jnp.dot(act, wdl_ref[...].astype(BF16), preferred_element_type=F32)
                  + bd_ref[:, pl.ds(ncol, MOE_TN)])
            hi = (jnp.dot(act, wdh_ref[...].astype(BF16), preferred_element_type=F32)
                  + bd_ref[:, pl.ds(half + ncol, MOE_TN)])
            y_ref[:m, :] = _pack_pair(lo, hi)
            if m < MOE_TM:
                y_ref[m:, :] = jnp.zeros((MOE_TM - m, y_ref.shape[1]), jnp.uint32)


def _ffn(xs, w_gate_up, b_gate_up, w_down, b_down, layer, tile_expert, tile_rows, n_used):
    n_slots, dw = xs.shape
    d = 2 * dw
    n_tiles = n_slots // MOE_TM
    nf = D_FF // MOE_TF
    nn = dw // MOE_TN
    bgu = b_gate_up.reshape(DEPTH, N_EXPERTS, 1, 2 * D_FF)
    bd = b_down.reshape(DEPTH, N_EXPERTS, 1, d)

    def tile(i, nu):
        return jnp.minimum(i, nu[0] - 1)

    def fcol(i, j, nu):
        return jnp.where(i < nu[0], jnp.minimum(j, nf - 1), nf - 1)

    def ncol(i, j, nu):
        return jnp.where(i < nu[0], jnp.maximum(j - nf, 0), nn - 1)

    xmap = lambda i, j, te, tr, nu: (tile(i, nu), 0)
    gmap = lambda i, j, te, tr, nu: (layer, te[tile(i, nu)], 0, fcol(i, j, nu))
    umap = lambda i, j, te, tr, nu: (layer, te[tile(i, nu)], 0, nf + fcol(i, j, nu))
    dlmap = lambda i, j, te, tr, nu: (layer, te[tile(i, nu)], 0, ncol(i, j, nu))
    dhmap = lambda i, j, te, tr, nu: (layer, te[tile(i, nu)], 0, nn + ncol(i, j, nu))
    bmap = lambda i, j, te, tr, nu: (layer, te[tile(i, nu)], 0, 0)
    ymap = lambda i, j, te, tr, nu: (tile(i, nu), ncol(i, j, nu))
    grid_spec = pltpu.PrefetchScalarGridSpec(
        num_scalar_prefetch=3,
        grid=(n_tiles, nf + nn),
        in_specs=[
            pl.BlockSpec((MOE_TM, dw), xmap, pipeline_mode=pl.Buffered(1)),
            pl.BlockSpec((None, None, d, MOE_TF), gmap),
            pl.BlockSpec((None, None, d, MOE_TF), umap),
            pl.BlockSpec((None, None, 1, 2 * D_FF), bmap),
            pl.BlockSpec((None, None, D_FF, MOE_TN), dlmap),
            pl.BlockSpec((None, None, D_FF, MOE_TN), dhmap),
            pl.BlockSpec((None, None, 1, d), bmap),
        ],
        out_specs=pl.BlockSpec((MOE_TM, MOE_TN), ymap),
        scratch_shapes=[pltpu.VMEM((MOE_TM, d), BF16), pltpu.VMEM((MOE_TM, D_FF), BF16)],
    )
    return pl.pallas_call(
        _ffn_kernel,
        grid_spec=grid_spec,
        out_shape=jax.ShapeDtypeStruct((n_slots, dw), jnp.uint32),
        compiler_params=_params(("arbitrary", "arbitrary"), 56),
        name="moe_ffn",
    )(tile_expert, tile_rows, n_used, xs, w_gate_up, w_gate_up, bgu, w_down, w_down, bd)


def _combine_kernel(pos_ref, posn_ref, ys_ref, gate_ref, h_ref, lng_ref, lnb_ref, hout_ref, hb_ref, buf, sem):
    i = pl.program_id(0)
    tq, d = h_ref.shape
    half = d // 2
    slot = i % 2

    def issue(p_ref, s):
        def body(r, carry):
            for k in range(TOP_K):
                src = p_ref[0, r * TOP_K + k]
                pltpu.make_async_copy(ys_ref.at[pl.ds(src, 1), :], buf.at[s, k, pl.ds(r, 1), :],
                                      sem.at[s]).start(priority=k % 2)
            return carry

        lax.fori_loop(0, tq, body, 0, unroll=4)

    @pl.when(i == 0)
    def _():
        issue(pos_ref, 0)

    for s in range(2):
        @pl.when((i + 1 < pl.num_programs(0)) & (slot == 1 - s))
        def _():
            issue(posn_ref, s)

    for k in range(TOP_K):
        pltpu.make_async_copy(ys_ref.at[pl.ds(0, tq), :], buf.at[slot, k], sem.at[slot]).wait()
    gate = gate_ref[...]
    acc_lo = DEEPNORM_ALPHA * h_ref[:, :half]
    acc_hi = DEEPNORM_ALPHA * h_ref[:, half:]
    for k in range(TOP_K):
        lo, hi = _unpack_halves(buf[slot, k])
        acc_lo = acc_lo + gate[:, k:k + 1] * lo
        acc_hi = acc_hi + gate[:, k:k + 1] * hi
    mu = (jnp.sum(acc_lo, axis=-1, keepdims=True) + jnp.sum(acc_hi, axis=-1, keepdims=True)) / d
    acc_lo = acc_lo - mu
    acc_hi = acc_hi - mu
    var = (jnp.sum(acc_lo * acc_lo, axis=-1, keepdims=True) + jnp.sum(acc_hi * acc_hi, axis=-1, keepdims=True)) / d
    inv = lax.rsqrt(var + LN_EPS)
    for sl, acc in ((slice(0, half), acc_lo), (slice(half, d), acc_hi)):
        hn = acc * inv * lng_ref[:, sl] + lnb_ref[:, sl]
        hout_ref[:, sl] = hn
        hb_ref[:, sl] = hn.astype(BF16)


def _combine(ys, pos, gate, h, ln_g, ln_b):
    t, d = h.shape
    tq = min(TOK_TILE, t)
    nt = t // tq
    row = lambda width: pl.BlockSpec((tq, width), lambda i: (i, 0))
    vec = pl.BlockSpec((1, d), lambda i: (0, 0))
    pos3 = pos.reshape(nt, 1, tq * TOP_K)
    return pl.pallas_call(
        _combine_kernel,
        grid=(nt,),
        in_specs=[pl.BlockSpec((None, 1, tq * TOP_K), lambda i: (i, 0, 0), memory_space=pltpu.SMEM),
                  pl.BlockSpec((None, 1, tq * TOP_K), lambda i: (jnp.minimum(i + 1, nt - 1), 0, 0),
                               memory_space=pltpu.SMEM),
                  pl.BlockSpec(memory_space=pl.ANY), row(ROUTE_LANES), row(d), vec, vec],
        out_specs=[row(d), row(d)],
        out_shape=[jax.ShapeDtypeStruct((t, d), F32), jax.ShapeDtypeStruct((t, d), BF16)],
        scratch_shapes=[pltpu.VMEM((2, TOP_K, tq, d // 2), jnp.uint32), pltpu.SemaphoreType.DMA((2,))],
        compiler_params=_params(("arbitrary",), 40),
        name="moe_combine",
    )(pos3, pos3, ys, gate, h, ln_g.reshape(1, d), ln_b.reshape(1, d))


def _hgrn_lower_bound(lb_table, layer):
    p = jax.nn.softmax(lb_table.astype(F32), axis=0)
    return jnp.cumsum(p, axis=0)[layer] - p[0]


def kernel(x, ln_in_g, ln_in_b, w_in, hg_lower_bound, hg_norm_g, hy_conv_w, hy_conv_b, hy_filt_w1, hy_filt_b1,
           hy_filt_w2, hy_filt_b2, hy_filt_freq, hy_filt_w3, hy_skip, att_sink, rel_bias, w_branch, w_out,
           ln_mix_g, ln_mix_b, router_w, router_b, w_gate_up, b_gate_up, w_down, b_down, ln_moe_g, ln_moe_b):
    batch, seq, d = x.shape
    t = batch * seq
    n_tiles = t * TOP_K // MOE_TM + N_EXPERTS
    n_slots = n_tiles * MOE_TM

    fmat, gmat = _dft_tables(seq)
    z_pos, window = _hy_positions(seq)
    bias_tab = _attn_bias_table(rel_bias)

    h, hb = _ln_in(x.reshape(t, d), ln_in_g, ln_in_b)
    for layer in range(DEPTH):
        proj = _inproj(hb, w_in, layer, 0, OFF_AK, PROJ_TN)
        proj_kv = _inproj(hb, w_in, layer, OFF_AK, PROJ_KV_COLS, PROJ_KV_COLS)
        w_gates = lax.slice(w_in, (layer, 0, OFF_GATES), (layer + 1, d, IN_COLS))
        gates = _inproj(hb, w_gates, 0, 0, N_BRANCH * d, PROJ_TN)

        lb = _hgrn_lower_bound(hg_lower_bound, layer)
        o_hg = _hgrn(proj, lb, hg_norm_g[layer], batch, seq)

        hcat = _hy_filters(z_pos, window, hy_filt_w1[layer], hy_filt_b1[layer], hy_filt_w2[layer],
                           hy_filt_b2[layer], hy_filt_freq[layer], hy_filt_w3[layer])
        pr, pi, nyq = _hy_spectrum(fmat, hcat, seq)
        o_hy = _hyena(proj, hy_conv_w[layer], hy_conv_b[layer], hy_skip[layer], fmat, gmat, pr, pi, nyq,
                      batch, seq)

        o_at = _attn(proj, proj_kv, att_sink[layer], bias_tab, batch, seq)

        h_mid, hp, top_idx, gate = _merge(
            o_hg, o_hy, o_at, gates, h, w_branch[layer].astype(BF16), w_out[layer].astype(BF16),
            ln_mix_g[layer], ln_mix_b[layer], router_w[layer], router_b[layer])

        pos, tile_expert, tile_rows, n_used = _route_plan(top_idx[:, :TOP_K], n_tiles)
        xs = _dispatch(hp, pos, n_slots)
        ys = _ffn(xs, w_gate_up, b_gate_up, w_down, b_down, layer, tile_expert, tile_rows, n_used)
        h, hb = _combine(ys, pos, gate, h_mid, ln_moe_g[layer], ln_moe_b[layer])
    return h.reshape(batch, seq, d)
```
